```python
import math
import jax, jax.numpy as jnp
from jax import lax
import numpy as np

D_MODEL = 2048
BATCH = 2
SEQ = 16384
DEPTH = 2

N_HEADS = 8
N_KV_HEADS = 2
HEAD_DIM = 128
GQA_GROUP = N_HEADS // N_KV_HEADS
ATTN_WIDTH = N_HEADS * HEAD_DIM
KV_WIDTH = N_KV_HEADS * HEAD_DIM
Q_BLOCK = 128
ROPE_THETA = 10000.0
ROPE_PAIRS = HEAD_DIM // 4
N_FOURIER_GROUPS = 8
FOURIER_GROUP_CH = 128
FOURIER_WIDTH = N_FOURIER_GROUPS * FOURIER_GROUP_CH
OFF_Q = 0
OFF_K = OFF_Q + ATTN_WIDTH
OFF_V = OFF_K + KV_WIDTH
OFF_F = OFF_V + KV_WIDTH
OFF_GA = OFF_F + FOURIER_WIDTH
OFF_GF = OFF_GA + D_MODEL
IN_WIDTH = OFF_GF + D_MODEL
D_FF = 5632
CONV_W = 3
N_META = 16
GRID_W = 64
NORM_EPS = 1e-6

kernel_name = "hybrid_gqa_fourier_convffn_encoder"


def _rmsnorm(x, g):
    xf = x.astype(jnp.float32)
    y = xf * lax.rsqrt(jnp.mean(xf * xf, axis=-1, keepdims=True) + NORM_EPS)
    return (y * g.astype(jnp.float32)).astype(x.dtype)


def _rope_half(x, cos, sin):
    x1, x2 = jnp.split(x, 2, axis=-1)
    c = cos[:, None, :]
    s = sin[:, None, :]
    return jnp.concatenate([x1 * c - x2 * s, x1 * s + x2 * c], axis=-1)


def _rope_2d(x, cos_r, sin_r, cos_c, sin_c):
    half = HEAD_DIM // 2
    return jnp.concatenate([_rope_half(x[..., :half], cos_r, sin_r),
                            _rope_half(x[..., half:], cos_c, sin_c)], axis=-1)


def _attend_block(qb, k, v):
    scale = 1.0 / math.sqrt(HEAD_DIM)
    s = jnp.einsum('bqkgd,bskd->bkgqs', qb, k, preferred_element_type=jnp.float32) * scale
    p = jax.nn.softmax(s, axis=-1).astype(v.dtype)
    return jnp.einsum('bkgqs,bskd->bqkgd', p, v)


def _dwconv3_centred(u, w, b):
    up = jnp.pad(u, ((0, 0), (1, 1), (0, 0)))
    return up[:, :-2] * w[0] + up[:, 1:-1] * w[1] + up[:, 2:] * w[2] + b


def setup_inputs(seed: int = 0) -> dict:
    key = jax.random.key(seed)
    ks = jax.random.split(key, 16)
    f32 = jnp.float32
    n = lambda k, shape, fan_in: jax.random.normal(k, shape, f32) * (fan_in ** -0.5)
    return {
        "x": jax.random.normal(ks[0], (BATCH, SEQ, D_MODEL), f32),
        "meta_tokens": jax.random.normal(ks[1], (N_META, D_MODEL), f32),
        "norm_mix": 1.0 + 0.05 * jax.random.normal(ks[2], (DEPTH, D_MODEL), f32),
        "norm_ffn": 1.0 + 0.05 * jax.random.normal(ks[3], (DEPTH, D_MODEL), f32),
        "w_in": n(ks[4], (DEPTH, D_MODEL, IN_WIDTH), D_MODEL),
        "b_gate": 0.02 * jax.random.normal(ks[5], (DEPTH, 2 * D_MODEL), f32),
        "q_norm": 1.0 + 0.05 * jax.random.normal(ks[6], (DEPTH, HEAD_DIM), f32),
        "k_norm": 1.0 + 0.05 * jax.random.normal(ks[7], (DEPTH, HEAD_DIM), f32),
        "w_attn_br": n(ks[8], (DEPTH, ATTN_WIDTH, D_MODEL), ATTN_WIDTH),
        "w_four": n(ks[9], (DEPTH, FOURIER_WIDTH, D_MODEL), FOURIER_WIDTH),
        "w_out": n(ks[10], (DEPTH, D_MODEL, D_MODEL), D_MODEL),
        "w_up": n(ks[11], (DEPTH, D_MODEL, 2 * D_FF), D_MODEL),
        "w_conv": n(ks[12], (DEPTH, CONV_W, D_FF), CONV_W),
        "b_conv": 0.02 * jax.random.normal(ks[13], (DEPTH, D_FF), f32),
        "w_down": n(ks[14], (DEPTH, D_FF, D_MODEL), D_FF),
    }


def reference(x, meta_tokens, norm_mix, norm_ffn, w_in, b_gate, q_norm, k_norm,
              w_attn_br, w_four, w_out, w_up, w_conv, b_conv, w_down):
    B, n_tok, D = x.shape
    ROWS = n_tok // GRID_W
    L = n_tok + N_META
    n_blocks = n_tok // Q_BLOCK

    f32 = jnp.float32
    rows = jnp.repeat(jnp.arange(ROWS, dtype=f32), GRID_W)
    cols = jnp.tile(jnp.arange(GRID_W, dtype=f32), ROWS)
    zeros_meta = jnp.zeros((N_META,), f32)
    pos_r = jnp.concatenate([zeros_meta, rows])
    pos_c = jnp.concatenate([zeros_meta, cols])
    inv_freq = 1.0 / (ROPE_THETA ** (jnp.arange(ROPE_PAIRS, dtype=f32) / ROPE_PAIRS))
    ang_r = pos_r[:, None] * inv_freq[None, :]
    ang_c = pos_c[:, None] * inv_freq[None, :]
    cos_r, sin_r = jnp.cos(ang_r).astype(x.dtype), jnp.sin(ang_r).astype(x.dtype)
    cos_c, sin_c = jnp.cos(ang_c).astype(x.dtype), jnp.sin(ang_c).astype(x.dtype)

    meta = jnp.broadcast_to(meta_tokens.astype(x.dtype)[None], (B, N_META, D))
    h_stream = jnp.concatenate([meta, x], axis=1)

    for i in range(DEPTH):
        h = _rmsnorm(h_stream, norm_mix[i])
        proj = h @ w_in[i]
        q = proj[..., OFF_Q:OFF_K].reshape(B, L, N_HEADS, HEAD_DIM)
        k = proj[..., OFF_K:OFF_V].reshape(B, L, N_KV_HEADS, HEAD_DIM)
        v = proj[..., OFF_V:OFF_F].reshape(B, L, N_KV_HEADS, HEAD_DIM)
        f = proj[..., OFF_F:OFF_GA]
        gates = proj[..., OFF_GA:] + b_gate[i]
        g_attn = jax.nn.sigmoid(gates[..., :D])
        g_four = jax.nn.sigmoid(gates[..., D:])

        q = _rope_2d(_rmsnorm(q, q_norm[i]), cos_r, sin_r, cos_c, sin_c)
        k = _rope_2d(_rmsnorm(k, k_norm[i]), cos_r, sin_r, cos_c, sin_c)
        q = q.reshape(B, L, N_KV_HEADS, GQA_GROUP, HEAD_DIM)
        o_meta = _attend_block(q[:, :N_META], k, v).reshape(B, N_META, ATTN_WIDTH)
        q_blocks = q[:, N_META:].reshape(B, n_blocks, Q_BLOCK, N_KV_HEADS, GQA_GROUP, HEAD_DIM)
        q_blocks = jnp.moveaxis(q_blocks, 1, 0)
        o_real = lax.map(lambda qb: _attend_block(qb, k, v), q_blocks)
        o_real = jnp.moveaxis(o_real, 0, 1).reshape(B, n_tok, ATTN_WIDTH)
        attn = jnp.concatenate([o_meta, o_real], axis=1)
        a_br = attn @ w_attn_br[i]

        fg = f.reshape(B, L, N_FOURIER_GROUPS, FOURIER_GROUP_CH).astype(jnp.float32)
        fr = jnp.fft.fft2(fg, axes=(1, 3), norm="ortho").real.astype(x.dtype)
        s_br = fr.reshape(B, L, FOURIER_WIDTH) @ w_four[i]

        merged = g_attn * a_br + g_four * s_br
        h_stream = h_stream + merged @ w_out[i]

        h2 = _rmsnorm(h_stream, norm_ffn[i])
        up = h2 @ w_up[i]
        u_gate = _dwconv3_centred(up[..., :D_FF], w_conv[i], b_conv[i])
        u_val = up[..., D_FF:]
        h_stream = h_stream + (jax.nn.silu(u_gate) * u_val) @ w_down[i]

    return h_stream[:, N_META:]
```

```python
import functools
import math

import jax
import jax.numpy as jnp
import numpy as np
from jax import lax
from jax.experimental import pallas as pl
from jax.experimental.pallas import tpu as pltpu

F32 = jnp.float32
BF16 = jnp.bfloat16

D_MODEL = 2048
N_HEADS = 8
N_KV_HEADS = 2
HEAD_DIM = 128
GQA_GROUP = N_HEADS // N_KV_HEADS
ATTN_WIDTH = N_HEADS * HEAD_DIM
KV_WIDTH = N_KV_HEADS * HEAD_DIM
N_FOURIER_GROUPS = 8
FOURIER_GROUP_CH = 128
FOURIER_WIDTH = N_FOURIER_GROUPS * FOURIER_GROUP_CH
QKVF_WIDTH = ATTN_WIDTH + 2 * KV_WIDTH + FOURIER_WIDTH
OFF_K = ATTN_WIDTH
OFF_V = OFF_K + KV_WIDTH
OFF_F = OFF_V + KV_WIDTH
D_FF = 5632
N_META = 16
GRID_W = 64
NORM_EPS = 1e-6
ROPE_THETA = 10000.0
ROPE_PAIRS = HEAD_DIM // 4

DFT_N2 = 80
ROW_ALIGN = 640
CONV_HALO = 16
MASK_VALUE = -1e30
VMEM_LIMIT = 56 * 1024 * 1024


def _cparams(*sem):
    return pltpu.CompilerParams(dimension_semantics=sem, vmem_limit_bytes=VMEM_LIMIT)


def _rms(x):
    return x * lax.rsqrt(jnp.mean(x * x, axis=-1, keepdims=True) + NORM_EPS)


def _inproj_kernel(x_ref, g_ref, w_ref, qn_ref, kn_ref, cos_ref, sin_ref, cs_ref,
                   q_ref, k_ref, v_ref, zr_ref, zi_ref):
    h = (_rms(x_ref[...]) * g_ref[...]).astype(BF16)
    proj = jnp.dot(h, w_ref[...], preferred_element_type=F32)
    cos = cos_ref[...]
    sin = sin_ref[...]
    lane = lax.broadcasted_iota(jnp.int32, cos.shape, 1)
    first_half = (lane % (HEAD_DIM // 2)) < ROPE_PAIRS

    def norm_rope(t, gain):
        t = _rms(t) * gain
        partner = jnp.where(first_half,
                            pltpu.roll(t, HEAD_DIM - ROPE_PAIRS, 1),
                            pltpu.roll(t, ROPE_PAIRS, 1))
        return t * cos + partner * sin

    scale = 1.0 / math.sqrt(HEAD_DIM)
    for hh in range(N_HEADS):
        c = hh * HEAD_DIM
        q_ref[:, c:c + HEAD_DIM] = (norm_rope(proj[:, c:c + HEAD_DIM], qn_ref[...]) * scale).astype(BF16)
    for hh in range(N_KV_HEADS):
        c = hh * HEAD_DIM
        k_ref[:, c:c + HEAD_DIM] = norm_rope(proj[:, OFF_K + c:OFF_K + c + HEAD_DIM], kn_ref[...]).astype(BF16)
    v_ref[...] = proj[:, OFF_V:OFF_F].astype(BF16)
    cs = cs_ref[...]
    for gg in range(N_FOURIER_GROUPS):
        c = gg * FOURIER_GROUP_CH
        z = jnp.dot(proj[:, OFF_F + c:OFF_F + c + FOURIER_GROUP_CH].astype(BF16), cs,
                    preferred_element_type=F32)
        zr_ref[:, c:c + FOURIER_GROUP_CH] = z[:, :FOURIER_GROUP_CH].astype(BF16)
        zi_ref[:, c:c + FOURIER_GROUP_CH] = z[:, FOURIER_GROUP_CH:].astype(BF16)


def _inproj(hs, gain, w_qkvf, qn, kn, cos_t, sin_t, cs, *, tm, tiles_per_batch):
    R = hs.shape[0]
    row = lambda i: (i, 0)
    fixed = lambda i: (0, 0)
    tab = lambda i: (i % tiles_per_batch, 0)
    return pl.pallas_call(
        _inproj_kernel,
        grid=(R // tm,),
        in_specs=[
            pl.BlockSpec((tm, D_MODEL), row),
            pl.BlockSpec((1, D_MODEL), fixed),
            pl.BlockSpec((D_MODEL, QKVF_WIDTH), fixed),
            pl.BlockSpec((1, HEAD_DIM), fixed),
            pl.BlockSpec((1, HEAD_DIM), fixed),
            pl.BlockSpec((tm, HEAD_DIM), tab),
            pl.BlockSpec((tm, HEAD_DIM), tab),
            pl.BlockSpec((FOURIER_GROUP_CH, 2 * FOURIER_GROUP_CH), fixed),
        ],
        out_specs=[
            pl.BlockSpec((tm, ATTN_WIDTH), row),
            pl.BlockSpec((tm, KV_WIDTH), row),
            pl.BlockSpec((tm, KV_WIDTH), row),
            pl.BlockSpec((tm, FOURIER_WIDTH), row),
            pl.BlockSpec((tm, FOURIER_WIDTH), row),
        ],
        out_shape=[
            jax.ShapeDtypeStruct((R, ATTN_WIDTH), BF16),
            jax.ShapeDtypeStruct((R, KV_WIDTH), BF16),
            jax.ShapeDtypeStruct((R, KV_WIDTH), BF16),
            jax.ShapeDtypeStruct((R, FOURIER_WIDTH), BF16),
            jax.ShapeDtypeStruct((R, FOURIER_WIDTH), BF16),
        ],
        compiler_params=_cparams("parallel"),
        name="inproj",
    )(hs, gain, w_qkvf, qn, kn, cos_t, sin_t, cs)


def _attn_kernel(q_ref, k_ref, v_ref, o_ref, m_ref, l_ref, acc_ref, *, tk, n_full, tail_valid):
    m_ref[...] = jnp.full(m_ref.shape, MASK_VALUE, F32)
    l_ref[...] = jnp.zeros(l_ref.shape, F32)
    acc_ref[...] = jnp.zeros(acc_ref.shape, F32)

    def step(j, masked):
        start = pl.multiple_of(j * tk, tk)
        kj = k_ref[0, pl.ds(start, tk), :]
        vj = v_ref[0, pl.ds(start, tk), :]
        for g in range(GQA_GROUP):
            c = g * HEAD_DIM
            s = lax.dot_general(q_ref[0, :, c:c + HEAD_DIM], kj, (((1,), (1,)), ((), ())),
                                preferred_element_type=F32)
            if masked:
                col = lax.broadcasted_iota(jnp.int32, s.shape, 1)
                s = jnp.where(col < tail_valid, s, MASK_VALUE)
            m_prev = m_ref[g]
            m_new = jnp.maximum(m_prev, jnp.max(s, axis=-1, keepdims=True))
            alpha = jnp.exp(m_prev - m_new)
            p = jnp.exp(s - m_new)
            l_ref[g] = alpha * l_ref[g] + jnp.sum(p, axis=-1, keepdims=True)
            acc_ref[g] = alpha * acc_ref[g] + jnp.dot(p.astype(BF16), vj, preferred_element_type=F32)
            m_ref[g] = m_new

    if n_full > 0:
        def body(j, carry):
            step(j, False)
            return carry
        lax.fori_loop(0, n_full, body, 0)
    if tail_valid > 0:
        step(n_full, True)
    for g in range(GQA_GROUP):
        c = g * HEAD_DIM
        o_ref[0, :, c:c + HEAD_DIM] = (acc_ref[g] / l_ref[g]).astype(BF16)


def _attention(q, k, v, *, L, tq, tk):
    B, LP, _ = q.shape
    gw = GQA_GROUP * HEAD_DIM
    n_full = L // tk
    tail_valid = L - n_full * tk
    kern = functools.partial(_attn_kernel, tk=tk, n_full=n_full, tail_valid=tail_valid)
    return pl.pallas_call(
        kern,
        grid=(B, N_KV_HEADS, LP // tq),
        in_specs=[
            pl.BlockSpec((1, tq, gw), lambda b, h, i: (b, i, h)),
            pl.BlockSpec((1, LP, HEAD_DIM), lambda b, h, i: (b, 0, h)),
            pl.BlockSpec((1, LP, HEAD_DIM), lambda b, h, i: (b, 0, h)),
        ],
        out_specs=pl.BlockSpec((1, tq, gw), lambda b, h, i: (b, i, h)),
        out_shape=jax.ShapeDtypeStruct((B, LP, ATTN_WIDTH), BF16),
        scratch_shapes=[
            pltpu.VMEM((GQA_GROUP, tq, 1), F32),
            pltpu.VMEM((GQA_GROUP, tq, 1), F32),
            pltpu.VMEM((GQA_GROUP, tq, HEAD_DIM), F32),
        ],
        compiler_params=_cparams("parallel", "parallel", "parallel"),
        name="attention",
    )(q, k, v)


def _dft_a_kernel(zr_ref, zi_ref, fr_ref, fi_ref, tc_ref, ts_ref, ar_ref, ai_ref, *, t2, n1p):
    a = (jnp.dot(fr_ref[...], zr_ref[0], preferred_element_type=F32)
         + jnp.dot(fi_ref[...], zi_ref[0], preferred_element_type=F32))
    for n in range(t2):
        tc = tc_ref[n]
        ts = ts_ref[n]
        for gg in range(N_FOURIER_GROUPS):
            c = n * FOURIER_WIDTH + gg * FOURIER_GROUP_CH
            o = gg * FOURIER_GROUP_CH
            re = a[:n1p, c:c + FOURIER_GROUP_CH]
            im = a[n1p:, c:c + FOURIER_GROUP_CH]
            ar_ref[0, n, :, o:o + FOURIER_GROUP_CH] = (re * tc + im * ts).astype(BF16)
            ai_ref[0, n, :, o:o + FOURIER_GROUP_CH] = (im * tc - re * ts).astype(BF16)


def _dft_b_kernel(ar_ref, ai_ref, fc_ref, fs_ref, o_ref):
    x = (jnp.dot(fc_ref[...], ar_ref[0], preferred_element_type=F32)
         + jnp.dot(fs_ref[...], ai_ref[0], preferred_element_type=F32))
    o_ref[0] = x.astype(BF16)


def _fourier(zr, zi, consts, *, B, L, LP, t2, tk1):
    n1 = L // DFT_N2
    n1p = LP // DFT_N2
    f1r, f1i, tc, ts, f2c, f2s = consts
    zr3 = zr.reshape(B, n1p, DFT_N2 * FOURIER_WIDTH)
    zi3 = zi.reshape(B, n1p, DFT_N2 * FOURIER_WIDTH)
    zspec = pl.BlockSpec((1, n1p, t2 * FOURIER_WIDTH), lambda b, j: (b, 0, j))
    fspec = pl.BlockSpec((2 * n1p, n1p), lambda b, j: (0, 0))
    tspec = pl.BlockSpec((t2, n1p, FOURIER_GROUP_CH), lambda b, j: (j, 0, 0))
    aspec = pl.BlockSpec((1, t2, n1p, FOURIER_WIDTH), lambda b, j: (b, j, 0, 0))
    ashape = jax.ShapeDtypeStruct((B, DFT_N2, n1p, FOURIER_WIDTH), BF16)
    ar, ai = pl.pallas_call(
        functools.partial(_dft_a_kernel, t2=t2, n1p=n1p),
        grid=(B, DFT_N2 // t2),
        in_specs=[zspec, zspec, fspec, fspec, tspec, tspec],
        out_specs=[aspec, aspec],
        out_shape=[ashape, ashape],
        compiler_params=_cparams("parallel", "parallel"),
        name="dft_rows",
    )(zr3, zi3, f1r, f1i, tc, ts)
    ar3 = ar.reshape(B, DFT_N2, n1p * FOURIER_WIDTH)
    ai3 = ai.reshape(B, DFT_N2, n1p * FOURIER_WIDTH)
    bspec = pl.BlockSpec((1, DFT_N2, tk1 * FOURIER_WIDTH), lambda b, j: (b, 0, j))
    cspec = pl.BlockSpec((DFT_N2, DFT_N2), lambda b, j: (0, 0))
    out = pl.pallas_call(
        _dft_b_kernel,
        grid=(B, n1 // tk1),
        in_specs=[bspec, bspec, cspec, cspec],
        out_specs=bspec,
        out_shape=jax.ShapeDtypeStruct((B, DFT_N2, n1 * FOURIER_WIDTH), BF16),
        compiler_params=_cparams("parallel", "parallel"),
        name="dft_cols",
    )(ar3, ai3, f2c, f2s)
    fr = out.reshape(B, L, FOURIER_WIDTH)
    return jnp.pad(fr, ((0, 0), (0, LP - L), (0, 0))).reshape(B * LP, FOURIER_WIDTH)


def _dft_constants(L, LP):
    n1 = L // DFT_N2
    n1p = LP // DFT_N2
    k = np.arange(n1)
    ang1 = 2.0 * np.pi * np.outer(k, k) / n1
    c1, s1 = np.cos(ang1), np.sin(ang1)
    f1r = np.zeros((2 * n1p, n1p))
    f1i = np.zeros((2 * n1p, n1p))
    f1r[:n1, :n1] = c1
    f1r[n1p:n1p + n1, :n1] = -s1
    f1i[:n1, :n1] = s1
    f1i[n1p:n1p + n1, :n1] = c1
    n2 = np.arange(DFT_N2)
    angt = 2.0 * np.pi * ((n2[:, None] * k[None, :]) % L) / L
    tc = np.zeros((DFT_N2, n1p, FOURIER_GROUP_CH))
    ts = np.zeros((DFT_N2, n1p, FOURIER_GROUP_CH))
    tc[:, :n1, :] = np.cos(angt)[:, :, None]
    ts[:, :n1, :] = np.sin(angt)[:, :, None]
    ang2 = 2.0 * np.pi * np.outer(n2, n2) / DFT_N2
    f2c = np.cos(ang2) / math.sqrt(L)
    f2s = np.sin(ang2) / math.sqrt(L)
    c = np.arange(FOURIER_GROUP_CH)
    angc = 2.0 * np.pi * np.outer(c, c) / FOURIER_GROUP_CH
    cs = np.concatenate([np.cos(angc), -np.sin(angc)], axis=1) / math.sqrt(FOURIER_GROUP_CH)
    dft = (jnp.asarray(f1r, BF16), jnp.asarray(f1i, BF16), jnp.asarray(tc, F32), jnp.asarray(ts, F32),
           jnp.asarray(f2c, BF16), jnp.asarray(f2s, BF16))
    return jnp.asarray(cs, BF16), dft


def _mix_kernel(x_ref, g_ref, attn_ref, four_ref, wga_ref, wgf_ref, bga_ref, bgf_ref,
                wa_ref, wf_ref, wo_ref, o_ref, h_ref, *, tm, L, LP):
    j = pl.program_id(1)

    @pl.when(j == 0)
    def _():
        x = x_ref[...]
        h_ref[...] = (_rms(x) * g_ref[...]).astype(BF16)
        o_ref[...] = x

    h = h_ref[...]
    g_attn = jax.nn.sigmoid(jnp.dot(h, wga_ref[...], preferred_element_type=F32) + bga_ref[...])
    g_four = jax.nn.sigmoid(jnp.dot(h, wgf_ref[...], preferred_element_type=F32) + bgf_ref[...])
    a_br = jnp.dot(attn_ref[...], wa_ref[...], preferred_element_type=F32)
    s_br = jnp.dot(four_ref[...], wf_ref[...], preferred_element_type=F32)
    merged = (g_attn * a_br + g_four * s_br).astype(BF16)
    o_ref[...] += jnp.dot(merged, wo_ref[...], preferred_element_type=F32)

    @pl.when(j == pl.num_programs(1) - 1)
    def _():
        r = pl.program_id(0) * tm + lax.broadcasted_iota(jnp.int32, (tm, 1), 0)
        o_ref[...] = jnp.where((r % LP) < L, o_ref[...], 0.0)


def _mix(hs, gain, attn, four, w_gates, b_gates, w_a, w_f, w_o, *, tm, tn, L, LP):
    R = hs.shape[0]
    nj = D_MODEL // tn
    row = lambda i, j: (i, 0)
    return pl.pallas_call(
        functools.partial(_mix_kernel, tm=tm, L=L, LP=LP),
        grid=(R // tm, nj),
        in_specs=[
            pl.BlockSpec((tm, D_MODEL), row),
            pl.BlockSpec((1, D_MODEL), lambda i, j: (0, 0)),
            pl.BlockSpec((tm, ATTN_WIDTH), row),
            pl.BlockSpec((tm, FOURIER_WIDTH), row),
            pl.BlockSpec((D_MODEL, tn), lambda i, j: (0, j)),
            pl.BlockSpec((D_MODEL, tn), lambda i, j: (0, j + nj)),
            pl.BlockSpec((1, tn), lambda i, j: (0, j)),
            pl.BlockSpec((1, tn), lambda i, j: (0, j + nj)),
            pl.BlockSpec((ATTN_WIDTH, tn), lambda i, j: (0, j)),
            pl.BlockSpec((FOURIER_WIDTH, tn), lambda i, j: (0, j)),
            pl.BlockSpec((tn, D_MODEL), lambda i, j: (j, 0)),
        ],
        out_specs=pl.BlockSpec((tm, D_MODEL), row),
        out_shape=jax.ShapeDtypeStruct((R, D_MODEL), F32),
        scratch_shapes=[pltpu.VMEM((tm, D_MODEL), BF16)],
        compiler_params=_cparams("parallel", "arbitrary"),
        name="mix_out",
    )(hs, gain, attn, four, w_gates, w_gates, b_gates, b_gates, w_a, w_f, w_o)


def _ffn_kernel(x_ref, xp_ref, xn_ref, g_ref, wg_ref, wv_ref, wc_ref, bc_ref, wd_ref,
                o_ref, h_ref, *, tm):
    i = pl.program_id(0)
    j = pl.program_id(1)
    H = CONV_HALO

    @pl.when(j == 0)
    def _():
        x = x_ref[...]
        g = g_ref[...]
        h_ref[H:H + tm, :] = (_rms(x) * g).astype(BF16)
        hp = jnp.where(i > 0, _rms(xp_ref[...]) * g, 0.0)
        hn = jnp.where(i < pl.num_programs(0) - 1, _rms(xn_ref[...]) * g, 0.0)
        h_ref[0:H, :] = hp.astype(BF16)
        h_ref[H + tm:, :] = hn.astype(BF16)
        o_ref[...] = x

    h = h_ref[...]
    up_g = jnp.dot(h, wg_ref[...], preferred_element_type=F32)
    n = tm + 2 * H
    wc = wc_ref[...]
    prev = pltpu.roll(up_g, 1, 0)[H:H + tm]
    nxt = pltpu.roll(up_g, n - 1, 0)[H:H + tm]
    u = prev * wc[0:1] + up_g[H:H + tm] * wc[1:2] + nxt * wc[2:3] + bc_ref[...]
    val = jnp.dot(h[H:H + tm], wv_ref[...], preferred_element_type=F32)
    act = (u * jax.nn.sigmoid(u) * val).astype(BF16)
    o_ref[...] += jnp.dot(act, wd_ref[...], preferred_element_type=F32)


def _ffn(hs, gain, w_up, w_conv, b_conv, w_down, *, tm, tf):
    R = hs.shape[0]
    nj = D_FF // tf
    hb = tm // CONV_HALO
    last_hb = R // CONV_HALO - 1
    row = lambda i, j: (i, 0)
    return pl.pallas_call(
        functools.partial(_ffn_kernel, tm=tm),
        grid=(R // tm, nj),
        in_specs=[
            pl.BlockSpec((tm, D_MODEL), row),
            pl.BlockSpec((CONV_HALO, D_MODEL), lambda i, j: (jnp.maximum(i * hb - 1, 0), 0)),
            pl.BlockSpec((CONV_HALO, D_MODEL), lambda i, j: (jnp.minimum((i + 1) * hb, last_hb), 0)),
            pl.BlockSpec((1, D_MODEL), lambda i, j: (0, 0)),
            pl.BlockSpec((D_MODEL, tf), lambda i, j: (0, j)),
            pl.BlockSpec((D_MODEL, tf), lambda i, j: (0, j + nj)),
            pl.BlockSpec((3, tf), lambda i, j: (0, j)),
            pl.BlockSpec((1, tf), lambda i, j: (0, j)),
            pl.BlockSpec((tf, D_MODEL), lambda i, j: (j, 0)),
        ],
        out_specs=pl.BlockSpec((tm, D_MODEL), row),
        out_shape=jax.ShapeDtypeStruct((R, D_MODEL), F32),
        scratch_shapes=[pltpu.VMEM((tm + 2 * CONV_HALO, D_MODEL), BF16)],
        compiler_params=_cparams("parallel", "arbitrary"),
        name="ffn",
    )(hs, hs, hs, gain, w_up, w_up, w_conv, b_conv, w_down)


def _rope_tables(n_tok, LP):
    rows = n_tok // GRID_W
    pad = LP - N_META - n_tok
    z_meta = jnp.zeros((N_META,), F32)
    z_pad = jnp.zeros((pad,), F32)
    pos_r = jnp.concatenate([z_meta, jnp.repeat(jnp.arange(rows, dtype=F32), GRID_W), z_pad])
    pos_c = jnp.concatenate([z_meta, jnp.tile(jnp.arange(GRID_W, dtype=F32), rows), z_pad])
    inv_freq = 1.0 / (ROPE_THETA ** (jnp.arange(ROPE_PAIRS, dtype=F32) / ROPE_PAIRS))
    ang_r = pos_r[:, None] * inv_freq[None, :]
    ang_c = pos_c[:, None] * inv_freq[None, :]
    cos_t = jnp.concatenate([jnp.cos(ang_r), jnp.cos(ang_r), jnp.cos(ang_c), jnp.cos(ang_c)], axis=-1)
    sin_t = jnp.concatenate([-jnp.sin(ang_r), jnp.sin(ang_r), -jnp.sin(ang_c), jnp.sin(ang_c)], axis=-1)
    return cos_t, sin_t


def _tiles(LP):
    big = 1280 if LP % 1280 == 0 else ROW_ALIGN
    return dict(tm_in=ROW_ALIGN, tq=256 if LP % 256 == 0 else 128, tk=big, tm_mix=ROW_ALIGN, tn_mix=256,
                tm_ffn=ROW_ALIGN, tf=512)


def kernel(x, meta_tokens, norm_mix, norm_ffn, w_in, b_gate, q_norm, k_norm, w_attn_br, w_four, w_out,
           w_up, w_conv, b_conv, w_down):
    B, n_tok, D = x.shape
    depth = w_in.shape[0]
    L = n_tok + N_META
    assert D == D_MODEL and n_tok % GRID_W == 0 and L % DFT_N2 == 0
    LP = -(-L // ROW_ALIGN) * ROW_ALIGN
    R = B * LP
    t = _tiles(LP)
    n1 = L // DFT_N2
    tk1 = 5 if n1 % 5 == 0 else 1

    cos_t, sin_t = _rope_tables(n_tok, LP)
    cs, dft = _dft_constants(L, LP)

    meta = jnp.broadcast_to(meta_tokens.astype(x.dtype)[None], (B, N_META, D))
    hs = jnp.concatenate([meta, x, jnp.zeros((B, LP - L, D), x.dtype)], axis=1).reshape(R, D)

    for i in range(depth):
        w_qkvf = w_in[i, :, :QKVF_WIDTH].astype(BF16)
        w_gates = w_in[i, :, QKVF_WIDTH:].astype(BF16)
        q, k, v, zr, zi = _inproj(hs, norm_mix[i][None], w_qkvf, q_norm[i][None], k_norm[i][None],
                                  cos_t, sin_t, cs, tm=t["tm_in"], tiles_per_batch=LP // t["tm_in"])
        attn = _attention(q.reshape(B, LP, ATTN_WIDTH), k.reshape(B, LP, KV_WIDTH),
                          v.reshape(B, LP, KV_WIDTH), L=L, tq=t["tq"], tk=t["tk"]).reshape(R, ATTN_WIDTH)
        four = _fourier(zr, zi, dft, B=B, L=L, LP=LP, t2=4, tk1=tk1)
        hs = _mix(hs, norm_mix[i][None], attn, four, w_gates, b_gate[i][None],
                  w_attn_br[i].astype(BF16), w_four[i].astype(BF16), w_out[i].astype(BF16),
                  tm=t["tm_mix"], tn=t["tn_mix"], L=L, LP=LP)
        hs = _ffn(hs, norm_ffn[i][None], w_up[i].astype(BF16), w_conv[i], b_conv[i][None],
                  w_down[i].astype(BF16), tm=t["tm_ffn"], tf=t["tf"])

    return hs.reshape(B, LP, D)[:, N_META:L]
```

```python
import functools
import math

import jax
import jax.numpy as jnp
import numpy as np
from jax import lax
from jax.experimental import pallas as pl
from jax.experimental.pallas import tpu as pltpu

F32 = jnp.float32
BF16 = jnp.bfloat16

D_MODEL = 2048
N_HEADS = 8
N_KV_HEADS = 2
HEAD_DIM = 128
GQA_GROUP = N_HEADS // N_KV_HEADS
ATTN_WIDTH = N_HEADS * HEAD_DIM
KV_WIDTH = N_KV_HEADS * HEAD_DIM
N_FOURIER_GROUPS = 8
FOURIER_GROUP_CH = 128
FOURIER_WIDTH = N_FOURIER_GROUPS * FOURIER_GROUP_CH
QKVF_WIDTH = ATTN_WIDTH + 2 * KV_WIDTH + FOURIER_WIDTH
OFF_K = ATTN_WIDTH
OFF_V = OFF_K + KV_WIDTH
OFF_F = OFF_V + KV_WIDTH
D_FF = 5632
N_META = 16
GRID_W = 64
NORM_EPS = 1e-6
ROPE_THETA = 10000.0
ROPE_PAIRS = HEAD_DIM // 4

DFT_N2 = 80
ROW_ALIGN = 640
CONV_HALO = 16
MASK_VALUE = -1e30
VMEM_LIMIT = 56 * 1024 * 1024


def _cparams(*sem):
    return pltpu.CompilerParams(dimension_semantics=sem, vmem_limit_bytes=VMEM_LIMIT)


def _rms(x):
    return x * lax.rsqrt(jnp.mean(x * x, axis=-1, keepdims=True) + NORM_EPS)


def _inproj_kernel(x_ref, g_ref, w_ref, qn_ref, kn_ref, cos_ref, sin_ref, cs_ref,
                   q_ref, k_ref, v_ref, zr_ref, zi_ref):
    h = (_rms(x_ref[...]) * g_ref[...]).astype(BF16)
    proj = jnp.dot(h, w_ref[...], preferred_element_type=F32)
    cos = cos_ref[...]
    sin = sin_ref[...]
    lane = lax.broadcasted_iota(jnp.int32, cos.shape, 1)
    first_half = (lane % (HEAD_DIM // 2)) < ROPE_PAIRS

    def norm_rope(t, gain):
        t = _rms(t) * gain
        partner = jnp.where(first_half,
                            pltpu.roll(t, HEAD_DIM - ROPE_PAIRS, 1),
                            pltpu.roll(t, ROPE_PAIRS, 1))
        return t * cos + partner * sin

    scale = math.log2(math.e) / math.sqrt(HEAD_DIM)
    for hh in range(N_HEADS):
        c = hh * HEAD_DIM
        q_ref[:, c:c + HEAD_DIM] = (norm_rope(proj[:, c:c + HEAD_DIM], qn_ref[...]) * scale).astype(BF16)
    for hh in range(N_KV_HEADS):
        c = hh * HEAD_DIM
        k_ref[:, c:c + HEAD_DIM] = norm_rope(proj[:, OFF_K + c:OFF_K + c + HEAD_DIM], kn_ref[...]).astype(BF16)
    v_ref[...] = proj[:, OFF_V:OFF_F].astype(BF16)
    cs = cs_ref[...]
    for gg in range(N_FOURIER_GROUPS):
        c = gg * FOURIER_GROUP_CH
        z = jnp.dot(proj[:, OFF_F + c:OFF_F + c + FOURIER_GROUP_CH].astype(BF16), cs,
                    preferred_element_type=F32)
        zr_ref[:, c:c + FOURIER_GROUP_CH] = z[:, :FOURIER_GROUP_CH].astype(BF16)
        zi_ref[:, c:c + FOURIER_GROUP_CH] = z[:, FOURIER_GROUP_CH:].astype(BF16)


def _inproj(hs, gain, w_qkvf, qn, kn, cos_t, sin_t, cs, *, tm, tiles_per_batch):
    R = hs.shape[0]
    row = lambda i: (i, 0)
    fixed = lambda i: (0, 0)
    tab = lambda i: (i % tiles_per_batch, 0)
    return pl.pallas_call(
        _inproj_kernel,
        grid=(R // tm,),
        in_specs=[
            pl.BlockSpec((tm, D_MODEL), row),
            pl.BlockSpec((1, D_MODEL), fixed),
            pl.BlockSpec((D_MODEL, QKVF_WIDTH), fixed),
            pl.BlockSpec((1, HEAD_DIM), fixed),
            pl.BlockSpec((1, HEAD_DIM), fixed),
            pl.BlockSpec((tm, HEAD_DIM), tab),
            pl.BlockSpec((tm, HEAD_DIM), tab),
            pl.BlockSpec((FOURIER_GROUP_CH, 2 * FOURIER_GROUP_CH), fixed),
        ],
        out_specs=[
            pl.BlockSpec((tm, ATTN_WIDTH), row),
            pl.BlockSpec((tm, KV_WIDTH), row),
            pl.BlockSpec((tm, KV_WIDTH), row),
            pl.BlockSpec((tm, FOURIER_WIDTH), row),
            pl.BlockSpec((tm, FOURIER_WIDTH), row),
        ],
        out_shape=[
            jax.ShapeDtypeStruct((R, ATTN_WIDTH), BF16),
            jax.ShapeDtypeStruct((R, KV_WIDTH), BF16),
            jax.ShapeDtypeStruct((R, KV_WIDTH), BF16),
            jax.ShapeDtypeStruct((R, FOURIER_WIDTH), BF16),
            jax.ShapeDtypeStruct((R, FOURIER_WIDTH), BF16),
        ],
        compiler_params=_cparams("parallel"),
        name="inproj",
    )(hs, gain, w_qkvf, qn, kn, cos_t, sin_t, cs)


def _attn_kernel(q_ref, k_ref, v_ref, o_ref, s_ref, m_ref, accl_ref, *, tk, n_full, tail_valid):
    has_tail = tail_valid > 0
    n_stage = n_full + (1 if has_tail else 0)
    off = 1 if has_tail else 0
    m_ref[...] = jnp.full(m_ref.shape, MASK_VALUE, F32)
    accl_ref[...] = jnp.zeros(accl_ref.shape, F32)

    def tile_start(t):
        if isinstance(t, int):
            return n_full * tk if (has_tail and t == 0) else (t - off) * tk
        return pl.multiple_of((t - off) * tk, tk)

    def scores(t, slot, masked):
        kj = k_ref[0, pl.ds(tile_start(t), tk), :]
        for g in range(GQA_GROUP):
            c = g * HEAD_DIM
            s = lax.dot_general(q_ref[0, :, c:c + HEAD_DIM], kj, (((1,), (1,)), ((), ())),
                                preferred_element_type=F32)
            if masked:
                col = lax.broadcasted_iota(jnp.int32, s.shape, 1)
                s = jnp.where(col < tail_valid, s, MASK_VALUE)
            s_ref[slot, g] = s

    def softmax_pv(t, slot):
        vj = v_ref[0, pl.ds(tile_start(t), tk), :]
        v1 = jnp.concatenate([vj, jnp.ones_like(vj)], axis=1)
        for g in range(GQA_GROUP):
            s = s_ref[slot, g]
            m_prev = m_ref[g]
            m_new = jnp.maximum(m_prev, jnp.max(s, axis=-1, keepdims=True))
            alpha = jnp.exp2(m_prev - m_new)
            p = jnp.exp2(s - pltpu.repeat(m_new, tk // HEAD_DIM, axis=1)).astype(BF16)
            pv = jnp.dot(p, v1, preferred_element_type=F32)
            accl_ref[g] = pltpu.repeat(alpha, 2, axis=1) * accl_ref[g] + pv
            m_ref[g] = m_new

    def stage(t, slot):
        if not isinstance(t, int) or t + 1 < n_stage:
            scores(t + 1, 1 - slot, False)
        softmax_pv(t, slot)

    scores(0, 0, has_tail)
    stage(0, 0)
    n_pairs = max(0, (n_stage - 2) // 2)
    if n_pairs > 0:
        def body(p, carry):
            stage(1 + 2 * p, 1)
            stage(2 + 2 * p, 0)
            return carry
        lax.fori_loop(0, n_pairs, body, 0)
    for t in range(1 + 2 * n_pairs, n_stage):
        stage(t, t % 2)
    for g in range(GQA_GROUP):
        c = g * HEAD_DIM
        accl = accl_ref[g]
        o_ref[0, :, c:c + HEAD_DIM] = (accl[:, :HEAD_DIM] / accl[:, HEAD_DIM:]).astype(BF16)


def _attention(q, k, v, *, L, tq, tk):
    B, LP, _ = q.shape
    gw = GQA_GROUP * HEAD_DIM
    n_full = L // tk
    tail_valid = L - n_full * tk
    kern = functools.partial(_attn_kernel, tk=tk, n_full=n_full, tail_valid=tail_valid)
    return pl.pallas_call(
        kern,
        grid=(B, N_KV_HEADS, LP // tq),
        in_specs=[
            pl.BlockSpec((1, tq, gw), lambda b, h, i: (b, i, h)),
            pl.BlockSpec((1, LP, HEAD_DIM), lambda b, h, i: (b, 0, h)),
            pl.BlockSpec((1, LP, HEAD_DIM), lambda b, h, i: (b, 0, h)),
        ],
        out_specs=pl.BlockSpec((1, tq, gw), lambda b, h, i: (b, i, h)),
        out_shape=jax.ShapeDtypeStruct((B, LP, ATTN_WIDTH), BF16),
        scratch_shapes=[
            pltpu.VMEM((2, GQA_GROUP, tq, tk), F32),
            pltpu.VMEM((GQA_GROUP, tq, HEAD_DIM), F32),
            pltpu.VMEM((GQA_GROUP, tq, 2 * HEAD_DIM), F32),
        ],
        compiler_params=_cparams("parallel", "parallel", "parallel"),
        name="attention",
    )(q, k, v)


def _dft_a_kernel(zr_ref, zi_ref, fr_ref, fi_ref, tc_ref, ts_ref, ar_ref, ai_ref, *, t2, n1p):
    a = (jnp.dot(fr_ref[...], zr_ref[0], preferred_element_type=F32)
         + jnp.dot(fi_ref[...], zi_ref[0], preferred_element_type=F32))
    for n in range(t2):
        tc = tc_ref[n]
        ts = ts_ref[n]
        for gg in range(N_FOURIER_GROUPS):
            c = n * FOURIER_WIDTH + gg * FOURIER_GROUP_CH
            o = gg * FOURIER_GROUP_CH
            re = a[:n1p, c:c + FOURIER_GROUP_CH]
            im = a[n1p:, c:c + FOURIER_GROUP_CH]
            ar_ref[0, n, :, o:o + FOURIER_GROUP_CH] = (re * tc + im * ts).astype(BF16)
            ai_ref[0, n, :, o:o + FOURIER_GROUP_CH] = (im * tc - re * ts).astype(BF16)


def _dft_b_kernel(ar_ref, ai_ref, fc_ref, fs_ref, o_ref):
    x = (jnp.dot(fc_ref[...], ar_ref[0], preferred_element_type=F32)
         + jnp.dot(fs_ref[...], ai_ref[0], preferred_element_type=F32))
    o_ref[0] = x.astype(BF16)


def _fourier(zr, zi, consts, *, B, L, LP, t2, tk1):
    n1 = L // DFT_N2
    n1p = LP // DFT_N2
    f1r, f1i, tc, ts, f2c, f2s = consts
    zr3 = zr.reshape(B, n1p, DFT_N2 * FOURIER_WIDTH)
    zi3 = zi.reshape(B, n1p, DFT_N2 * FOURIER_WIDTH)
    zspec = pl.BlockSpec((1, n1p, t2 * FOURIER_WIDTH), lambda b, j: (b, 0, j))
    fspec = pl.BlockSpec((2 * n1p, n1p), lambda b, j: (0, 0))
    tspec = pl.BlockSpec((t2, n1p, FOURIER_GROUP_CH), lambda b, j: (j, 0, 0))
    aspec = pl.BlockSpec((1, t2, n1p, FOURIER_WIDTH), lambda b, j: (b, j, 0, 0))
    ashape = jax.ShapeDtypeStruct((B, DFT_N2, n1p, FOURIER_WIDTH), BF16)
    ar, ai = pl.pallas_call(
        functools.partial(_dft_a_kernel, t2=t2, n1p=n1p),
        grid=(B, DFT_N2 // t2),
        in_specs=[zspec, zspec, fspec, fspec, tspec, tspec],
        out_specs=[aspec, aspec],
        out_shape=[ashape, ashape],
        compiler_params=_cparams("parallel", "parallel"),
        name="dft_rows",
    )(zr3, zi3, f1r, f1i, tc, ts)
    ar3 = ar.reshape(B, DFT_N2, n1p * FOURIER_WIDTH)
    ai3 = ai.reshape(B, DFT_N2, n1p * FOURIER_WIDTH)
    bspec = pl.BlockSpec((1, DFT_N2, tk1 * FOURIER_WIDTH), lambda b, j: (b, 0, j))
    cspec = pl.BlockSpec((DFT_N2, DFT_N2), lambda b, j: (0, 0))
    out = pl.pallas_call(
        _dft_b_kernel,
        grid=(B, n1 // tk1),
        in_specs=[bspec, bspec, cspec, cspec],
        out_specs=bspec,
        out_shape=jax.ShapeDtypeStruct((B, DFT_N2, n1 * FOURIER_WIDTH), BF16),
        compiler_params=_cparams("parallel", "parallel"),
        name="dft_cols",
    )(ar3, ai3, f2c, f2s)
    fr = out.reshape(B, L, FOURIER_WIDTH)
    return jnp.pad(fr, ((0, 0), (0, LP - L), (0, 0))).reshape(B * LP, FOURIER_WIDTH)


def _dft_constants(L, LP):
    n1 = L // DFT_N2
    n1p = LP // DFT_N2
    k = np.arange(n1)
    ang1 = 2.0 * np.pi * np.outer(k, k) / n1
    c1, s1 = np.cos(ang1), np.sin(ang1)
    f1r = np.zeros((2 * n1p, n1p))
    f1i = np.zeros((2 * n1p, n1p))
    f1r[:n1, :n1] = c1
    f1r[n1p:n1p + n1, :n1] = -s1
    f1i[:n1, :n1] = s1
    f1i[n1p:n1p + n1, :n1] = c1
    n2 = np.arange(DFT_N2)
    angt = 2.0 * np.pi * ((n2[:, None] * k[None, :]) % L) / L
    tc = np.zeros((DFT_N2, n1p, FOURIER_GROUP_CH))
    ts = np.zeros((DFT_N2, n1p, FOURIER_GROUP_CH))
    tc[:, :n1, :] = np.cos(angt)[:, :, None]
    ts[:, :n1, :] = np.sin(angt)[:, :, None]
    ang2 = 2.0 * np.pi * np.outer(n2, n2) / DFT_N2
    f2c = np.cos(ang2) / math.sqrt(L)
    f2s = np.sin(ang2) / math.sqrt(L)
    c = np.arange(FOURIER_GROUP_CH)
    angc = 2.0 * np.pi * np.outer(c, c) / FOURIER_GROUP_CH
    cs = np.concatenate([np.cos(angc), -np.sin(angc)], axis=1) / math.sqrt(FOURIER_GROUP_CH)
    dft = (jnp.asarray(f1r, BF16), jnp.asarray(f1i, BF16), jnp.asarray(tc, F32), jnp.asarray(ts, F32),
           jnp.asarray(f2c, BF16), jnp.asarray(f2s, BF16))
    return jnp.asarray(cs, BF16), dft


def _mix_kernel(x_ref, g_ref, attn_ref, four_ref, wga_ref, wgf_ref, bga_ref, bgf_ref,
                wa_ref, wf_ref, wo_ref, o_ref, h_ref, *, tm, L, LP):
    j = pl.program_id(1)

    @pl.when(j == 0)
    def _():
        x = x_ref[...]
        h_ref[...] = (_rms(x) * g_ref[...]).astype(BF16)
        o_ref[...] = x

    h = h_ref[...]
    g_attn = jax.nn.sigmoid(jnp.dot(h, wga_ref[...], preferred_element_type=F32) + bga_ref[...])
    g_four = jax.nn.sigmoid(jnp.dot(h, wgf_ref[...], preferred_element_type=F32) + bgf_ref[...])
    a_br = jnp.dot(attn_ref[...], wa_ref[...], preferred_element_type=F32)
    s_br = jnp.dot(four_ref[...], wf_ref[...], preferred_element_type=F32)
    merged = (g_attn * a_br + g_four * s_br).astype(BF16)
    o_ref[...] += jnp.dot(merged, wo_ref[...], preferred_element_type=F32)

    @pl.when(j == pl.num_programs(1) - 1)
    def _():
        r = pl.program_id(0) * tm + lax.broadcasted_iota(jnp.int32, (tm, 1), 0)
        o_ref[...] = jnp.where((r % LP) < L, o_ref[...], 0.0)


def _mix(hs, gain, attn, four, w_gates, b_gates, w_a, w_f, w_o, *, tm, tn, L, LP):
    R = hs.shape[0]
    nj = D_MODEL // tn
    row = lambda i, j: (i, 0)
    return pl.pallas_call(
        functools.partial(_mix_kernel, tm=tm, L=L, LP=LP),
        grid=(R // tm, nj),
        in_specs=[
            pl.BlockSpec((tm, D_MODEL), row),
            pl.BlockSpec((1, D_MODEL), lambda i, j: (0, 0)),
            pl.BlockSpec((tm, ATTN_WIDTH), row),
            pl.BlockSpec((tm, FOURIER_WIDTH), row),
            pl.BlockSpec((D_MODEL, tn), lambda i, j: (0, j)),
            pl.BlockSpec((D_MODEL, tn), lambda i, j: (0, j + nj)),
            pl.BlockSpec((1, tn), lambda i, j: (0, j)),
            pl.BlockSpec((1, tn), lambda i, j: (0, j + nj)),
            pl.BlockSpec((ATTN_WIDTH, tn), lambda i, j: (0, j)),
            pl.BlockSpec((FOURIER_WIDTH, tn), lambda i, j: (0, j)),
            pl.BlockSpec((tn, D_MODEL), lambda i, j: (j, 0)),
        ],
        out_specs=pl.BlockSpec((tm, D_MODEL), row),
        out_shape=jax.ShapeDtypeStruct((R, D_MODEL), F32),
        scratch_shapes=[pltpu.VMEM((tm, D_MODEL), BF16)],
        compiler_params=_cparams("parallel", "arbitrary"),
        name="mix_out",
    )(hs, gain, attn, four, w_gates, w_gates, b_gates, b_gates, w_a, w_f, w_o)


def _ffn_kernel(x_ref, xp_ref, xn_ref, g_ref, wg_ref, wv_ref, wc_ref, bc_ref, wd_ref,
                o_ref, h_ref, *, tm):
    i = pl.program_id(0)
    j = pl.program_id(1)
    H = CONV_HALO

    @pl.when(j == 0)
    def _():
        x = x_ref[...]
        g = g_ref[...]
        h_ref[H:H + tm, :] = (_rms(x) * g).astype(BF16)
        hp = jnp.where(i > 0, _rms(xp_ref[...]) * g, 0.0)
        hn = jnp.where(i < pl.num_programs(0) - 1, _rms(xn_ref[...]) * g, 0.0)
        h_ref[0:H, :] = hp.astype(BF16)
        h_ref[H + tm:, :] = hn.astype(BF16)
        o_ref[...] = x

    h = h_ref[...]
    up_g = jnp.dot(h, wg_ref[...], preferred_element_type=F32)
    n = tm + 2 * H
    wc = wc_ref[...]
    prev = pltpu.roll(up_g, 1, 0)[H:H + tm]
    nxt = pltpu.roll(up_g, n - 1, 0)[H:H + tm]
    u = prev * wc[0:1] + up_g[H:H + tm] * wc[1:2] + nxt * wc[2:3] + bc_ref[...]
    val = jnp.dot(h[H:H + tm], wv_ref[...], preferred_element_type=F32)
    act = (u * jax.nn.sigmoid(u) * val).astype(BF16)
    o_ref[...] += jnp.dot(act, wd_ref[...], preferred_element_type=F32)


def _ffn(hs, gain, w_up, w_conv, b_conv, w_down, *, tm, tf):
    R = hs.shape[0]
    nj = D_FF // tf
    hb = tm // CONV_HALO
    last_hb = R // CONV_HALO - 1
    row = lambda i, j: (i, 0)
    return pl.pallas_call(
        functools.partial(_ffn_kernel, tm=tm),
        grid=(R // tm, nj),
        in_specs=[
            pl.BlockSpec((tm, D_MODEL), row),
            pl.BlockSpec((CONV_HALO, D_MODEL), lambda i, j: (jnp.maximum(i * hb - 1, 0), 0)),
            pl.BlockSpec((CONV_HALO, D_MODEL), lambda i, j: (jnp.minimum((i + 1) * hb, last_hb), 0)),
            pl.BlockSpec((1, D_MODEL), lambda i, j: (0, 0)),
            pl.BlockSpec((D_MODEL, tf), lambda i, j: (0, j)),
            pl.BlockSpec((D_MODEL, tf), lambda i, j: (0, j + nj)),
            pl.BlockSpec((3, tf), lambda i, j: (0, j)),
            pl.BlockSpec((1, tf), lambda i, j: (0, j)),
            pl.BlockSpec((tf, D_MODEL), lambda i, j: (j, 0)),
        ],
        out_specs=pl.BlockSpec((tm, D_MODEL), row),
        out_shape=jax.ShapeDtypeStruct((R, D_MODEL), F32),
        scratch_shapes=[pltpu.VMEM((tm + 2 * CONV_HALO, D_MODEL), BF16)],
        compiler_params=_cparams("parallel", "arbitrary"),
        name="ffn",
    )(hs, hs, hs, gain, w_up, w_up, w_conv, b_conv, w_down)


def _rope_tables(n_tok, LP):
    rows = n_tok // GRID_W
    pad = LP - N_META - n_tok
    z_meta = jnp.zeros((N_META,), F32)
    z_pad = jnp.zeros((pad,), F32)
    pos_r = jnp.concatenate([z_meta, jnp.repeat(jnp.arange(rows, dtype=F32), GRID_W), z_pad])
    pos_c = jnp.concatenate([z_meta, jnp.tile(jnp.arange(GRID_W, dtype=F32), rows), z_pad])
    inv_freq = 1.0 / (ROPE_THETA ** (jnp.arange(ROPE_PAIRS, dtype=F32) / ROPE_PAIRS))
    ang_r = pos_r[:, None] * inv_freq[None, :]
    ang_c = pos_c[:, None] * inv_freq[None, :]
    cos_t = jnp.concatenate([jnp.cos(ang_r), jnp.cos(ang_r), jnp.cos(ang_c), jnp.cos(ang_c)], axis=-1)
    sin_t = jnp.concatenate([-jnp.sin(ang_r), jnp.sin(ang_r), -jnp.sin(ang_c), jnp.sin(ang_c)], axis=-1)
    return cos_t, sin_t


def _tiles(LP):
    big = 1280 if LP % 1280 == 0 else ROW_ALIGN
    return dict(tm_in=ROW_ALIGN, tq=256 if LP % 256 == 0 else 128, tk=big, tm_mix=ROW_ALIGN, tn_mix=256,
                tm_ffn=ROW_ALIGN, tf=512)


def kernel(x, meta_tokens, norm_mix, norm_ffn, w_in, b_gate, q_norm, k_norm, w_attn_br, w_four, w_out,
           w_up, w_conv, b_conv, w_down):
    B, n_tok, D = x.shape
    depth = w_in.shape[0]
    L = n_tok + N_META
    assert D == D_MODEL and n_tok % GRID_W == 0 and L % DFT_N2 == 0
    LP = -(-L // ROW_ALIGN) * ROW_ALIGN
    R = B * LP
    t = _tiles(LP)
    n1 = L // DFT_N2
    tk1 = 5 if n1 % 5 == 0 else 1

    cos_t, sin_t = _rope_tables(n_tok, LP)
    cs, dft = _dft_constants(L, LP)

    meta = jnp.broadcast_to(meta_tokens.astype(x.dtype)[None], (B, N_META, D))
    hs = jnp.concatenate([meta, x, jnp.zeros((B, LP - L, D), x.dtype)], axis=1).reshape(R, D)

    for i in range(depth):
        w_qkvf = w_in[i, :, :QKVF_WIDTH].astype(BF16)
        w_gates = w_in[i, :, QKVF_WIDTH:].astype(BF16)
        q, k, v, zr, zi = _inproj(hs, norm_mix[i][None], w_qkvf, q_norm[i][None], k_norm[i][None],
                                  cos_t, sin_t, cs, tm=t["tm_in"], tiles_per_batch=LP // t["tm_in"])
        attn = _attention(q.reshape(B, LP, ATTN_WIDTH), k.reshape(B, LP, KV_WIDTH),
                          v.reshape(B, LP, KV_WIDTH), L=L, tq=t["tq"], tk=t["tk"]).reshape(R, ATTN_WIDTH)
        four = _fourier(zr, zi, dft, B=B, L=L, LP=LP, t2=4, tk1=tk1)
        hs = _mix(hs, norm_mix[i][None], attn, four, w_gates, b_gate[i][None],
                  w_attn_br[i].astype(BF16), w_four[i].astype(BF16), w_out[i].astype(BF16),
                  tm=t["tm_mix"], tn=t["tn_mix"], L=L, LP=LP)
        hs = _ffn(hs, norm_ffn[i][None], w_up[i].astype(BF16), w_conv[i], b_conv[i][None],
                  w_down[i].astype(BF16), tm=t["tm_ffn"], tf=t["tf"])

    return hs.reshape(B, LP, D)[:, N_META:L]
```

```python
import functools
import math

import jax
import jax.numpy as jnp
import numpy as np
from jax import lax
from jax.experimental import pallas as pl
from jax.experimental.pallas import tpu as pltpu

F32 = jnp.float32
BF16 = jnp.bfloat16

D_MODEL = 2048
N_HEADS = 8
N_KV_HEADS = 2
HEAD_DIM = 128
GQA_GROUP = N_HEADS // N_KV_HEADS
ATTN_WIDTH = N_HEADS * HEAD_DIM
KV_WIDTH = N_KV_HEADS * HEAD_DIM
N_FOURIER_GROUPS = 8
FOURIER_GROUP_CH = 128
FOURIER_WIDTH = N_FOURIER_GROUPS * FOURIER_GROUP_CH
QKVF_WIDTH = ATTN_WIDTH + 2 * KV_WIDTH + FOURIER_WIDTH
OFF_K = ATTN_WIDTH
OFF_V = OFF_K + KV_WIDTH
OFF_F = OFF_V + KV_WIDTH
D_FF = 5632
N_META = 16
GRID_W = 64
NORM_EPS = 1e-6
ROPE_THETA = 10000.0
ROPE_PAIRS = HEAD_DIM // 4

DFT_N2 = 80
ROW_ALIGN = 640
CONV_HALO = 16
MASK_VALUE = -1e30
VMEM_LIMIT = 56 * 1024 * 1024


def _cparams(*sem):
    return pltpu.CompilerParams(dimension_semantics=sem, vmem_limit_bytes=VMEM_LIMIT)


def _rms(x):
    return x * lax.rsqrt(jnp.mean(x * x, axis=-1, keepdims=True) + NORM_EPS)


def _inproj_kernel(x_ref, g_ref, w_ref, qn_ref, kn_ref, cos_ref, sin_ref,
                   q_ref, k_ref, v_ref, f_ref):
    h = (_rms(x_ref[...]) * g_ref[...]).astype(BF16)
    proj = jnp.dot(h, w_ref[...], preferred_element_type=F32)
    cos = cos_ref[...]
    sin = sin_ref[...]
    lane = lax.broadcasted_iota(jnp.int32, cos.shape, 1)
    first_half = (lane % (HEAD_DIM // 2)) < ROPE_PAIRS

    def norm_rope(t, gain):
        t = _rms(t) * gain
        partner = jnp.where(first_half,
                            pltpu.roll(t, HEAD_DIM - ROPE_PAIRS, 1),
                            pltpu.roll(t, ROPE_PAIRS, 1))
        return t * cos + partner * sin

    scale = math.log2(math.e) / math.sqrt(HEAD_DIM)
    for hh in range(N_HEADS):
        c = hh * HEAD_DIM
        q_ref[:, c:c + HEAD_DIM] = (norm_rope(proj[:, c:c + HEAD_DIM], qn_ref[...]) * scale).astype(BF16)
    for hh in range(N_KV_HEADS):
        c = hh * HEAD_DIM
        k_ref[:, c:c + HEAD_DIM] = norm_rope(proj[:, OFF_K + c:OFF_K + c + HEAD_DIM], kn_ref[...]).astype(BF16)
    v_ref[...] = proj[:, OFF_V:OFF_F].astype(BF16)
    f_ref[...] = proj[:, OFF_F:].astype(BF16)


def _inproj(hs, gain, w_qkvf, qn, kn, cos_t, sin_t, *, tm, tiles_per_batch):
    R = hs.shape[0]
    row = lambda i: (i, 0)
    fixed = lambda i: (0, 0)
    tab = lambda i: (i % tiles_per_batch, 0)
    return pl.pallas_call(
        _inproj_kernel,
        grid=(R // tm,),
        in_specs=[
            pl.BlockSpec((tm, D_MODEL), row),
            pl.BlockSpec((1, D_MODEL), fixed),
            pl.BlockSpec((D_MODEL, QKVF_WIDTH), fixed),
            pl.BlockSpec((1, HEAD_DIM), fixed),
            pl.BlockSpec((1, HEAD_DIM), fixed),
            pl.BlockSpec((tm, HEAD_DIM), tab),
            pl.BlockSpec((tm, HEAD_DIM), tab),
        ],
        out_specs=[
            pl.BlockSpec((tm, ATTN_WIDTH), row),
            pl.BlockSpec((tm, KV_WIDTH), row),
            pl.BlockSpec((tm, KV_WIDTH), row),
            pl.BlockSpec((tm, FOURIER_WIDTH), row),
        ],
        out_shape=[
            jax.ShapeDtypeStruct((R, ATTN_WIDTH), BF16),
            jax.ShapeDtypeStruct((R, KV_WIDTH), BF16),
            jax.ShapeDtypeStruct((R, KV_WIDTH), BF16),
            jax.ShapeDtypeStruct((R, FOURIER_WIDTH), BF16),
        ],
        compiler_params=_cparams("parallel"),
        name="inproj",
    )(hs, gain, w_qkvf, qn, kn, cos_t, sin_t)


def _attn_kernel(q_ref, k_ref, v_ref, o_ref, s_ref, m_ref, accl_ref, *, tk, n_full, tail_valid):
    has_tail = tail_valid > 0
    n_stage = n_full + (1 if has_tail else 0)
    off = 1 if has_tail else 0
    m_ref[...] = jnp.full(m_ref.shape, MASK_VALUE, F32)
    accl_ref[...] = jnp.zeros(accl_ref.shape, F32)

    def tile_start(t):
        if isinstance(t, int):
            return n_full * tk if (has_tail and t == 0) else (t - off) * tk
        return pl.multiple_of((t - off) * tk, tk)

    def scores(t, slot, masked):
        kj = k_ref[0, pl.ds(tile_start(t), tk), :]
        for g in range(GQA_GROUP):
            c = g * HEAD_DIM
            s = lax.dot_general(q_ref[0, :, c:c + HEAD_DIM], kj, (((1,), (1,)), ((), ())),
                                preferred_element_type=F32)
            if masked:
                col = lax.broadcasted_iota(jnp.int32, s.shape, 1)
                s = jnp.where(col < tail_valid, s, MASK_VALUE)
            s_ref[slot, g] = s

    def softmax_pv(t, slot):
        vj = v_ref[0, pl.ds(tile_start(t), tk), :]
        v1 = jnp.concatenate([vj, jnp.ones_like(vj)], axis=1)
        for g in range(GQA_GROUP):
            s = s_ref[slot, g]
            m_prev = m_ref[g]
            m_new = jnp.maximum(m_prev, jnp.max(s, axis=-1, keepdims=True))
            alpha = jnp.exp2(m_prev - m_new)
            p = jnp.exp2(s - jnp.concatenate([m_new] * (tk // HEAD_DIM), axis=1)).astype(BF16)
            pv = jnp.dot(p, v1, preferred_element_type=F32)
            accl_ref[g] = jnp.concatenate([alpha, alpha], axis=1) * accl_ref[g] + pv
            m_ref[g] = m_new

    def stage(t, slot):
        if not isinstance(t, int) or t + 1 < n_stage:
            scores(t + 1, 1 - slot, False)
        softmax_pv(t, slot)

    scores(0, 0, has_tail)
    stage(0, 0)
    n_pairs = max(0, (n_stage - 2) // 2)
    if n_pairs > 0:
        def body(p, carry):
            stage(1 + 2 * p, 1)
            stage(2 + 2 * p, 0)
            return carry
        lax.fori_loop(0, n_pairs, body, 0)
    for t in range(1 + 2 * n_pairs, n_stage):
        stage(t, t % 2)
    for g in range(GQA_GROUP):
        c = g * HEAD_DIM
        accl = accl_ref[g]
        o_ref[0, :, c:c + HEAD_DIM] = (accl[:, :HEAD_DIM] / accl[:, HEAD_DIM:]).astype(BF16)


def _attention(q, k, v, *, L, tq, tk):
    B, LP, _ = q.shape
    gw = GQA_GROUP * HEAD_DIM
    n_full = L // tk
    tail_valid = L - n_full * tk
    kern = functools.partial(_attn_kernel, tk=tk, n_full=n_full, tail_valid=tail_valid)
    return pl.pallas_call(
        kern,
        grid=(B, N_KV_HEADS, LP // tq),
        in_specs=[
            pl.BlockSpec((1, tq, gw), lambda b, h, i: (b, i, h)),
            pl.BlockSpec((1, LP, HEAD_DIM), lambda b, h, i: (b, 0, h)),
            pl.BlockSpec((1, LP, HEAD_DIM), lambda b, h, i: (b, 0, h)),
        ],
        out_specs=pl.BlockSpec((1, tq, gw), lambda b, h, i: (b, i, h)),
        out_shape=jax.ShapeDtypeStruct((B, LP, ATTN_WIDTH), BF16),
        scratch_shapes=[
            pltpu.VMEM((2, GQA_GROUP, tq, tk), F32),
            pltpu.VMEM((GQA_GROUP, tq, HEAD_DIM), F32),
            pltpu.VMEM((GQA_GROUP, tq, 2 * HEAD_DIM), F32),
        ],
        compiler_params=_cparams("parallel", "parallel", "parallel"),
        name="attention",
    )(q, k, v)


def _fourier_kernel(f_ref, cs_ref, f1r_ref, f1i_ref, g_ref, o_ref, wr_ref, wi_ref, os_ref,
                    *, L, LP, n1, n1p):
    chunk = ROW_ALIGN

    def channel_dft(c, carry):
        r0 = pl.multiple_of(c * chunk, chunk)
        z = jnp.dot(f_ref[0, pl.ds(r0, chunk), :], cs_ref[...], preferred_element_type=F32)
        wr_ref[pl.ds(r0, chunk), :] = z[:, :FOURIER_GROUP_CH]
        wi_ref[pl.ds(r0, chunk), :] = z[:, FOURIER_GROUP_CH:]
        return carry

    lax.fori_loop(0, LP // chunk, channel_dft, 0, unroll=2)

    def row_dft(n2, carry):
        rows = pl.ds(n2, n1p, stride=DFT_N2)
        zr = wr_ref[rows, :].astype(BF16)
        zi = wi_ref[rows, :].astype(BF16)
        a = (jnp.dot(f1r_ref[...], zr, preferred_element_type=F32)
             + jnp.dot(f1i_ref[...], zi, preferred_element_type=F32))
        wr_ref[rows, :] = a[:n1p]
        wi_ref[rows, :] = a[n1p:]
        return carry

    lax.fori_loop(0, DFT_N2, row_dft, 0, unroll=4)

    os_ref[L:, :] = jnp.zeros((LP - L, FOURIER_GROUP_CH), F32)

    def col_dft(k1, carry):
        r0 = pl.multiple_of(k1 * DFT_N2, DFT_N2)
        a = jnp.concatenate([wr_ref[pl.ds(r0, DFT_N2), :].astype(BF16),
                             wi_ref[pl.ds(r0, DFT_N2), :].astype(BF16)], axis=0)
        os_ref[pl.ds(k1, DFT_N2, stride=n1), :] = jnp.dot(g_ref[k1], a, preferred_element_type=F32)
        return carry

    lax.fori_loop(0, n1, col_dft, 0, unroll=5 if n1 % 5 == 0 else 1)

    def emit(c, carry):
        r0 = pl.multiple_of(c * chunk, chunk)
        o_ref[0, pl.ds(r0, chunk), :] = os_ref[pl.ds(r0, chunk), :].astype(BF16)
        return carry

    lax.fori_loop(0, LP // chunk, emit, 0)


def _fourier(f, consts, *, L):
    B, LP, _ = f.shape
    n1 = L // DFT_N2
    n1p = LP // DFT_N2
    cs, f1r, f1i, g = consts
    slab = pl.BlockSpec((1, LP, FOURIER_GROUP_CH), lambda b, c: (b, 0, c))
    fixed2 = lambda b, c: (0, 0)
    return pl.pallas_call(
        functools.partial(_fourier_kernel, L=L, LP=LP, n1=n1, n1p=n1p),
        grid=(B, N_FOURIER_GROUPS),
        in_specs=[
            slab,
            pl.BlockSpec((FOURIER_GROUP_CH, 2 * FOURIER_GROUP_CH), fixed2),
            pl.BlockSpec((2 * n1p, n1p), fixed2),
            pl.BlockSpec((2 * n1p, n1p), fixed2),
            pl.BlockSpec((n1, DFT_N2, 2 * DFT_N2), lambda b, c: (0, 0, 0), pipeline_mode=pl.Buffered(1)),
        ],
        out_specs=slab,
        out_shape=jax.ShapeDtypeStruct((B, LP, FOURIER_WIDTH), BF16),
        scratch_shapes=[pltpu.VMEM((LP, FOURIER_GROUP_CH), F32)] * 3,
        compiler_params=_cparams("parallel", "parallel"),
        name="fourier",
    )(f, cs, f1r, f1i, g)


def _dft_constants(L, LP):
    n1 = L // DFT_N2
    n1p = LP // DFT_N2
    k1 = np.arange(n1)
    ang1 = 2.0 * np.pi * np.outer(k1, k1) / n1
    c1, s1 = np.cos(ang1), np.sin(ang1)
    f1r = np.zeros((2 * n1p, n1p))
    f1i = np.zeros((2 * n1p, n1p))
    f1r[:n1, :n1] = c1
    f1r[n1p:n1p + n1, :n1] = -s1
    f1i[:n1, :n1] = s1
    f1i[n1p:n1p + n1, :n1] = c1
    n2 = np.arange(DFT_N2)
    k = k1[:, None, None] + n1 * n2[None, :, None]
    ang = 2.0 * np.pi * ((k * n2[None, None, :]) % L) / L
    g = np.concatenate([np.cos(ang), np.sin(ang)], axis=2) / math.sqrt(L)
    c = np.arange(FOURIER_GROUP_CH)
    angc = 2.0 * np.pi * np.outer(c, c) / FOURIER_GROUP_CH
    cs = np.concatenate([np.cos(angc), -np.sin(angc)], axis=1) / math.sqrt(FOURIER_GROUP_CH)
    return (jnp.asarray(cs, BF16), jnp.asarray(f1r, BF16), jnp.asarray(f1i, BF16), jnp.asarray(g, BF16))


def _mix_kernel(x_ref, g_ref, attn_ref, four_ref, wga_ref, wgf_ref, bga_ref, bgf_ref,
                wa_ref, wf_ref, wo_ref, o_ref, h_ref, *, tm, L, LP):
    j = pl.program_id(1)

    @pl.when(j == 0)
    def _():
        x = x_ref[...]
        h_ref[...] = (_rms(x) * g_ref[...]).astype(BF16)
        o_ref[...] = x

    h = h_ref[...]
    g_attn = jax.nn.sigmoid(jnp.dot(h, wga_ref[...], preferred_element_type=F32) + bga_ref[...])
    g_four = jax.nn.sigmoid(jnp.dot(h, wgf_ref[...], preferred_element_type=F32) + bgf_ref[...])
    a_br = jnp.dot(attn_ref[...], wa_ref[...], preferred_element_type=F32)
    s_br = jnp.dot(four_ref[...], wf_ref[...], preferred_element_type=F32)
    merged = (g_attn * a_br + g_four * s_br).astype(BF16)
    o_ref[...] += jnp.dot(merged, wo_ref[...], preferred_element_type=F32)

    @pl.when(j == pl.num_programs(1) - 1)
    def _():
        r = pl.program_id(0) * tm + lax.broadcasted_iota(jnp.int32, (tm, 1), 0)
        o_ref[...] = jnp.where((r % LP) < L, o_ref[...], 0.0)


def _mix(hs, gain, attn, four, w_gates, b_gates, w_a, w_f, w_o, *, tm, tn, L, LP):
    R = hs.shape[0]
    nj = D_MODEL // tn
    row = lambda i, j: (i, 0)
    return pl.pallas_call(
        functools.partial(_mix_kernel, tm=tm, L=L, LP=LP),
        grid=(R // tm, nj),
        in_specs=[
            pl.BlockSpec((tm, D_MODEL), row),
            pl.BlockSpec((1, D_MODEL), lambda i, j: (0, 0)),
            pl.BlockSpec((tm, ATTN_WIDTH), row),
            pl.BlockSpec((tm, FOURIER_WIDTH), row),
            pl.BlockSpec((D_MODEL, tn), lambda i, j: (0, j)),
            pl.BlockSpec((D_MODEL, tn), lambda i, j: (0, j + nj)),
            pl.BlockSpec((1, tn), lambda i, j: (0, j)),
            pl.BlockSpec((1, tn), lambda i, j: (0, j + nj)),
            pl.BlockSpec((ATTN_WIDTH, tn), lambda i, j: (0, j)),
            pl.BlockSpec((FOURIER_WIDTH, tn), lambda i, j: (0, j)),
            pl.BlockSpec((tn, D_MODEL), lambda i, j: (j, 0)),
        ],
        out_specs=pl.BlockSpec((tm, D_MODEL), row),
        out_shape=jax.ShapeDtypeStruct((R, D_MODEL), F32),
        scratch_shapes=[pltpu.VMEM((tm, D_MODEL), BF16)],
        compiler_params=_cparams("parallel", "arbitrary"),
        name="mix_out",
    )(hs, gain, attn, four, w_gates, w_gates, b_gates, b_gates, w_a, w_f, w_o)


def _ffn_kernel(x_ref, xp_ref, xn_ref, g_ref, wg_ref, wv_ref, wc_ref, bc_ref, wd_ref,
                o_ref, h_ref, *, tm):
    i = pl.program_id(0)
    j = pl.program_id(1)
    H = CONV_HALO

    @pl.when(j == 0)
    def _():
        x = x_ref[...]
        g = g_ref[...]
        h_ref[H:H + tm, :] = (_rms(x) * g).astype(BF16)
        hp = jnp.where(i > 0, _rms(xp_ref[...]) * g, 0.0)
        hn = jnp.where(i < pl.num_programs(0) - 1, _rms(xn_ref[...]) * g, 0.0)
        h_ref[0:H, :] = hp.astype(BF16)
        h_ref[H + tm:, :] = hn.astype(BF16)
        o_ref[...] = x

    h = h_ref[...]
    up_g = jnp.dot(h, wg_ref[...], preferred_element_type=F32)
    n = tm + 2 * H
    wc = wc_ref[...]
    prev = pltpu.roll(up_g, 1, 0)[H:H + tm]
    nxt = pltpu.roll(up_g, n - 1, 0)[H:H + tm]
    u = prev * wc[0:1] + up_g[H:H + tm] * wc[1:2] + nxt * wc[2:3] + bc_ref[...]
    val = jnp.dot(h[H:H + tm], wv_ref[...], preferred_element_type=F32)
    act = (u * jax.nn.sigmoid(u) * val).astype(BF16)
    o_ref[...] += jnp.dot(act, wd_ref[...], preferred_element_type=F32)


def _ffn(hs, gain, w_up, w_conv, b_conv, w_down, *, tm, tf):
    R = hs.shape[0]
    nj = D_FF // tf
    hb = tm // CONV_HALO
    last_hb = R // CONV_HALO - 1
    row = lambda i, j: (i, 0)
    return pl.pallas_call(
        functools.partial(_ffn_kernel, tm=tm),
        grid=(R // tm, nj),
        in_specs=[
            pl.BlockSpec((tm, D_MODEL), row),
            pl.BlockSpec((CONV_HALO, D_MODEL), lambda i, j: (jnp.maximum(i * hb - 1, 0), 0)),
            pl.BlockSpec((CONV_HALO, D_MODEL), lambda i, j: (jnp.minimum((i + 1) * hb, last_hb), 0)),
            pl.BlockSpec((1, D_MODEL), lambda i, j: (0, 0)),
            pl.BlockSpec((D_MODEL, tf), lambda i, j: (0, j)),
            pl.BlockSpec((D_MODEL, tf), lambda i, j: (0, j + nj)),
            pl.BlockSpec((3, tf), lambda i, j: (0, j)),
            pl.BlockSpec((1, tf), lambda i, j: (0, j)),
            pl.BlockSpec((tf, D_MODEL), lambda i, j: (j, 0)),
        ],
        out_specs=pl.BlockSpec((tm, D_MODEL), row),
        out_shape=jax.ShapeDtypeStruct((R, D_MODEL), F32),
        scratch_shapes=[pltpu.VMEM((tm + 2 * CONV_HALO, D_MODEL), BF16)],
        compiler_params=_cparams("parallel", "arbitrary"),
        name="ffn",
    )(hs, hs, hs, gain, w_up, w_up, w_conv, b_conv, w_down)


def _rope_tables(n_tok, LP):
    rows = n_tok // GRID_W
    pad = LP - N_META - n_tok
    z_meta = jnp.zeros((N_META,), F32)
    z_pad = jnp.zeros((pad,), F32)
    pos_r = jnp.concatenate([z_meta, jnp.repeat(jnp.arange(rows, dtype=F32), GRID_W), z_pad])
    pos_c = jnp.concatenate([z_meta, jnp.tile(jnp.arange(GRID_W, dtype=F32), rows), z_pad])
    inv_freq = 1.0 / (ROPE_THETA ** (jnp.arange(ROPE_PAIRS, dtype=F32) / ROPE_PAIRS))
    ang_r = pos_r[:, None] * inv_freq[None, :]
    ang_c = pos_c[:, None] * inv_freq[None, :]
    cos_t = jnp.concatenate([jnp.cos(ang_r), jnp.cos(ang_r), jnp.cos(ang_c), jnp.cos(ang_c)], axis=-1)
    sin_t = jnp.concatenate([-jnp.sin(ang_r), jnp.sin(ang_r), -jnp.sin(ang_c), jnp.sin(ang_c)], axis=-1)
    return cos_t, sin_t


def _tiles(LP):
    big = 1280 if LP % 1280 == 0 else ROW_ALIGN
    return dict(tm_in=ROW_ALIGN, tq=256 if LP % 256 == 0 else 128, tk=big, tm_mix=ROW_ALIGN, tn_mix=512,
                tm_ffn=ROW_ALIGN, tf=512)


def kernel(x, meta_tokens, norm_mix, norm_ffn, w_in, b_gate, q_norm, k_norm, w_attn_br, w_four, w_out,
           w_up, w_conv, b_conv, w_down):
    B, n_tok, D = x.shape
    depth = w_in.shape[0]
    L = n_tok + N_META
    assert D == D_MODEL and n_tok % GRID_W == 0 and L % DFT_N2 == 0
    LP = -(-L // ROW_ALIGN) * ROW_ALIGN
    R = B * LP
    t = _tiles(LP)

    cos_t, sin_t = _rope_tables(n_tok, LP)
    dft = _dft_constants(L, LP)

    meta = jnp.broadcast_to(meta_tokens.astype(x.dtype)[None], (B, N_META, D))
    hs = jnp.concatenate([meta, x, jnp.zeros((B, LP - L, D), x.dtype)], axis=1).reshape(R, D)

    for i in range(depth):
        w_qkvf = w_in[i, :, :QKVF_WIDTH].astype(BF16)
        w_gates = w_in[i, :, QKVF_WIDTH:].astype(BF16)
        q, k, v, f = _inproj(hs, norm_mix[i][None], w_qkvf, q_norm[i][None], k_norm[i][None],
                             cos_t, sin_t, tm=t["tm_in"], tiles_per_batch=LP // t["tm_in"])
        attn = _attention(q.reshape(B, LP, ATTN_WIDTH), k.reshape(B, LP, KV_WIDTH),
                          v.reshape(B, LP, KV_WIDTH), L=L, tq=t["tq"], tk=t["tk"]).reshape(R, ATTN_WIDTH)
        four = _fourier(f.reshape(B, LP, FOURIER_WIDTH), dft, L=L).reshape(R, FOURIER_WIDTH)
        hs = _mix(hs, norm_mix[i][None], attn, four, w_gates, b_gate[i][None],
                  w_attn_br[i].astype(BF16), w_four[i].astype(BF16), w_out[i].astype(BF16),
                  tm=t["tm_mix"], tn=t["tn_mix"], L=L, LP=LP)
        hs = _ffn(hs, norm_ffn[i][None], w_up[i].astype(BF16), w_conv[i], b_conv[i][None],
                  w_down[i].astype(BF16), tm=t["tm_ffn"], tf=t["tf"])

    return hs.reshape(B, LP, D)[:, N_META:L]
```

```python
import functools
import math

import jax
import jax.numpy as jnp
import numpy as np
from jax import lax
from jax.experimental import pallas as pl
from jax.experimental.pallas import tpu as pltpu

F32 = jnp.float32
BF16 = jnp.bfloat16

D_MODEL = 2048
N_HEADS = 8
N_KV_HEADS = 2
HEAD_DIM = 128
GQA_GROUP = N_HEADS // N_KV_HEADS
ATTN_WIDTH = N_HEADS * HEAD_DIM
KV_WIDTH = N_KV_HEADS * HEAD_DIM
N_FOURIER_GROUPS = 8
FOURIER_GROUP_CH = 128
FOURIER_WIDTH = N_FOURIER_GROUPS * FOURIER_GROUP_CH
QKVF_WIDTH = ATTN_WIDTH + 2 * KV_WIDTH + FOURIER_WIDTH
OFF_K = ATTN_WIDTH
OFF_V = OFF_K + KV_WIDTH
OFF_F = OFF_V + KV_WIDTH
D_FF = 5632
N_META = 16
GRID_W = 64
NORM_EPS = 1e-6
ROPE_THETA = 10000.0
ROPE_PAIRS = HEAD_DIM // 4

DFT_N2 = 80
ROW_ALIGN = 640
CONV_HALO = 16
MASK_VALUE = -1e30
VMEM_LIMIT = 56 * 1024 * 1024


def _cparams(*sem):
    return pltpu.CompilerParams(dimension_semantics=sem, vmem_limit_bytes=VMEM_LIMIT)


def _rms(x):
    return x * lax.rsqrt(jnp.mean(x * x, axis=-1, keepdims=True) + NORM_EPS)


def _inproj_kernel(x_ref, g_ref, w_ref, qn_ref, kn_ref, cos_ref, sin_ref,
                   q_ref, k_ref, vt_ref, f_ref):
    h = (_rms(x_ref[...]) * g_ref[...]).astype(BF16)
    proj = jnp.dot(h, w_ref[...], preferred_element_type=F32)
    cos = cos_ref[...]
    sin = sin_ref[...]
    lane = lax.broadcasted_iota(jnp.int32, cos.shape, 1)
    first_half = (lane % (HEAD_DIM // 2)) < ROPE_PAIRS

    def norm_rope(t, gain):
        t = _rms(t) * gain
        partner = jnp.where(first_half,
                            pltpu.roll(t, HEAD_DIM - ROPE_PAIRS, 1),
                            pltpu.roll(t, ROPE_PAIRS, 1))
        return t * cos + partner * sin

    scale = math.log2(math.e) / math.sqrt(HEAD_DIM)
    for hh in range(N_HEADS):
        c = hh * HEAD_DIM
        q_ref[:, c:c + HEAD_DIM] = (norm_rope(proj[:, c:c + HEAD_DIM], qn_ref[...]) * scale).astype(BF16)
    for hh in range(N_KV_HEADS):
        c = hh * HEAD_DIM
        k_ref[:, c:c + HEAD_DIM] = norm_rope(proj[:, OFF_K + c:OFF_K + c + HEAD_DIM], kn_ref[...]).astype(BF16)
    vt_ref[...] = proj[:, OFF_V:OFF_F].T.astype(BF16)
    f_ref[...] = proj[:, OFF_F:].astype(BF16)


def _inproj(hs, gain, w_qkvf, qn, kn, cos_t, sin_t, *, tm, tiles_per_batch):
    R = hs.shape[0]
    row = lambda i: (i, 0)
    fixed = lambda i: (0, 0)
    tab = lambda i: (i % tiles_per_batch, 0)
    return pl.pallas_call(
        _inproj_kernel,
        grid=(R // tm,),
        in_specs=[
            pl.BlockSpec((tm, D_MODEL), row),
            pl.BlockSpec((1, D_MODEL), fixed),
            pl.BlockSpec((D_MODEL, QKVF_WIDTH), fixed),
            pl.BlockSpec((1, HEAD_DIM), fixed),
            pl.BlockSpec((1, HEAD_DIM), fixed),
            pl.BlockSpec((tm, HEAD_DIM), tab),
            pl.BlockSpec((tm, HEAD_DIM), tab),
        ],
        out_specs=[
            pl.BlockSpec((tm, ATTN_WIDTH), row),
            pl.BlockSpec((tm, KV_WIDTH), row),
            pl.BlockSpec((KV_WIDTH, tm), lambda i: (0, i)),
            pl.BlockSpec((tm, FOURIER_WIDTH), row),
        ],
        out_shape=[
            jax.ShapeDtypeStruct((R, ATTN_WIDTH), BF16),
            jax.ShapeDtypeStruct((R, KV_WIDTH), BF16),
            jax.ShapeDtypeStruct((KV_WIDTH, R), BF16),
            jax.ShapeDtypeStruct((R, FOURIER_WIDTH), BF16),
        ],
        compiler_params=_cparams("parallel"),
        name="inproj",
    )(hs, gain, w_qkvf, qn, kn, cos_t, sin_t)


def _attn_kernel(q_ref, k_ref, vt_ref, o_ref, s_ref, m_ref, l_ref, acc_ref, *, tk, n_full, tail_valid):
    has_tail = tail_valid > 0
    n_stage = n_full + (1 if has_tail else 0)
    off = 1 if has_tail else 0
    m_ref[...] = jnp.full(m_ref.shape, MASK_VALUE, F32)
    l_ref[...] = jnp.zeros(l_ref.shape, F32)
    acc_ref[...] = jnp.zeros(acc_ref.shape, F32)

    def tile_start(t):
        if isinstance(t, int):
            return n_full * tk if (has_tail and t == 0) else (t - off) * tk
        return pl.multiple_of((t - off) * tk, tk)

    def scores(t, slot, masked):
        kj = k_ref[0, pl.ds(tile_start(t), tk), :]
        for g in range(GQA_GROUP):
            c = g * HEAD_DIM
            st = lax.dot_general(kj, q_ref[0, :, c:c + HEAD_DIM], (((1,), (1,)), ((), ())),
                                 preferred_element_type=F32)
            if masked:
                key = lax.broadcasted_iota(jnp.int32, st.shape, 0)
                st = jnp.where(key < tail_valid, st, MASK_VALUE)
            s_ref[slot, g] = st

    def softmax_pv(t, slot):
        vtj = vt_ref[:, pl.ds(tile_start(t), tk)]
        for g in range(GQA_GROUP):
            st = s_ref[slot, g]
            m_prev = m_ref[g]
            m_new = jnp.maximum(m_prev, jnp.max(st, axis=0, keepdims=True))
            alpha = jnp.exp2(m_prev - m_new)
            p = jnp.exp2(st - m_new)
            l_ref[g] = alpha * l_ref[g] + jnp.sum(p, axis=0, keepdims=True)
            acc_ref[g] = alpha * acc_ref[g] + jnp.dot(vtj, p.astype(BF16), preferred_element_type=F32)
            m_ref[g] = m_new

    def stage(t, slot):
        if not isinstance(t, int) or t + 1 < n_stage:
            scores(t + 1, 1 - slot, False)
        softmax_pv(t, slot)

    scores(0, 0, has_tail)
    stage(0, 0)
    n_pairs = max(0, (n_stage - 2) // 2)
    if n_pairs > 0:
        def body(p, carry):
            stage(1 + 2 * p, 1)
            stage(2 + 2 * p, 0)
            return carry
        lax.fori_loop(0, n_pairs, body, 0)
    for t in range(1 + 2 * n_pairs, n_stage):
        stage(t, t % 2)
    for g in range(GQA_GROUP):
        c = g * HEAD_DIM
        o_ref[0, :, c:c + HEAD_DIM] = (acc_ref[g] / l_ref[g]).T.astype(BF16)


def _attention(q, k, vt, *, L, tq, tk):
    B, LP, _ = q.shape
    gw = GQA_GROUP * HEAD_DIM
    n_full = L // tk
    tail_valid = L - n_full * tk
    kern = functools.partial(_attn_kernel, tk=tk, n_full=n_full, tail_valid=tail_valid)
    return pl.pallas_call(
        kern,
        grid=(B, N_KV_HEADS, LP // tq),
        in_specs=[
            pl.BlockSpec((1, tq, gw), lambda b, h, i: (b, i, h)),
            pl.BlockSpec((1, LP, HEAD_DIM), lambda b, h, i: (b, 0, h)),
            pl.BlockSpec((HEAD_DIM, LP), lambda b, h, i: (h, b)),
        ],
        out_specs=pl.BlockSpec((1, tq, gw), lambda b, h, i: (b, i, h)),
        out_shape=jax.ShapeDtypeStruct((B, LP, ATTN_WIDTH), BF16),
        scratch_shapes=[
            pltpu.VMEM((2, GQA_GROUP, tk, tq), F32),
            pltpu.VMEM((GQA_GROUP, 1, tq), F32),
            pltpu.VMEM((GQA_GROUP, 1, tq), F32),
            pltpu.VMEM((GQA_GROUP, HEAD_DIM, tq), F32),
        ],
        compiler_params=_cparams("parallel", "parallel", "parallel"),
        name="attention",
    )(q, k, vt)


def _fourier_kernel(f_ref, cs_ref, f1r_ref, f1i_ref, g_ref, o_ref, wr_ref, wi_ref, os_ref,
                    *, L, LP, n1, n1p):
    chunk = ROW_ALIGN

    def channel_dft(c, carry):
        r0 = pl.multiple_of(c * chunk, chunk)
        z = jnp.dot(f_ref[0, pl.ds(r0, chunk), :], cs_ref[...], preferred_element_type=F32)
        wr_ref[pl.ds(r0, chunk), :] = z[:, :FOURIER_GROUP_CH]
        wi_ref[pl.ds(r0, chunk), :] = z[:, FOURIER_GROUP_CH:]
        return carry

    lax.fori_loop(0, LP // chunk, channel_dft, 0, unroll=2)

    def row_dft(n2, carry):
        rows = pl.ds(n2, n1p, stride=DFT_N2)
        zr = wr_ref[rows, :].astype(BF16)
        zi = wi_ref[rows, :].astype(BF16)
        a = (jnp.dot(f1r_ref[...], zr, preferred_element_type=F32)
             + jnp.dot(f1i_ref[...], zi, preferred_element_type=F32))
        wr_ref[rows, :] = a[:n1p]
        wi_ref[rows, :] = a[n1p:]
        return carry

    lax.fori_loop(0, DFT_N2, row_dft, 0, unroll=4)

    os_ref[L:, :] = jnp.zeros((LP - L, FOURIER_GROUP_CH), F32)

    def col_dft(k1, carry):
        r0 = pl.multiple_of(k1 * DFT_N2, DFT_N2)
        a = jnp.concatenate([wr_ref[pl.ds(r0, DFT_N2), :].astype(BF16),
                             wi_ref[pl.ds(r0, DFT_N2), :].astype(BF16)], axis=0)
        os_ref[pl.ds(k1, DFT_N2, stride=n1), :] = jnp.dot(g_ref[k1], a, preferred_element_type=F32)
        return carry

    lax.fori_loop(0, n1, col_dft, 0, unroll=5 if n1 % 5 == 0 else 1)

    def emit(c, carry):
        r0 = pl.multiple_of(c * chunk, chunk)
        o_ref[0, pl.ds(r0, chunk), :] = os_ref[pl.ds(r0, chunk), :].astype(BF16)
        return carry

    lax.fori_loop(0, LP // chunk, emit, 0)


def _fourier(f, consts, *, L):
    B, LP, _ = f.shape
    n1 = L // DFT_N2
    n1p = LP // DFT_N2
    cs, f1r, f1i, g = consts
    slab = pl.BlockSpec((1, LP, FOURIER_GROUP_CH), lambda b, c: (b, 0, c))
    fixed2 = lambda b, c: (0, 0)
    return pl.pallas_call(
        functools.partial(_fourier_kernel, L=L, LP=LP, n1=n1, n1p=n1p),
        grid=(B, N_FOURIER_GROUPS),
        in_specs=[
            slab,
            pl.BlockSpec((FOURIER_GROUP_CH, 2 * FOURIER_GROUP_CH), fixed2),
            pl.BlockSpec((2 * n1p, n1p), fixed2),
            pl.BlockSpec((2 * n1p, n1p), fixed2),
            pl.BlockSpec((n1, DFT_N2, 2 * DFT_N2), lambda b, c: (0, 0, 0), pipeline_mode=pl.Buffered(1)),
        ],
        out_specs=slab,
        out_shape=jax.ShapeDtypeStruct((B, LP, FOURIER_WIDTH), BF16),
        scratch_shapes=[pltpu.VMEM((LP, FOURIER_GROUP_CH), F32)] * 3,
        compiler_params=_cparams("parallel", "parallel"),
        name="fourier",
    )(f, cs, f1r, f1i, g)


def _dft_constants(L, LP):
    n1 = L // DFT_N2
    n1p = LP // DFT_N2
    k1 = np.arange(n1)
    ang1 = 2.0 * np.pi * np.outer(k1, k1) / n1
    c1, s1 = np.cos(ang1), np.sin(ang1)
    f1r = np.zeros((2 * n1p, n1p))
    f1i = np.zeros((2 * n1p, n1p))
    f1r[:n1, :n1] = c1
    f1r[n1p:n1p + n1, :n1] = -s1
    f1i[:n1, :n1] = s1
    f1i[n1p:n1p + n1, :n1] = c1
    n2 = np.arange(DFT_N2)
    k = k1[:, None, None] + n1 * n2[None, :, None]
    ang = 2.0 * np.pi * ((k * n2[None, None, :]) % L) / L
    g = np.concatenate([np.cos(ang), np.sin(ang)], axis=2) / math.sqrt(L)
    c = np.arange(FOURIER_GROUP_CH)
    angc = 2.0 * np.pi * np.outer(c, c) / FOURIER_GROUP_CH
    cs = np.concatenate([np.cos(angc), -np.sin(angc)], axis=1) / math.sqrt(FOURIER_GROUP_CH)
    return (jnp.asarray(cs, BF16), jnp.asarray(f1r, BF16), jnp.asarray(f1i, BF16), jnp.asarray(g, BF16))


def _mix_kernel(x_ref, g_ref, attn_ref, four_ref, wga_ref, wgf_ref, bga_ref, bgf_ref,
                wa_ref, wf_ref, wo_ref, o_ref, h_ref, *, tm, L, LP):
    j = pl.program_id(1)

    @pl.when(j == 0)
    def _():
        x = x_ref[...]
        h_ref[...] = (_rms(x) * g_ref[...]).astype(BF16)
        o_ref[...] = x

    h = h_ref[...]
    g_attn = jax.nn.sigmoid(jnp.dot(h, wga_ref[...], preferred_element_type=F32) + bga_ref[...])
    g_four = jax.nn.sigmoid(jnp.dot(h, wgf_ref[...], preferred_element_type=F32) + bgf_ref[...])
    a_br = jnp.dot(attn_ref[...], wa_ref[...], preferred_element_type=F32)
    s_br = jnp.dot(four_ref[...], wf_ref[...], preferred_element_type=F32)
    merged = (g_attn * a_br + g_four * s_br).astype(BF16)
    o_ref[...] += jnp.dot(merged, wo_ref[...], preferred_element_type=F32)

    @pl.when(j == pl.num_programs(1) - 1)
    def _():
        r = pl.program_id(0) * tm + lax.broadcasted_iota(jnp.int32, (tm, 1), 0)
        o_ref[...] = jnp.where((r % LP) < L, o_ref[...], 0.0)


def _mix(hs, gain, attn, four, w_gates, b_gates, w_a, w_f, w_o, *, tm, tn, L, LP):
    R = hs.shape[0]
    nj = D_MODEL // tn
    row = lambda i, j: (i, 0)
    return pl.pallas_call(
        functools.partial(_mix_kernel, tm=tm, L=L, LP=LP),
        grid=(R // tm, nj),
        in_specs=[
            pl.BlockSpec((tm, D_MODEL), row),
            pl.BlockSpec((1, D_MODEL), lambda i, j: (0, 0)),
            pl.BlockSpec((tm, ATTN_WIDTH), row),
            pl.BlockSpec((tm, FOURIER_WIDTH), row),
            pl.BlockSpec((D_MODEL, tn), lambda i, j: (0, j)),
            pl.BlockSpec((D_MODEL, tn), lambda i, j: (0, j + nj)),
            pl.BlockSpec((1, tn), lambda i, j: (0, j)),
            pl.BlockSpec((1, tn), lambda i, j: (0, j + nj)),
            pl.BlockSpec((ATTN_WIDTH, tn), lambda i, j: (0, j)),
            pl.BlockSpec((FOURIER_WIDTH, tn), lambda i, j: (0, j)),
            pl.BlockSpec((tn, D_MODEL), lambda i, j: (j, 0)),
        ],
        out_specs=pl.BlockSpec((tm, D_MODEL), row),
        out_shape=jax.ShapeDtypeStruct((R, D_MODEL), F32),
        scratch_shapes=[pltpu.VMEM((tm, D_MODEL), BF16)],
        compiler_params=_cparams("parallel", "arbitrary"),
        name="mix_out",
    )(hs, gain, attn, four, w_gates, w_gates, b_gates, b_gates, w_a, w_f, w_o)


def _ffn_kernel(x_ref, xp_ref, xn_ref, g_ref, wg_ref, wv_ref, wc_ref, bc_ref, wd_ref,
                o_ref, h_ref, *, tm):
    i = pl.program_id(0)
    j = pl.program_id(1)
    H = CONV_HALO

    @pl.when(j == 0)
    def _():
        x = x_ref[...]
        g = g_ref[...]
        h_ref[H:H + tm, :] = (_rms(x) * g).astype(BF16)
        hp = jnp.where(i > 0, _rms(xp_ref[...]) * g, 0.0)
        hn = jnp.where(i < pl.num_programs(0) - 1, _rms(xn_ref[...]) * g, 0.0)
        h_ref[0:H, :] = hp.astype(BF16)
        h_ref[H + tm:, :] = hn.astype(BF16)
        o_ref[...] = x

    h = h_ref[...]
    up_g = jnp.dot(h, wg_ref[...], preferred_element_type=F32)
    n = tm + 2 * H
    wc = wc_ref[...]
    prev = pltpu.roll(up_g, 1, 0)[H:H + tm]
    nxt = pltpu.roll(up_g, n - 1, 0)[H:H + tm]
    u = prev * wc[0:1] + up_g[H:H + tm] * wc[1:2] + nxt * wc[2:3] + bc_ref[...]
    val = jnp.dot(h[H:H + tm], wv_ref[...], preferred_element_type=F32)
    act = (u * jax.nn.sigmoid(u) * val).astype(BF16)
    o_ref[...] += jnp.dot(act, wd_ref[...], preferred_element_type=F32)


def _ffn(hs, gain, w_up, w_conv, b_conv, w_down, *, tm, tf):
    R = hs.shape[0]
    nj = D_FF // tf
    hb = tm // CONV_HALO
    last_hb = R // CONV_HALO - 1
    row = lambda i, j: (i, 0)
    return pl.pallas_call(
        functools.partial(_ffn_kernel, tm=tm),
        grid=(R // tm, nj),
        in_specs=[
            pl.BlockSpec((tm, D_MODEL), row),
            pl.BlockSpec((CONV_HALO, D_MODEL), lambda i, j: (jnp.maximum(i * hb - 1, 0), 0)),
            pl.BlockSpec((CONV_HALO, D_MODEL), lambda i, j: (jnp.minimum((i + 1) * hb, last_hb), 0)),
            pl.BlockSpec((1, D_MODEL), lambda i, j: (0, 0)),
            pl.BlockSpec((D_MODEL, tf), lambda i, j: (0, j)),
            pl.BlockSpec((D_MODEL, tf), lambda i, j: (0, j + nj)),
            pl.BlockSpec((3, tf), lambda i, j: (0, j)),
            pl.BlockSpec((1, tf), lambda i, j: (0, j)),
            pl.BlockSpec((tf, D_MODEL), lambda i, j: (j, 0)),
        ],
        out_specs=pl.BlockSpec((tm, D_MODEL), row),
        out_shape=jax.ShapeDtypeStruct((R, D_MODEL), F32),
        scratch_shapes=[pltpu.VMEM((tm + 2 * CONV_HALO, D_MODEL), BF16)],
        compiler_params=_cparams("parallel", "arbitrary"),
        name="ffn",
    )(hs, hs, hs, gain, w_up, w_up, w_conv, b_conv, w_down)


def _rope_tables(n_tok, LP):
    rows = n_tok // GRID_W
    pad = LP - N_META - n_tok
    z_meta = jnp.zeros((N_META,), F32)
    z_pad = jnp.zeros((pad,), F32)
    pos_r = jnp.concatenate([z_meta, jnp.repeat(jnp.arange(rows, dtype=F32), GRID_W), z_pad])
    pos_c = jnp.concatenate([z_meta, jnp.tile(jnp.arange(GRID_W, dtype=F32), rows), z_pad])
    inv_freq = 1.0 / (ROPE_THETA ** (jnp.arange(ROPE_PAIRS, dtype=F32) / ROPE_PAIRS))
    ang_r = pos_r[:, None] * inv_freq[None, :]
    ang_c = pos_c[:, None] * inv_freq[None, :]
    cos_t = jnp.concatenate([jnp.cos(ang_r), jnp.cos(ang_r), jnp.cos(ang_c), jnp.cos(ang_c)], axis=-1)
    sin_t = jnp.concatenate([-jnp.sin(ang_r), jnp.sin(ang_r), -jnp.sin(ang_c), jnp.sin(ang_c)], axis=-1)
    return cos_t, sin_t


def _tiles(LP):
    big = 1280 if LP % 1280 == 0 else ROW_ALIGN
    return dict(tm_in=ROW_ALIGN, tq=256 if LP % 256 == 0 else 128, tk=big, tm_mix=ROW_ALIGN, tn_mix=512,
                tm_ffn=ROW_ALIGN, tf=512)


def kernel(x, meta_tokens, norm_mix, norm_ffn, w_in, b_gate, q_norm, k_norm, w_attn_br, w_four, w_out,
           w_up, w_conv, b_conv, w_down):
    B, n_tok, D = x.shape
    depth = w_in.shape[0]
    L = n_tok + N_META
    assert D == D_MODEL and n_tok % GRID_W == 0 and L % DFT_N2 == 0
    LP = -(-L // ROW_ALIGN) * ROW_ALIGN
    R = B * LP
    t = _tiles(LP)

    cos_t, sin_t = _rope_tables(n_tok, LP)
    dft = _dft_constants(L, LP)

    meta = jnp.broadcast_to(meta_tokens.astype(x.dtype)[None], (B, N_META, D))
    hs = jnp.concatenate([meta, x, jnp.zeros((B, LP - L, D), x.dtype)], axis=1).reshape(R, D)

    for i in range(depth):
        w_qkvf = w_in[i, :, :QKVF_WIDTH].astype(BF16)
        w_gates = w_in[i, :, QKVF_WIDTH:].astype(BF16)
        q, k, vt, f = _inproj(hs, norm_mix[i][None], w_qkvf, q_norm[i][None], k_norm[i][None],
                              cos_t, sin_t, tm=t["tm_in"], tiles_per_batch=LP // t["tm_in"])
        attn = _attention(q.reshape(B, LP, ATTN_WIDTH), k.reshape(B, LP, KV_WIDTH), vt,
                          L=L, tq=t["tq"], tk=t["tk"]).reshape(R, ATTN_WIDTH)
        four = _fourier(f.reshape(B, LP, FOURIER_WIDTH), dft, L=L).reshape(R, FOURIER_WIDTH)
        hs = _mix(hs, norm_mix[i][None], attn, four, w_gates, b_gate[i][None],
                  w_attn_br[i].astype(BF16), w_four[i].astype(BF16), w_out[i].astype(BF16),
                  tm=t["tm_mix"], tn=t["tn_mix"], L=L, LP=LP)
        hs = _ffn(hs, norm_ffn[i][None], w_up[i].astype(BF16), w_conv[i], b_conv[i][None],
                  w_down[i].astype(BF16), tm=t["tm_ffn"], tf=t["tf"])

    return hs.reshape(B, LP, D)[:, N_META:L]
```

```python
import functools
import math

import jax
import jax.numpy as jnp
import numpy as np
from jax import lax
from jax.experimental import pallas as pl
from jax.experimental.pallas import tpu as pltpu

F32 = jnp.float32
BF16 = jnp.bfloat16

D_MODEL = 2048
N_HEADS = 8
N_KV_HEADS = 2
HEAD_DIM = 128
GQA_GROUP = N_HEADS // N_KV_HEADS
ATTN_WIDTH = N_HEADS * HEAD_DIM
KV_WIDTH = N_KV_HEADS * HEAD_DIM
N_FOURIER_GROUPS = 8
FOURIER_GROUP_CH = 128
FOURIER_WIDTH = N_FOURIER_GROUPS * FOURIER_GROUP_CH
QKVF_WIDTH = ATTN_WIDTH + 2 * KV_WIDTH + FOURIER_WIDTH
OFF_K = ATTN_WIDTH
OFF_V = OFF_K + KV_WIDTH
OFF_F = OFF_V + KV_WIDTH
D_FF = 5632
N_META = 16
GRID_W = 64
NORM_EPS = 1e-6
ROPE_THETA = 10000.0
ROPE_PAIRS = HEAD_DIM // 4

DFT_N2 = 80
ROW_ALIGN = 640
CONV_HALO = 16
MASK_VALUE = -1e30
ONES_ROWS = 16
VMEM_LIMIT = 56 * 1024 * 1024


def _cparams(*sem):
    return pltpu.CompilerParams(dimension_semantics=sem, vmem_limit_bytes=VMEM_LIMIT)


def _rms(x):
    return x * lax.rsqrt(jnp.mean(x * x, axis=-1, keepdims=True) + NORM_EPS)


def _inproj_kernel(x_ref, g_ref, w_ref, qn_ref, kn_ref, cos_ref, sin_ref,
                   q_ref, k_ref, vt_ref, f_ref):
    h = (_rms(x_ref[...]) * g_ref[...]).astype(BF16)
    proj = jnp.dot(h, w_ref[...], preferred_element_type=F32)
    cos = cos_ref[...]
    sin = sin_ref[...]
    lane = lax.broadcasted_iota(jnp.int32, cos.shape, 1)
    first_half = (lane % (HEAD_DIM // 2)) < ROPE_PAIRS

    def norm_rope(t, gain):
        t = _rms(t) * gain
        partner = jnp.where(first_half,
                            pltpu.roll(t, HEAD_DIM - ROPE_PAIRS, 1),
                            pltpu.roll(t, ROPE_PAIRS, 1))
        return t * cos + partner * sin

    scale = math.log2(math.e) / math.sqrt(HEAD_DIM)
    for hh in range(N_HEADS):
        c = hh * HEAD_DIM
        q_ref[:, c:c + HEAD_DIM] = (norm_rope(proj[:, c:c + HEAD_DIM], qn_ref[...]) * scale).astype(BF16)
    for hh in range(N_KV_HEADS):
        c = hh * HEAD_DIM
        k_ref[:, c:c + HEAD_DIM] = norm_rope(proj[:, OFF_K + c:OFF_K + c + HEAD_DIM], kn_ref[...]).astype(BF16)
    vt_ref[...] = proj[:, OFF_V:OFF_F].T.astype(BF16)
    f_ref[...] = proj[:, OFF_F:].astype(BF16)


def _inproj(hs, gain, w_qkvf, qn, kn, cos_t, sin_t, *, tm, tiles_per_batch):
    R = hs.shape[0]
    row = lambda i: (i, 0)
    fixed = lambda i: (0, 0)
    tab = lambda i: (i % tiles_per_batch, 0)
    return pl.pallas_call(
        _inproj_kernel,
        grid=(R // tm,),
        in_specs=[
            pl.BlockSpec((tm, D_MODEL), row),
            pl.BlockSpec((1, D_MODEL), fixed),
            pl.BlockSpec((D_MODEL, QKVF_WIDTH), fixed),
            pl.BlockSpec((1, HEAD_DIM), fixed),
            pl.BlockSpec((1, HEAD_DIM), fixed),
            pl.BlockSpec((tm, HEAD_DIM), tab),
            pl.BlockSpec((tm, HEAD_DIM), tab),
        ],
        out_specs=[
            pl.BlockSpec((tm, ATTN_WIDTH), row),
            pl.BlockSpec((tm, KV_WIDTH), row),
            pl.BlockSpec((KV_WIDTH, tm), lambda i: (0, i)),
            pl.BlockSpec((tm, FOURIER_WIDTH), row),
        ],
        out_shape=[
            jax.ShapeDtypeStruct((R, ATTN_WIDTH), BF16),
            jax.ShapeDtypeStruct((R, KV_WIDTH), BF16),
            jax.ShapeDtypeStruct((KV_WIDTH, R), BF16),
            jax.ShapeDtypeStruct((R, FOURIER_WIDTH), BF16),
        ],
        compiler_params=_cparams("parallel"),
        name="inproj",
    )(hs, gain, w_qkvf, qn, kn, cos_t, sin_t)


def _attn_kernel(q_ref, k_ref, vt_ref, o_ref, s_ref, m_ref, acc_ref, *, tk, n_full, tail_valid, sum_on_mxu):
    has_tail = tail_valid > 0
    n_stage = n_full + (1 if has_tail else 0)
    off = 1 if has_tail else 0
    m_ref[...] = jnp.full(m_ref.shape, MASK_VALUE, F32)
    acc_ref[...] = jnp.zeros(acc_ref.shape, F32)

    def tile_start(t):
        if isinstance(t, int):
            return n_full * tk if (has_tail and t == 0) else (t - off) * tk
        return pl.multiple_of((t - off) * tk, tk)

    def scores(t, slot, masked):
        kj = k_ref[0, pl.ds(tile_start(t), tk), :]
        for g in range(GQA_GROUP):
            c = g * HEAD_DIM
            st = lax.dot_general(kj, q_ref[0, :, c:c + HEAD_DIM], (((1,), (1,)), ((), ())),
                                 preferred_element_type=F32)
            if masked:
                key = lax.broadcasted_iota(jnp.int32, st.shape, 0)
                st = jnp.where(key < tail_valid, st, MASK_VALUE)
            s_ref[slot, g] = st

    def softmax_pv(t, slot):
        vtj = vt_ref[:, pl.ds(tile_start(t), tk)]
        vt1 = jnp.concatenate([vtj, jnp.ones((ONES_ROWS, tk), BF16)], axis=0)
        for g in range(GQA_GROUP):
            st = s_ref[slot, g]
            m_prev = m_ref[g]
            m_new = jnp.maximum(m_prev, jnp.max(st, axis=0, keepdims=True))
            alpha = jnp.exp2(m_prev - m_new)
            if sum_on_mxu:
                p = jnp.exp2(st - m_new).astype(BF16)
                acc_ref[g] = alpha * acc_ref[g] + jnp.dot(vt1, p, preferred_element_type=F32)
            else:
                p = jnp.exp2(st - m_new)
                acc_ref[g, HEAD_DIM:HEAD_DIM + 1] = (alpha * acc_ref[g, HEAD_DIM:HEAD_DIM + 1]
                                                     + jnp.sum(p, axis=0, keepdims=True))
                acc_ref[g, :HEAD_DIM] = alpha * acc_ref[g, :HEAD_DIM] + jnp.dot(
                    vtj, p.astype(BF16), preferred_element_type=F32)
            m_ref[g] = m_new

    def stage(t, slot):
        if not isinstance(t, int) or t + 1 < n_stage:
            scores(t + 1, 1 - slot, False)
        softmax_pv(t, slot)

    scores(0, 0, has_tail)
    stage(0, 0)
    n_pairs = max(0, (n_stage - 2) // 2)
    if n_pairs > 0:
        def body(p, carry):
            stage(1 + 2 * p, 1)
            stage(2 + 2 * p, 0)
            return carry
        lax.fori_loop(0, n_pairs, body, 0)
    for t in range(1 + 2 * n_pairs, n_stage):
        stage(t, t % 2)
    for g in range(GQA_GROUP):
        c = g * HEAD_DIM
        acc = acc_ref[g]
        o_ref[0, :, c:c + HEAD_DIM] = (acc[:HEAD_DIM] / acc[HEAD_DIM:HEAD_DIM + 1]).T.astype(BF16)


def _attention(q, k, vt, *, L, tq, tk, sum_on_mxu):
    B, LP, _ = q.shape
    gw = GQA_GROUP * HEAD_DIM
    n_full = L // tk
    tail_valid = L - n_full * tk
    kern = functools.partial(_attn_kernel, tk=tk, n_full=n_full, tail_valid=tail_valid, sum_on_mxu=sum_on_mxu)
    return pl.pallas_call(
        kern,
        grid=(B, N_KV_HEADS, LP // tq),
        in_specs=[
            pl.BlockSpec((1, tq, gw), lambda b, h, i: (b, i, h)),
            pl.BlockSpec((1, LP, HEAD_DIM), lambda b, h, i: (b, 0, h)),
            pl.BlockSpec((HEAD_DIM, LP), lambda b, h, i: (h, b)),
        ],
        out_specs=pl.BlockSpec((1, tq, gw), lambda b, h, i: (b, i, h)),
        out_shape=jax.ShapeDtypeStruct((B, LP, ATTN_WIDTH), BF16),
        scratch_shapes=[
            pltpu.VMEM((2, GQA_GROUP, tk, tq), F32),
            pltpu.VMEM((GQA_GROUP, 1, tq), F32),
            pltpu.VMEM((GQA_GROUP, HEAD_DIM + ONES_ROWS, tq), F32),
        ],
        compiler_params=_cparams("parallel", "parallel", "parallel"),
        name="attention",
    )(q, k, vt)


def _fourier_kernel(f_ref, cs_ref, f1r_ref, f1i_ref, g_ref, o_ref, wr_ref, wi_ref, os_ref,
                    *, L, LP, n1, n1p, deep_unroll):
    chunk = ROW_ALIGN

    def channel_dft(c, carry):
        r0 = pl.multiple_of(c * chunk, chunk)
        z = jnp.dot(f_ref[0, pl.ds(r0, chunk), :], cs_ref[...], preferred_element_type=F32)
        wr_ref[pl.ds(r0, chunk), :] = z[:, :FOURIER_GROUP_CH]
        wi_ref[pl.ds(r0, chunk), :] = z[:, FOURIER_GROUP_CH:]
        return carry

    lax.fori_loop(0, LP // chunk, channel_dft, 0, unroll=2)

    def row_dft(n2, carry):
        rows = pl.ds(n2, n1p, stride=DFT_N2)
        zr = wr_ref[rows, :].astype(BF16)
        zi = wi_ref[rows, :].astype(BF16)
        a = (jnp.dot(f1r_ref[...], zr, preferred_element_type=F32)
             + jnp.dot(f1i_ref[...], zi, preferred_element_type=F32))
        wr_ref[rows, :] = a[:n1p]
        wi_ref[rows, :] = a[n1p:]
        return carry

    lax.fori_loop(0, DFT_N2, row_dft, 0, unroll=8 if deep_unroll else 4)

    os_ref[L:, :] = jnp.zeros((LP - L, FOURIER_GROUP_CH), F32)

    def col_dft(k1, carry):
        r0 = pl.multiple_of(k1 * DFT_N2, DFT_N2)
        a = jnp.concatenate([wr_ref[pl.ds(r0, DFT_N2), :].astype(BF16),
                             wi_ref[pl.ds(r0, DFT_N2), :].astype(BF16)], axis=0)
        os_ref[pl.ds(k1, DFT_N2, stride=n1), :] = jnp.dot(g_ref[k1], a, preferred_element_type=F32)
        return carry

    lax.fori_loop(0, n1, col_dft, 0, unroll=(41 if deep_unroll and n1 % 41 == 0 else 5) if n1 % 5 == 0 else 1)

    def emit(c, carry):
        r0 = pl.multiple_of(c * chunk, chunk)
        o_ref[0, pl.ds(r0, chunk), :] = os_ref[pl.ds(r0, chunk), :].astype(BF16)
        return carry

    lax.fori_loop(0, LP // chunk, emit, 0)


def _fourier(f, consts, *, L, deep_unroll):
    B, LP, _ = f.shape
    n1 = L // DFT_N2
    n1p = LP // DFT_N2
    cs, f1r, f1i, g = consts
    slab = pl.BlockSpec((1, LP, FOURIER_GROUP_CH), lambda b, c: (b, 0, c))
    fixed2 = lambda b, c: (0, 0)
    return pl.pallas_call(
        functools.partial(_fourier_kernel, L=L, LP=LP, n1=n1, n1p=n1p, deep_unroll=deep_unroll),
        grid=(B, N_FOURIER_GROUPS),
        in_specs=[
            slab,
            pl.BlockSpec((FOURIER_GROUP_CH, 2 * FOURIER_GROUP_CH), fixed2),
            pl.BlockSpec((2 * n1p, n1p), fixed2),
            pl.BlockSpec((2 * n1p, n1p), fixed2),
            pl.BlockSpec((n1, DFT_N2, 2 * DFT_N2), lambda b, c: (0, 0, 0), pipeline_mode=pl.Buffered(1)),
        ],
        out_specs=slab,
        out_shape=jax.ShapeDtypeStruct((B, LP, FOURIER_WIDTH), BF16),
        scratch_shapes=[pltpu.VMEM((LP, FOURIER_GROUP_CH), F32)] * 3,
        compiler_params=_cparams("parallel", "parallel"),
        name="fourier",
    )(f, cs, f1r, f1i, g)


def _dft_constants(L, LP):
    n1 = L // DFT_N2
    n1p = LP // DFT_N2
    k1 = np.arange(n1)
    ang1 = 2.0 * np.pi * np.outer(k1, k1) / n1
    c1, s1 = np.cos(ang1), np.sin(ang1)
    f1r = np.zeros((2 * n1p, n1p))
    f1i = np.zeros((2 * n1p, n1p))
    f1r[:n1, :n1] = c1
    f1r[n1p:n1p + n1, :n1] = -s1
    f1i[:n1, :n1] = s1
    f1i[n1p:n1p + n1, :n1] = c1
    n2 = np.arange(DFT_N2)
    k = k1[:, None, None] + n1 * n2[None, :, None]
    ang = 2.0 * np.pi * ((k * n2[None, None, :]) % L) / L
    g = np.concatenate([np.cos(ang), np.sin(ang)], axis=2) / math.sqrt(L)
    c = np.arange(FOURIER_GROUP_CH)
    angc = 2.0 * np.pi * np.outer(c, c) / FOURIER_GROUP_CH
    cs = np.concatenate([np.cos(angc), -np.sin(angc)], axis=1) / math.sqrt(FOURIER_GROUP_CH)
    return (jnp.asarray(cs, BF16), jnp.asarray(f1r, BF16), jnp.asarray(f1i, BF16), jnp.asarray(g, BF16))


def _mix_kernel(x_ref, g_ref, attn_ref, four_ref, wga_ref, wgf_ref, bga_ref, bgf_ref,
                wa_ref, wf_ref, wo_ref, o_ref, h_ref, *, tm, L, LP):
    j = pl.program_id(1)

    @pl.when(j == 0)
    def _():
        x = x_ref[...]
        h_ref[...] = (_rms(x) * g_ref[...]).astype(BF16)
        o_ref[...] = x

    h = h_ref[...]
    g_attn = jax.nn.sigmoid(jnp.dot(h, wga_ref[...], preferred_element_type=F32) + bga_ref[...])
    g_four = jax.nn.sigmoid(jnp.dot(h, wgf_ref[...], preferred_element_type=F32) + bgf_ref[...])
    a_br = jnp.dot(attn_ref[...], wa_ref[...], preferred_element_type=F32)
    s_br = jnp.dot(four_ref[...], wf_ref[...], preferred_element_type=F32)
    merged = (g_attn * a_br + g_four * s_br).astype(BF16)
    o_ref[...] += jnp.dot(merged, wo_ref[...], preferred_element_type=F32)

    @pl.when(j == pl.num_programs(1) - 1)
    def _():
        r = pl.program_id(0) * tm + lax.broadcasted_iota(jnp.int32, (tm, 1), 0)
        o_ref[...] = jnp.where((r % LP) < L, o_ref[...], 0.0)


def _mix(hs, gain, attn, four, w_gates, b_gates, w_a, w_f, w_o, *, tm, tn, L, LP):
    R = hs.shape[0]
    nj = D_MODEL // tn
    row = lambda i, j: (i, 0)
    return pl.pallas_call(
        functools.partial(_mix_kernel, tm=tm, L=L, LP=LP),
        grid=(R // tm, nj),
        in_specs=[
            pl.BlockSpec((tm, D_MODEL), row),
            pl.BlockSpec((1, D_MODEL), lambda i, j: (0, 0)),
            pl.BlockSpec((tm, ATTN_WIDTH), row),
            pl.BlockSpec((tm, FOURIER_WIDTH), row),
            pl.BlockSpec((D_MODEL, tn), lambda i, j: (0, j)),
            pl.BlockSpec((D_MODEL, tn), lambda i, j: (0, j + nj)),
            pl.BlockSpec((1, tn), lambda i, j: (0, j)),
            pl.BlockSpec((1, tn), lambda i, j: (0, j + nj)),
            pl.BlockSpec((ATTN_WIDTH, tn), lambda i, j: (0, j)),
            pl.BlockSpec((FOURIER_WIDTH, tn), lambda i, j: (0, j)),
            pl.BlockSpec((tn, D_MODEL), lambda i, j: (j, 0)),
        ],
        out_specs=pl.BlockSpec((tm, D_MODEL), row),
        out_shape=jax.ShapeDtypeStruct((R, D_MODEL), F32),
        scratch_shapes=[pltpu.VMEM((tm, D_MODEL), BF16)],
        compiler_params=_cparams("parallel", "arbitrary"),
        name="mix_out",
    )(hs, gain, attn, four, w_gates, w_gates, b_gates, b_gates, w_a, w_f, w_o)


def _ffn_kernel(x_ref, xp_ref, xn_ref, g_ref, wg_ref, wv_ref, wc_ref, bc_ref, wd_ref,
                o_ref, h_ref, *, tm):
    i = pl.program_id(0)
    j = pl.program_id(1)
    H = CONV_HALO

    @pl.when(j == 0)
    def _():
        x = x_ref[...]
        g = g_ref[...]
        h_ref[H:H + tm, :] = (_rms(x) * g).astype(BF16)
        hp = jnp.where(i > 0, _rms(xp_ref[...]) * g, 0.0)
        hn = jnp.where(i < pl.num_programs(0) - 1, _rms(xn_ref[...]) * g, 0.0)
        h_ref[0:H, :] = hp.astype(BF16)
        h_ref[H + tm:, :] = hn.astype(BF16)
        o_ref[...] = x

    h = h_ref[...]
    up_g = jnp.dot(h, wg_ref[...], preferred_element_type=F32)
    n = tm + 2 * H
    wc = wc_ref[...]
    prev = pltpu.roll(up_g, 1, 0)[H:H + tm]
    nxt = pltpu.roll(up_g, n - 1, 0)[H:H + tm]
    u = prev * wc[0:1] + up_g[H:H + tm] * wc[1:2] + nxt * wc[2:3] + bc_ref[...]
    val = jnp.dot(h[H:H + tm], wv_ref[...], preferred_element_type=F32)
    act = (u * jax.nn.sigmoid(u) * val).astype(BF16)
    o_ref[...] += jnp.dot(act, wd_ref[...], preferred_element_type=F32)


def _ffn(hs, gain, w_up, w_conv, b_conv, w_down, *, tm, tf):
    R = hs.shape[0]
    nj = D_FF // tf
    hb = tm // CONV_HALO
    last_hb = R // CONV_HALO - 1
    row = lambda i, j: (i, 0)
    return pl.pallas_call(
        functools.partial(_ffn_kernel, tm=tm),
        grid=(R // tm, nj),
        in_specs=[
            pl.BlockSpec((tm, D_MODEL), row),
            pl.BlockSpec((CONV_HALO, D_MODEL), lambda i, j: (jnp.maximum(i * hb - 1, 0), 0)),
            pl.BlockSpec((CONV_HALO, D_MODEL), lambda i, j: (jnp.minimum((i + 1) * hb, last_hb), 0)),
            pl.BlockSpec((1, D_MODEL), lambda i, j: (0, 0)),
            pl.BlockSpec((D_MODEL, tf), lambda i, j: (0, j)),
            pl.BlockSpec((D_MODEL, tf), lambda i, j: (0, j + nj)),
            pl.BlockSpec((3, tf), lambda i, j: (0, j)),
            pl.BlockSpec((1, tf), lambda i, j: (0, j)),
            pl.BlockSpec((tf, D_MODEL), lambda i, j: (j, 0)),
        ],
        out_specs=pl.BlockSpec((tm, D_MODEL), row),
        out_shape=jax.ShapeDtypeStruct((R, D_MODEL), F32),
        scratch_shapes=[pltpu.VMEM((tm + 2 * CONV_HALO, D_MODEL), BF16)],
        compiler_params=_cparams("parallel", "arbitrary"),
        name="ffn",
    )(hs, hs, hs, gain, w_up, w_up, w_conv, b_conv, w_down)


def _rope_tables(n_tok, LP):
    rows = n_tok // GRID_W
    pad = LP - N_META - n_tok
    z_meta = jnp.zeros((N_META,), F32)
    z_pad = jnp.zeros((pad,), F32)
    pos_r = jnp.concatenate([z_meta, jnp.repeat(jnp.arange(rows, dtype=F32), GRID_W), z_pad])
    pos_c = jnp.concatenate([z_meta, jnp.tile(jnp.arange(GRID_W, dtype=F32), rows), z_pad])
    inv_freq = 1.0 / (ROPE_THETA ** (jnp.arange(ROPE_PAIRS, dtype=F32) / ROPE_PAIRS))
    ang_r = pos_r[:, None] * inv_freq[None, :]
    ang_c = pos_c[:, None] * inv_freq[None, :]
    cos_t = jnp.concatenate([jnp.cos(ang_r), jnp.cos(ang_r), jnp.cos(ang_c), jnp.cos(ang_c)], axis=-1)
    sin_t = jnp.concatenate([-jnp.sin(ang_r), jnp.sin(ang_r), -jnp.sin(ang_c), jnp.sin(ang_c)], axis=-1)
    return cos_t, sin_t


def _tiles(LP):
    big = 1280 if LP % 1280 == 0 else ROW_ALIGN
    return dict(tm_in=ROW_ALIGN, tq=256 if LP % 256 == 0 else 128, tk=big, tm_mix=ROW_ALIGN, tn_mix=512,
                tm_ffn=ROW_ALIGN, tf=512)


def kernel(x, meta_tokens, norm_mix, norm_ffn, w_in, b_gate, q_norm, k_norm, w_attn_br, w_four, w_out,
           w_up, w_conv, b_conv, w_down):
    B, n_tok, D = x.shape
    depth = w_in.shape[0]
    L = n_tok + N_META
    assert D == D_MODEL and n_tok % GRID_W == 0 and L % DFT_N2 == 0
    LP = -(-L // ROW_ALIGN) * ROW_ALIGN
    R = B * LP
    t = _tiles(LP)

    cos_t, sin_t = _rope_tables(n_tok, LP)
    dft = _dft_constants(L, LP)

    meta = jnp.broadcast_to(meta_tokens.astype(x.dtype)[None], (B, N_META, D))
    hs = jnp.concatenate([meta, x, jnp.zeros((B, LP - L, D), x.dtype)], axis=1).reshape(R, D)

    for i in range(depth):
        w_qkvf = w_in[i, :, :QKVF_WIDTH].astype(BF16)
        w_gates = w_in[i, :, QKVF_WIDTH:].astype(BF16)
        q, k, vt, f = _inproj(hs, norm_mix[i][None], w_qkvf, q_norm[i][None], k_norm[i][None],
                              cos_t, sin_t, tm=t["tm_in"], tiles_per_batch=LP // t["tm_in"])
        attn = _attention(q.reshape(B, LP, ATTN_WIDTH), k.reshape(B, LP, KV_WIDTH), vt,
                          L=L, tq=t["tq"], tk=t["tk"], sum_on_mxu=(i % 2 == 1)).reshape(R, ATTN_WIDTH)
        four = _fourier(f.reshape(B, LP, FOURIER_WIDTH), dft, L=L, deep_unroll=(i % 2 == 1)).reshape(R, FOURIER_WIDTH)
        hs = _mix(hs, norm_mix[i][None], attn, four, w_gates, b_gate[i][None],
                  w_attn_br[i].astype(BF16), w_four[i].astype(BF16), w_out[i].astype(BF16),
                  tm=t["tm_mix"], tn=t["tn_mix"], L=L, LP=LP)
        hs = _ffn(hs, norm_ffn[i][None], w_up[i].astype(BF16), w_conv[i], b_conv[i][None],
                  w_down[i].astype(BF16), tm=t["tm_ffn"], tf=t["tf"])

    return hs.reshape(B, LP, D)[:, N_META:L]
```

```python
import functools
import math

import jax
import jax.numpy as jnp
import numpy as np
from jax import lax
from jax.experimental import pallas as pl
from jax.experimental.pallas import tpu as pltpu

F32 = jnp.float32
BF16 = jnp.bfloat16

D_MODEL = 2048
N_HEADS = 8
N_KV_HEADS = 2
HEAD_DIM = 128
GQA_GROUP = N_HEADS // N_KV_HEADS
ATTN_WIDTH = N_HEADS * HEAD_DIM
KV_WIDTH = N_KV_HEADS * HEAD_DIM
N_FOURIER_GROUPS = 8
FOURIER_GROUP_CH = 128
FOURIER_WIDTH = N_FOURIER_GROUPS * FOURIER_GROUP_CH
QKVF_WIDTH = ATTN_WIDTH + 2 * KV_WIDTH + FOURIER_WIDTH
OFF_K = ATTN_WIDTH
OFF_V = OFF_K + KV_WIDTH
OFF_F = OFF_V + KV_WIDTH
D_FF = 5632
N_META = 16
GRID_W = 64
NORM_EPS = 1e-6
ROPE_THETA = 10000.0
ROPE_PAIRS = HEAD_DIM // 4

DFT_N2 = 80
ROW_ALIGN = 640
CONV_HALO = 16
MASK_VALUE = -1e30
VMEM_LIMIT = 56 * 1024 * 1024


def _cparams(*sem):
    return pltpu.CompilerParams(dimension_semantics=sem, vmem_limit_bytes=VMEM_LIMIT)


def _rms(x):
    return x * lax.rsqrt(jnp.mean(x * x, axis=-1, keepdims=True) + NORM_EPS)


def _inproj_kernel(x_ref, g_ref, w_ref, qn_ref, kn_ref, cos_ref, sin_ref,
                   q_ref, k_ref, vt_ref, f_ref):
    h = (_rms(x_ref[...]) * g_ref[...]).astype(BF16)
    proj = jnp.dot(h, w_ref[...], preferred_element_type=F32)
    cos = cos_ref[...]
    sin = sin_ref[...]
    lane = lax.broadcasted_iota(jnp.int32, cos.shape, 1)
    first_half = (lane % (HEAD_DIM // 2)) < ROPE_PAIRS

    def norm_rope(t, gain):
        t = _rms(t) * gain
        partner = jnp.where(first_half,
                            pltpu.roll(t, HEAD_DIM - ROPE_PAIRS, 1),
                            pltpu.roll(t, ROPE_PAIRS, 1))
        return t * cos + partner * sin

    scale = math.log2(math.e) / math.sqrt(HEAD_DIM)
    for hh in range(N_HEADS):
        c = hh * HEAD_DIM
        q_ref[:, c:c + HEAD_DIM] = (norm_rope(proj[:, c:c + HEAD_DIM], qn_ref[...]) * scale).astype(BF16)
    for hh in range(N_KV_HEADS):
        c = hh * HEAD_DIM
        k_ref[:, c:c + HEAD_DIM] = norm_rope(proj[:, OFF_K + c:OFF_K + c + HEAD_DIM], kn_ref[...]).astype(BF16)
    vt_ref[...] = proj[:, OFF_V:OFF_F].T.astype(BF16)
    f_ref[...] = proj[:, OFF_F:].astype(BF16)


def _inproj(hs, gain, w_qkvf, qn, kn, cos_t, sin_t, *, tm, tiles_per_batch):
    R = hs.shape[0]
    row = lambda i: (i, 0)
    fixed = lambda i: (0, 0)
    tab = lambda i: (i % tiles_per_batch, 0)
    return pl.pallas_call(
        _inproj_kernel,
        grid=(R // tm,),
        in_specs=[
            pl.BlockSpec((tm, D_MODEL), row),
            pl.BlockSpec((1, D_MODEL), fixed),
            pl.BlockSpec((D_MODEL, QKVF_WIDTH), fixed),
            pl.BlockSpec((1, HEAD_DIM), fixed),
            pl.BlockSpec((1, HEAD_DIM), fixed),
            pl.BlockSpec((tm, HEAD_DIM), tab),
            pl.BlockSpec((tm, HEAD_DIM), tab),
        ],
        out_specs=[
            pl.BlockSpec((tm, ATTN_WIDTH), row),
            pl.BlockSpec((tm, KV_WIDTH), row),
            pl.BlockSpec((KV_WIDTH, tm), lambda i: (0, i)),
            pl.BlockSpec((tm, FOURIER_WIDTH), row),
        ],
        out_shape=[
            jax.ShapeDtypeStruct((R, ATTN_WIDTH), BF16),
            jax.ShapeDtypeStruct((R, KV_WIDTH), BF16),
            jax.ShapeDtypeStruct((KV_WIDTH, R), BF16),
            jax.ShapeDtypeStruct((R, FOURIER_WIDTH), BF16),
        ],
        compiler_params=_cparams("parallel"),
        name="inproj",
    )(hs, gain, w_qkvf, qn, kn, cos_t, sin_t)


def _attn_kernel(q_ref, k_ref, vt_ref, o_ref, s_ref, mt_ref, m_ref, l_ref, acc_ref, *, tk, n_full, tail_valid, early_max):
    has_tail = tail_valid > 0
    n_stage = n_full + (1 if has_tail else 0)
    off = 1 if has_tail else 0
    m_ref[...] = jnp.full(m_ref.shape, MASK_VALUE, F32)
    l_ref[...] = jnp.zeros(l_ref.shape, F32)
    acc_ref[...] = jnp.zeros(acc_ref.shape, F32)

    def tile_start(t):
        if isinstance(t, int):
            return n_full * tk if (has_tail and t == 0) else (t - off) * tk
        return pl.multiple_of((t - off) * tk, tk)

    def scores(t, slot, masked):
        kj = k_ref[0, pl.ds(tile_start(t), tk), :]
        for g in range(GQA_GROUP):
            c = g * HEAD_DIM
            st = lax.dot_general(kj, q_ref[0, :, c:c + HEAD_DIM], (((1,), (1,)), ((), ())),
                                 preferred_element_type=F32)
            if masked:
                key = lax.broadcasted_iota(jnp.int32, st.shape, 0)
                st = jnp.where(key < tail_valid, st, MASK_VALUE)
            s_ref[slot, g] = st
            if early_max:
                mt_ref[slot, g] = jnp.max(st, axis=0, keepdims=True)

    def softmax_pv(t, slot):
        vtj = vt_ref[:, pl.ds(tile_start(t), tk)]
        for g in range(GQA_GROUP):
            st = s_ref[slot, g]
            m_prev = m_ref[g]
            m_tile = mt_ref[slot, g] if early_max else jnp.max(st, axis=0, keepdims=True)
            m_new = jnp.maximum(m_prev, m_tile)
            alpha = jnp.exp2(m_prev - m_new)
            p = jnp.exp2(st - m_new)
            l_ref[g] = alpha * l_ref[g] + jnp.sum(p, axis=0, keepdims=True)
            acc_ref[g] = alpha * acc_ref[g] + jnp.dot(vtj, p.astype(BF16), preferred_element_type=F32)
            m_ref[g] = m_new

    def stage(t, slot):
        if not isinstance(t, int) or t + 1 < n_stage:
            scores(t + 1, 1 - slot, False)
        softmax_pv(t, slot)

    scores(0, 0, has_tail)
    stage(0, 0)
    n_pairs = max(0, (n_stage - 2) // 2)
    if n_pairs > 0:
        def body(p, carry):
            stage(1 + 2 * p, 1)
            stage(2 + 2 * p, 0)
            return carry
        lax.fori_loop(0, n_pairs, body, 0)
    for t in range(1 + 2 * n_pairs, n_stage):
        stage(t, t % 2)
    for g in range(GQA_GROUP):
        c = g * HEAD_DIM
        o_ref[0, :, c:c + HEAD_DIM] = (acc_ref[g] / l_ref[g]).T.astype(BF16)


def _attention(q, k, vt, *, L, tq, tk, early_max):
    B, LP, _ = q.shape
    gw = GQA_GROUP * HEAD_DIM
    n_full = L // tk
    tail_valid = L - n_full * tk
    kern = functools.partial(_attn_kernel, tk=tk, n_full=n_full, tail_valid=tail_valid, early_max=early_max)
    return pl.pallas_call(
        kern,
        grid=(B, N_KV_HEADS, LP // tq),
        in_specs=[
            pl.BlockSpec((1, tq, gw), lambda b, h, i: (b, i, h)),
            pl.BlockSpec((1, LP, HEAD_DIM), lambda b, h, i: (b, 0, h)),
            pl.BlockSpec((HEAD_DIM, LP), lambda b, h, i: (h, b)),
        ],
        out_specs=pl.BlockSpec((1, tq, gw), lambda b, h, i: (b, i, h)),
        out_shape=jax.ShapeDtypeStruct((B, LP, ATTN_WIDTH), BF16),
        scratch_shapes=[
            pltpu.VMEM((2, GQA_GROUP, tk, tq), F32),
            pltpu.VMEM((2, GQA_GROUP, 1, tq), F32),
            pltpu.VMEM((GQA_GROUP, 1, tq), F32),
            pltpu.VMEM((GQA_GROUP, 1, tq), F32),
            pltpu.VMEM((GQA_GROUP, HEAD_DIM, tq), F32),
        ],
        compiler_params=_cparams("parallel", "parallel", "parallel"),
        name="attention",
    )(q, k, vt)


def _fourier_kernel(f_ref, cs_ref, f1r_ref, f1i_ref, g_ref, o_ref, wr_ref, wi_ref, os_ref,
                    *, L, LP, n1, n1p, deep_unroll):
    chunk = ROW_ALIGN

    def channel_dft(c, carry):
        r0 = pl.multiple_of(c * chunk, chunk)
        z = jnp.dot(f_ref[0, pl.ds(r0, chunk), :], cs_ref[...], preferred_element_type=F32)
        wr_ref[pl.ds(r0, chunk), :] = z[:, :FOURIER_GROUP_CH]
        wi_ref[pl.ds(r0, chunk), :] = z[:, FOURIER_GROUP_CH:]
        return carry

    lax.fori_loop(0, LP // chunk, channel_dft, 0, unroll=13 if deep_unroll and (LP // chunk) % 13 == 0 else 2)

    def row_dft(n2, carry):
        rows = pl.ds(n2, n1p, stride=DFT_N2)
        zr = wr_ref[rows, :].astype(BF16)
        zi = wi_ref[rows, :].astype(BF16)
        a = (jnp.dot(f1r_ref[...], zr, preferred_element_type=F32)
             + jnp.dot(f1i_ref[...], zi, preferred_element_type=F32))
        wr_ref[rows, :] = a[:n1p]
        wi_ref[rows, :] = a[n1p:]
        return carry

    lax.fori_loop(0, DFT_N2, row_dft, 0, unroll=16 if deep_unroll else 8)

    os_ref[L:, :] = jnp.zeros((LP - L, FOURIER_GROUP_CH), F32)

    def col_dft(k1, carry):
        r0 = pl.multiple_of(k1 * DFT_N2, DFT_N2)
        a = jnp.concatenate([wr_ref[pl.ds(r0, DFT_N2), :].astype(BF16),
                             wi_ref[pl.ds(r0, DFT_N2), :].astype(BF16)], axis=0)
        os_ref[pl.ds(k1, DFT_N2, stride=n1), :] = jnp.dot(g_ref[k1], a, preferred_element_type=F32)
        return carry

    lax.fori_loop(0, n1, col_dft, 0, unroll=(41 if n1 % 41 == 0 else 5) if n1 % 5 == 0 else 1)

    def emit(c, carry):
        r0 = pl.multiple_of(c * chunk, chunk)
        o_ref[0, pl.ds(r0, chunk), :] = os_ref[pl.ds(r0, chunk), :].astype(BF16)
        return carry

    lax.fori_loop(0, LP // chunk, emit, 0)


def _fourier(f, consts, *, L, deep_unroll):
    B, LP, _ = f.shape
    n1 = L // DFT_N2
    n1p = LP // DFT_N2
    cs, f1r, f1i, g = consts
    slab = pl.BlockSpec((1, LP, FOURIER_GROUP_CH), lambda b, c: (b, 0, c))
    fixed2 = lambda b, c: (0, 0)
    return pl.pallas_call(
        functools.partial(_fourier_kernel, L=L, LP=LP, n1=n1, n1p=n1p, deep_unroll=deep_unroll),
        grid=(B, N_FOURIER_GROUPS),
        in_specs=[
            slab,
            pl.BlockSpec((FOURIER_GROUP_CH, 2 * FOURIER_GROUP_CH), fixed2),
            pl.BlockSpec((2 * n1p, n1p), fixed2),
            pl.BlockSpec((2 * n1p, n1p), fixed2),
            pl.BlockSpec((n1, DFT_N2, 2 * DFT_N2), lambda b, c: (0, 0, 0), pipeline_mode=pl.Buffered(1)),
        ],
        out_specs=slab,
        out_shape=jax.ShapeDtypeStruct((B, LP, FOURIER_WIDTH), BF16),
        scratch_shapes=[pltpu.VMEM((LP, FOURIER_GROUP_CH), F32)] * 3,
        compiler_params=_cparams("parallel", "parallel"),
        name="fourier",
    )(f, cs, f1r, f1i, g)


def _dft_constants(L, LP):
    n1 = L // DFT_N2
    n1p = LP // DFT_N2
    k1 = np.arange(n1)
    ang1 = 2.0 * np.pi * np.outer(k1, k1) / n1
    c1, s1 = np.cos(ang1), np.sin(ang1)
    f1r = np.zeros((2 * n1p, n1p))
    f1i = np.zeros((2 * n1p, n1p))
    f1r[:n1, :n1] = c1
    f1r[n1p:n1p + n1, :n1] = -s1
    f1i[:n1, :n1] = s1
    f1i[n1p:n1p + n1, :n1] = c1
    n2 = np.arange(DFT_N2)
    k = k1[:, None, None] + n1 * n2[None, :, None]
    ang = 2.0 * np.pi * ((k * n2[None, None, :]) % L) / L
    g = np.concatenate([np.cos(ang), np.sin(ang)], axis=2) / math.sqrt(L)
    c = np.arange(FOURIER_GROUP_CH)
    angc = 2.0 * np.pi * np.outer(c, c) / FOURIER_GROUP_CH
    cs = np.concatenate([np.cos(angc), -np.sin(angc)], axis=1) / math.sqrt(FOURIER_GROUP_CH)
    return (jnp.asarray(cs, BF16), jnp.asarray(f1r, BF16), jnp.asarray(f1i, BF16), jnp.asarray(g, BF16))


def _mix_kernel(x_ref, g_ref, attn_ref, four_ref, wga_ref, wgf_ref, bga_ref, bgf_ref,
                wa_ref, wf_ref, wo_ref, o_ref, h_ref, *, tm, L, LP):
    j = pl.program_id(1)

    @pl.when(j == 0)
    def _():
        x = x_ref[...]
        h_ref[...] = (_rms(x) * g_ref[...]).astype(BF16)
        o_ref[...] = x

    h = h_ref[...]
    g_attn = jax.nn.sigmoid(jnp.dot(h, wga_ref[...], preferred_element_type=F32) + bga_ref[...])
    g_four = jax.nn.sigmoid(jnp.dot(h, wgf_ref[...], preferred_element_type=F32) + bgf_ref[...])
    a_br = jnp.dot(attn_ref[...], wa_ref[...], preferred_element_type=F32)
    s_br = jnp.dot(four_ref[...], wf_ref[...], preferred_element_type=F32)
    merged = (g_attn * a_br + g_four * s_br).astype(BF16)
    o_ref[...] += jnp.dot(merged, wo_ref[...], preferred_element_type=F32)

    @pl.when(j == pl.num_programs(1) - 1)
    def _():
        r = pl.program_id(0) * tm + lax.broadcasted_iota(jnp.int32, (tm, 1), 0)
        o_ref[...] = jnp.where((r % LP) < L, o_ref[...], 0.0)


def _mix(hs, gain, attn, four, w_gates, b_gates, w_a, w_f, w_o, *, tm, tn, L, LP):
    R = hs.shape[0]
    nj = D_MODEL // tn
    row = lambda i, j: (i, 0)
    return pl.pallas_call(
        functools.partial(_mix_kernel, tm=tm, L=L, LP=LP),
        grid=(R // tm, nj),
        in_specs=[
            pl.BlockSpec((tm, D_MODEL), row),
            pl.BlockSpec((1, D_MODEL), lambda i, j: (0, 0)),
            pl.BlockSpec((tm, ATTN_WIDTH), row),
            pl.BlockSpec((tm, FOURIER_WIDTH), row),
            pl.BlockSpec((D_MODEL, tn), lambda i, j: (0, j)),
            pl.BlockSpec((D_MODEL, tn), lambda i, j: (0, j + nj)),
            pl.BlockSpec((1, tn), lambda i, j: (0, j)),
            pl.BlockSpec((1, tn), lambda i, j: (0, j + nj)),
            pl.BlockSpec((ATTN_WIDTH, tn), lambda i, j: (0, j)),
            pl.BlockSpec((FOURIER_WIDTH, tn), lambda i, j: (0, j)),
            pl.BlockSpec((tn, D_MODEL), lambda i, j: (j, 0)),
        ],
        out_specs=pl.BlockSpec((tm, D_MODEL), row),
        out_shape=jax.ShapeDtypeStruct((R, D_MODEL), F32),
        scratch_shapes=[pltpu.VMEM((tm, D_MODEL), BF16)],
        compiler_params=_cparams("parallel", "arbitrary"),
        name="mix_out",
    )(hs, gain, attn, four, w_gates, w_gates, b_gates, b_gates, w_a, w_f, w_o)


def _ffn_kernel(x_ref, xp_ref, xn_ref, g_ref, wg_ref, wv_ref, wc_ref, bc_ref, wd_ref,
                o_ref, h_ref, *, tm):
    i = pl.program_id(0)
    j = pl.program_id(1)
    H = CONV_HALO

    @pl.when(j == 0)
    def _():
        x = x_ref[...]
        g = g_ref[...]
        h_ref[H:H + tm, :] = (_rms(x) * g).astype(BF16)
        hp = jnp.where(i > 0, _rms(xp_ref[...]) * g, 0.0)
        hn = jnp.where(i < pl.num_programs(0) - 1, _rms(xn_ref[...]) * g, 0.0)
        h_ref[0:H, :] = hp.astype(BF16)
        h_ref[H + tm:, :] = hn.astype(BF16)
        o_ref[...] = x

    h = h_ref[...]
    up_g = jnp.dot(h, wg_ref[...], preferred_element_type=F32)
    n = tm + 2 * H
    wc = wc_ref[...]
    prev = pltpu.roll(up_g, 1, 0)[H:H + tm]
    nxt = pltpu.roll(up_g, n - 1, 0)[H:H + tm]
    u = prev * wc[0:1] + up_g[H:H + tm] * wc[1:2] + nxt * wc[2:3] + bc_ref[...]
    val = jnp.dot(h[H:H + tm], wv_ref[...], preferred_element_type=F32)
    act = (u * jax.nn.sigmoid(u) * val).astype(BF16)
    o_ref[...] += jnp.dot(act, wd_ref[...], preferred_element_type=F32)


def _ffn(hs, gain, w_up, w_conv, b_conv, w_down, *, tm, tf):
    R = hs.shape[0]
    nj = D_FF // tf
    hb = tm // CONV_HALO
    last_hb = R // CONV_HALO - 1
    row = lambda i, j: (i, 0)
    return pl.pallas_call(
        functools.partial(_ffn_kernel, tm=tm),
        grid=(R // tm, nj),
        in_specs=[
            pl.BlockSpec((tm, D_MODEL), row),
            pl.BlockSpec((CONV_HALO, D_MODEL), lambda i, j: (jnp.maximum(i * hb - 1, 0), 0)),
            pl.BlockSpec((CONV_HALO, D_MODEL), lambda i, j: (jnp.minimum((i + 1) * hb, last_hb), 0)),
            pl.BlockSpec((1, D_MODEL), lambda i, j: (0, 0)),
            pl.BlockSpec((D_MODEL, tf), lambda i, j: (0, j)),
            pl.BlockSpec((D_MODEL, tf), lambda i, j: (0, j + nj)),
            pl.BlockSpec((3, tf), lambda i, j: (0, j)),
            pl.BlockSpec((1, tf), lambda i, j: (0, j)),
            pl.BlockSpec((tf, D_MODEL), lambda i, j: (j, 0)),
        ],
        out_specs=pl.BlockSpec((tm, D_MODEL), row),
        out_shape=jax.ShapeDtypeStruct((R, D_MODEL), F32),
        scratch_shapes=[pltpu.VMEM((tm + 2 * CONV_HALO, D_MODEL), BF16)],
        compiler_params=_cparams("parallel", "arbitrary"),
        name="ffn",
    )(hs, hs, hs, gain, w_up, w_up, w_conv, b_conv, w_down)


def _rope_tables(n_tok, LP):
    rows = n_tok // GRID_W
    pad = LP - N_META - n_tok
    z_meta = jnp.zeros((N_META,), F32)
    z_pad = jnp.zeros((pad,), F32)
    pos_r = jnp.concatenate([z_meta, jnp.repeat(jnp.arange(rows, dtype=F32), GRID_W), z_pad])
    pos_c = jnp.concatenate([z_meta, jnp.tile(jnp.arange(GRID_W, dtype=F32), rows), z_pad])
    inv_freq = 1.0 / (ROPE_THETA ** (jnp.arange(ROPE_PAIRS, dtype=F32) / ROPE_PAIRS))
    ang_r = pos_r[:, None] * inv_freq[None, :]
    ang_c = pos_c[:, None] * inv_freq[None, :]
    cos_t = jnp.concatenate([jnp.cos(ang_r), jnp.cos(ang_r), jnp.cos(ang_c), jnp.cos(ang_c)], axis=-1)
    sin_t = jnp.concatenate([-jnp.sin(ang_r), jnp.sin(ang_r), -jnp.sin(ang_c), jnp.sin(ang_c)], axis=-1)
    return cos_t, sin_t


def _tiles(LP):
    big = 1280 if LP % 1280 == 0 else ROW_ALIGN
    return dict(tm_in=ROW_ALIGN, tq=256 if LP % 256 == 0 else 128, tk=big, tm_mix=ROW_ALIGN, tn_mix=512,
                tm_ffn=ROW_ALIGN, tf=512)


def kernel(x, meta_tokens, norm_mix, norm_ffn, w_in, b_gate, q_norm, k_norm, w_attn_br, w_four, w_out,
           w_up, w_conv, b_conv, w_down):
    B, n_tok, D = x.shape
    depth = w_in.shape[0]
    L = n_tok + N_META
    assert D == D_MODEL and n_tok % GRID_W == 0 and L % DFT_N2 == 0
    LP = -(-L // ROW_ALIGN) * ROW_ALIGN
    R = B * LP
    t = _tiles(LP)

    cos_t, sin_t = _rope_tables(n_tok, LP)
    dft = _dft_constants(L, LP)

    meta = jnp.broadcast_to(meta_tokens.astype(x.dtype)[None], (B, N_META, D))
    hs = jnp.concatenate([meta, x, jnp.zeros((B, LP - L, D), x.dtype)], axis=1).reshape(R, D)

    for i in range(depth):
        w_qkvf = w_in[i, :, :QKVF_WIDTH].astype(BF16)
        w_gates = w_in[i, :, QKVF_WIDTH:].astype(BF16)
        q, k, vt, f = _inproj(hs, norm_mix[i][None], w_qkvf, q_norm[i][None], k_norm[i][None],
                              cos_t, sin_t, tm=t["tm_in"], tiles_per_batch=LP // t["tm_in"])
        attn = _attention(q.reshape(B, LP, ATTN_WIDTH), k.reshape(B, LP, KV_WIDTH), vt,
                          L=L, tq=t["tq"], tk=t["tk"], early_max=(i % 2 == 1)).reshape(R, ATTN_WIDTH)
        four = _fourier(f.reshape(B, LP, FOURIER_WIDTH), dft, L=L, deep_unroll=(i % 2 == 1)).reshape(R, FOURIER_WIDTH)
        hs = _mix(hs, norm_mix[i][None], attn, four, w_gates, b_gate[i][None],
                  w_attn_br[i].astype(BF16), w_four[i].astype(BF16), w_out[i].astype(BF16),
                  tm=t["tm_mix"], tn=t["tn_mix"], L=L, LP=LP)
        hs = _ffn(hs, norm_ffn[i][None], w_up[i].astype(BF16), w_conv[i], b_conv[i][None],
                  w_down[i].astype(BF16), tm=t["tm_ffn"], tf=t["tf"])

    return hs.reshape(B, LP, D)[:, N_META:L]
```

```python
import functools
import math

import jax
import jax.numpy as jnp
import numpy as np
from jax import lax
from jax.experimental import pallas as pl
from jax.experimental.pallas import tpu as pltpu

F32 = jnp.float32
BF16 = jnp.bfloat16

D_MODEL = 2048
N_HEADS = 8
N_KV_HEADS = 2
HEAD_DIM = 128
GQA_GROUP = N_HEADS // N_KV_HEADS
ATTN_WIDTH = N_HEADS * HEAD_DIM
KV_WIDTH = N_KV_HEADS * HEAD_DIM
N_FOURIER_GROUPS = 8
FOURIER_GROUP_CH = 128
FOURIER_WIDTH = N_FOURIER_GROUPS * FOURIER_GROUP_CH
QKVF_WIDTH = ATTN_WIDTH + 2 * KV_WIDTH + FOURIER_WIDTH
OFF_K = ATTN_WIDTH
OFF_V = OFF_K + KV_WIDTH
OFF_F = OFF_V + KV_WIDTH
D_FF = 5632
N_META = 16
GRID_W = 64
NORM_EPS = 1e-6
ROPE_THETA = 10000.0
ROPE_PAIRS = HEAD_DIM // 4

DFT_N2 = 80
ROW_ALIGN = 640
CONV_HALO = 16
MASK_VALUE = -1e30
VMEM_LIMIT = 56 * 1024 * 1024


def _cparams(*sem):
    return pltpu.CompilerParams(dimension_semantics=sem, vmem_limit_bytes=VMEM_LIMIT)


def _rms(x):
    return x * lax.rsqrt(jnp.mean(x * x, axis=-1, keepdims=True) + NORM_EPS)


def _inproj_kernel(x_ref, g_ref, w_ref, qn_ref, kn_ref, cos_ref, sin_ref,
                   q_ref, k_ref, vt_ref, f_ref):
    h = (_rms(x_ref[...]) * g_ref[...]).astype(BF16)
    proj = jnp.dot(h, w_ref[...], preferred_element_type=F32)
    cos = cos_ref[...]
    sin = sin_ref[...]
    lane = lax.broadcasted_iota(jnp.int32, cos.shape, 1)
    first_half = (lane % (HEAD_DIM // 2)) < ROPE_PAIRS

    def norm_rope(t, gain):
        t = _rms(t) * gain
        partner = jnp.where(first_half,
                            pltpu.roll(t, HEAD_DIM - ROPE_PAIRS, 1),
                            pltpu.roll(t, ROPE_PAIRS, 1))
        return t * cos + partner * sin

    scale = math.log2(math.e) / math.sqrt(HEAD_DIM)
    for hh in range(N_HEADS):
        c = hh * HEAD_DIM
        q_ref[:, c:c + HEAD_DIM] = (norm_rope(proj[:, c:c + HEAD_DIM], qn_ref[...]) * scale).astype(BF16)
    for hh in range(N_KV_HEADS):
        c = hh * HEAD_DIM
        k_ref[:, c:c + HEAD_DIM] = norm_rope(proj[:, OFF_K + c:OFF_K + c + HEAD_DIM], kn_ref[...]).astype(BF16)
    vt_ref[...] = proj[:, OFF_V:OFF_F].T.astype(BF16)
    f_ref[...] = proj[:, OFF_F:].astype(BF16)


def _inproj(hs, gain, w_qkvf, qn, kn, cos_t, sin_t, *, tm, tiles_per_batch):
    R = hs.shape[0]
    row = lambda i: (i, 0)
    fixed = lambda i: (0, 0)
    tab = lambda i: (i % tiles_per_batch, 0)
    return pl.pallas_call(
        _inproj_kernel,
        grid=(R // tm,),
        in_specs=[
            pl.BlockSpec((tm, D_MODEL), row),
            pl.BlockSpec((1, D_MODEL), fixed),
            pl.BlockSpec((D_MODEL, QKVF_WIDTH), fixed),
            pl.BlockSpec((1, HEAD_DIM), fixed),
            pl.BlockSpec((1, HEAD_DIM), fixed),
            pl.BlockSpec((tm, HEAD_DIM), tab),
            pl.BlockSpec((tm, HEAD_DIM), tab),
        ],
        out_specs=[
            pl.BlockSpec((tm, ATTN_WIDTH), row),
            pl.BlockSpec((tm, KV_WIDTH), row),
            pl.BlockSpec((KV_WIDTH, tm), lambda i: (0, i)),
            pl.BlockSpec((tm, FOURIER_WIDTH), row),
        ],
        out_shape=[
            jax.ShapeDtypeStruct((R, ATTN_WIDTH), BF16),
            jax.ShapeDtypeStruct((R, KV_WIDTH), BF16),
            jax.ShapeDtypeStruct((KV_WIDTH, R), BF16),
            jax.ShapeDtypeStruct((R, FOURIER_WIDTH), BF16),
        ],
        compiler_params=_cparams("parallel"),
        name="inproj",
    )(hs, gain, w_qkvf, qn, kn, cos_t, sin_t)


def _attn_kernel(q_ref, k_ref, vt_ref, o_ref, s_ref, mt_ref, m_ref, l_ref, acc_ref, *, tk, n_full, tail_valid):
    has_tail = tail_valid > 0
    n_stage = n_full + (1 if has_tail else 0)
    off = 1 if has_tail else 0
    m_ref[...] = jnp.full(m_ref.shape, MASK_VALUE, F32)
    l_ref[...] = jnp.zeros(l_ref.shape, F32)
    acc_ref[...] = jnp.zeros(acc_ref.shape, F32)

    def tile_start(t):
        if isinstance(t, int):
            return n_full * tk if (has_tail and t == 0) else (t - off) * tk
        return pl.multiple_of((t - off) * tk, tk)

    def scores(t, slot, masked):
        kj = k_ref[0, pl.ds(tile_start(t), tk), :]
        for g in range(GQA_GROUP):
            c = g * HEAD_DIM
            st = lax.dot_general(kj, q_ref[0, :, c:c + HEAD_DIM], (((1,), (1,)), ((), ())),
                                 preferred_element_type=F32)
            if masked:
                key = lax.broadcasted_iota(jnp.int32, st.shape, 0)
                st = jnp.where(key < tail_valid, st, MASK_VALUE)
            s_ref[slot, g] = st
            mt_ref[slot, g] = jnp.max(st, axis=0, keepdims=True)

    def softmax_pv(t, slot):
        vtj = vt_ref[:, pl.ds(tile_start(t), tk)]
        for g in range(GQA_GROUP):
            st = s_ref[slot, g]
            m_prev = m_ref[g]
            m_new = jnp.maximum(m_prev, mt_ref[slot, g])
            alpha = jnp.exp2(m_prev - m_new)
            p = jnp.exp2(st - m_new)
            l_ref[g] = alpha * l_ref[g] + jnp.sum(p, axis=0, keepdims=True)
            acc_ref[g] = alpha * acc_ref[g] + jnp.dot(vtj, p.astype(BF16), preferred_element_type=F32)
            m_ref[g] = m_new

    def stage(t, slot):
        if not isinstance(t, int) or t + 1 < n_stage:
            scores(t + 1, 1 - slot, False)
        softmax_pv(t, slot)

    scores(0, 0, has_tail)
    stage(0, 0)
    n_pairs = max(0, (n_stage - 2) // 2)
    if n_pairs > 0:
        def body(p, carry):
            stage(1 + 2 * p, 1)
            stage(2 + 2 * p, 0)
            return carry
        lax.fori_loop(0, n_pairs, body, 0)
    for t in range(1 + 2 * n_pairs, n_stage):
        stage(t, t % 2)
    for g in range(GQA_GROUP):
        c = g * HEAD_DIM
        o_ref[0, :, c:c + HEAD_DIM] = (acc_ref[g] / l_ref[g]).T.astype(BF16)


def _attention(q, k, vt, *, L, tq, tk):
    B, LP, _ = q.shape
    gw = GQA_GROUP * HEAD_DIM
    n_full = L // tk
    tail_valid = L - n_full * tk
    kern = functools.partial(_attn_kernel, tk=tk, n_full=n_full, tail_valid=tail_valid)
    return pl.pallas_call(
        kern,
        grid=(B, N_KV_HEADS, LP // tq),
        in_specs=[
            pl.BlockSpec((1, tq, gw), lambda b, h, i: (b, i, h)),
            pl.BlockSpec((1, LP, HEAD_DIM), lambda b, h, i: (b, 0, h)),
            pl.BlockSpec((HEAD_DIM, LP), lambda b, h, i: (h, b)),
        ],
        out_specs=pl.BlockSpec((1, tq, gw), lambda b, h, i: (b, i, h)),
        out_shape=jax.ShapeDtypeStruct((B, LP, ATTN_WIDTH), BF16),
        scratch_shapes=[
            pltpu.VMEM((2, GQA_GROUP, tk, tq), F32),
            pltpu.VMEM((2, GQA_GROUP, 1, tq), F32),
            pltpu.VMEM((GQA_GROUP, 1, tq), F32),
            pltpu.VMEM((GQA_GROUP, 1, tq), F32),
            pltpu.VMEM((GQA_GROUP, HEAD_DIM, tq), F32),
        ],
        compiler_params=_cparams("parallel", "parallel", "parallel"),
        name="attention",
    )(q, k, vt)


def _fourier_kernel(f_ref, cs_ref, f1r_ref, f1i_ref, g_ref, o_ref, wr_ref, wi_ref, os_ref,
                    *, L, LP, n1, n1p):
    chunk = ROW_ALIGN

    def channel_dft(c, carry):
        r0 = pl.multiple_of(c * chunk, chunk)
        z = jnp.dot(f_ref[0, pl.ds(r0, chunk), :], cs_ref[...], preferred_element_type=F32)
        wr_ref[pl.ds(r0, chunk), :] = z[:, :FOURIER_GROUP_CH]
        wi_ref[pl.ds(r0, chunk), :] = z[:, FOURIER_GROUP_CH:]
        return carry

    lax.fori_loop(0, LP // chunk, channel_dft, 0, unroll=13 if (LP // chunk) % 13 == 0 else 2)

    def row_dft(n2, carry):
        rows = pl.ds(n2, n1p, stride=DFT_N2)
        zr = wr_ref[rows, :].astype(BF16)
        zi = wi_ref[rows, :].astype(BF16)
        a = (jnp.dot(f1r_ref[...], zr, preferred_element_type=F32)
             + jnp.dot(f1i_ref[...], zi, preferred_element_type=F32))
        wr_ref[rows, :] = a[:n1p]
        wi_ref[rows, :] = a[n1p:]
        return carry

    lax.fori_loop(0, DFT_N2, row_dft, 0, unroll=16)

    os_ref[L:, :] = jnp.zeros((LP - L, FOURIER_GROUP_CH), F32)

    def col_dft(k1, carry):
        r0 = pl.multiple_of(k1 * DFT_N2, DFT_N2)
        a = jnp.concatenate([wr_ref[pl.ds(r0, DFT_N2), :].astype(BF16),
                             wi_ref[pl.ds(r0, DFT_N2), :].astype(BF16)], axis=0)
        os_ref[pl.ds(k1, DFT_N2, stride=n1), :] = jnp.dot(g_ref[k1], a, preferred_element_type=F32)
        return carry

    lax.fori_loop(0, n1, col_dft, 0, unroll=(41 if n1 % 41 == 0 else 5) if n1 % 5 == 0 else 1)

    def emit(c, carry):
        r0 = pl.multiple_of(c * chunk, chunk)
        o_ref[0, pl.ds(r0, chunk), :] = os_ref[pl.ds(r0, chunk), :].astype(BF16)
        return carry

    lax.fori_loop(0, LP // chunk, emit, 0)


def _fourier(f, consts, *, L):
    B, LP, _ = f.shape
    n1 = L // DFT_N2
    n1p = LP // DFT_N2
    cs, f1r, f1i, g = consts
    slab = pl.BlockSpec((1, LP, FOURIER_GROUP_CH), lambda b, c: (b, 0, c))
    fixed2 = lambda b, c: (0, 0)
    return pl.pallas_call(
        functools.partial(_fourier_kernel, L=L, LP=LP, n1=n1, n1p=n1p),
        grid=(B, N_FOURIER_GROUPS),
        in_specs=[
            slab,
            pl.BlockSpec((FOURIER_GROUP_CH, 2 * FOURIER_GROUP_CH), fixed2),
            pl.BlockSpec((2 * n1p, n1p), fixed2),
            pl.BlockSpec((2 * n1p, n1p), fixed2),
            pl.BlockSpec((n1, DFT_N2, 2 * DFT_N2), lambda b, c: (0, 0, 0), pipeline_mode=pl.Buffered(1)),
        ],
        out_specs=slab,
        out_shape=jax.ShapeDtypeStruct((B, LP, FOURIER_WIDTH), BF16),
        scratch_shapes=[pltpu.VMEM((LP, FOURIER_GROUP_CH), F32)] * 3,
        compiler_params=_cparams("parallel", "parallel"),
        name="fourier",
    )(f, cs, f1r, f1i, g)


def _dft_constants(L, LP):
    n1 = L // DFT_N2
    n1p = LP // DFT_N2
    k1 = np.arange(n1)
    ang1 = 2.0 * np.pi * np.outer(k1, k1) / n1
    c1, s1 = np.cos(ang1), np.sin(ang1)
    f1r = np.zeros((2 * n1p, n1p))
    f1i = np.zeros((2 * n1p, n1p))
    f1r[:n1, :n1] = c1
    f1r[n1p:n1p + n1, :n1] = -s1
    f1i[:n1, :n1] = s1
    f1i[n1p:n1p + n1, :n1] = c1
    n2 = np.arange(DFT_N2)
    k = k1[:, None, None] + n1 * n2[None, :, None]
    ang = 2.0 * np.pi * ((k * n2[None, None, :]) % L) / L
    g = np.concatenate([np.cos(ang), np.sin(ang)], axis=2) / math.sqrt(L)
    c = np.arange(FOURIER_GROUP_CH)
    angc = 2.0 * np.pi * np.outer(c, c) / FOURIER_GROUP_CH
    cs = np.concatenate([np.cos(angc), -np.sin(angc)], axis=1) / math.sqrt(FOURIER_GROUP_CH)
    return (jnp.asarray(cs, BF16), jnp.asarray(f1r, BF16), jnp.asarray(f1i, BF16), jnp.asarray(g, BF16))


def _mix_kernel(x_ref, g_ref, attn_ref, four_ref, wga_ref, wgf_ref, bga_ref, bgf_ref,
                wa_ref, wf_ref, wo_ref, o_ref, h_ref, *, tm, L, LP):
    j = pl.program_id(1)

    @pl.when(j == 0)
    def _():
        x = x_ref[...]
        h_ref[...] = (_rms(x) * g_ref[...]).astype(BF16)
        o_ref[...] = x

    h = h_ref[...]
    g_attn = jax.nn.sigmoid(jnp.dot(h, wga_ref[...], preferred_element_type=F32) + bga_ref[...])
    g_four = jax.nn.sigmoid(jnp.dot(h, wgf_ref[...], preferred_element_type=F32) + bgf_ref[...])
    a_br = jnp.dot(attn_ref[...], wa_ref[...], preferred_element_type=F32)
    s_br = jnp.dot(four_ref[...], wf_ref[...], preferred_element_type=F32)
    merged = (g_attn * a_br + g_four * s_br).astype(BF16)
    o_ref[...] += jnp.dot(merged, wo_ref[...], preferred_element_type=F32)

    @pl.when(j == pl.num_programs(1) - 1)
    def _():
        r = pl.program_id(0) * tm + lax.broadcasted_iota(jnp.int32, (tm, 1), 0)
        o_ref[...] = jnp.where((r % LP) < L, o_ref[...], 0.0)


def _mix(hs, gain, attn, four, w_gates, b_gates, w_a, w_f, w_o, *, tm, tn, L, LP):
    R = hs.shape[0]
    nj = D_MODEL // tn
    row = lambda i, j: (i, 0)
    return pl.pallas_call(
        functools.partial(_mix_kernel, tm=tm, L=L, LP=LP),
        grid=(R // tm, nj),
        in_specs=[
            pl.BlockSpec((tm, D_MODEL), row),
            pl.BlockSpec((1, D_MODEL), lambda i, j: (0, 0)),
            pl.BlockSpec((tm, ATTN_WIDTH), row),
            pl.BlockSpec((tm, FOURIER_WIDTH), row),
            pl.BlockSpec((D_MODEL, tn), lambda i, j: (0, j)),
            pl.BlockSpec((D_MODEL, tn), lambda i, j: (0, j + nj)),
            pl.BlockSpec((1, tn), lambda i, j: (0, j)),
            pl.BlockSpec((1, tn), lambda i, j: (0, j + nj)),
            pl.BlockSpec((ATTN_WIDTH, tn), lambda i, j: (0, j)),
            pl.BlockSpec((FOURIER_WIDTH, tn), lambda i, j: (0, j)),
            pl.BlockSpec((tn, D_MODEL), lambda i, j: (j, 0)),
        ],
        out_specs=pl.BlockSpec((tm, D_MODEL), row),
        out_shape=jax.ShapeDtypeStruct((R, D_MODEL), F32),
        scratch_shapes=[pltpu.VMEM((tm, D_MODEL), BF16)],
        compiler_params=_cparams("parallel", "arbitrary"),
        name="mix_out",
    )(hs, gain, attn, four, w_gates, w_gates, b_gates, b_gates, w_a, w_f, w_o)


def _ffn_kernel(x_ref, xp_ref, xn_ref, g_ref, wg_ref, wv_ref, wc_ref, bc_ref, wd_ref,
                o_ref, h_ref, *, tm):
    i = pl.program_id(0)
    j = pl.program_id(1)
    H = CONV_HALO

    @pl.when(j == 0)
    def _():
        x = x_ref[...]
        g = g_ref[...]
        h_ref[H:H + tm, :] = (_rms(x) * g).astype(BF16)
        hp = jnp.where(i > 0, _rms(xp_ref[...]) * g, 0.0)
        hn = jnp.where(i < pl.num_programs(0) - 1, _rms(xn_ref[...]) * g, 0.0)
        h_ref[0:H, :] = hp.astype(BF16)
        h_ref[H + tm:, :] = hn.astype(BF16)
        o_ref[...] = x

    h = h_ref[...]
    up_g = jnp.dot(h, wg_ref[...], preferred_element_type=F32)
    n = tm + 2 * H
    wc = wc_ref[...]
    prev = pltpu.roll(up_g, 1, 0)[H:H + tm]
    nxt = pltpu.roll(up_g, n - 1, 0)[H:H + tm]
    u = prev * wc[0:1] + up_g[H:H + tm] * wc[1:2] + nxt * wc[2:3] + bc_ref[...]
    val = jnp.dot(h[H:H + tm], wv_ref[...], preferred_element_type=F32)
    act = (u * jax.nn.sigmoid(u) * val).astype(BF16)
    o_ref[...] += jnp.dot(act, wd_ref[...], preferred_element_type=F32)


def _ffn(hs, gain, w_up, w_conv, b_conv, w_down, *, tm, tf):
    R = hs.shape[0]
    nj = D_FF // tf
    hb = tm // CONV_HALO
    last_hb = R // CONV_HALO - 1
    row = lambda i, j: (i, 0)
    return pl.pallas_call(
        functools.partial(_ffn_kernel, tm=tm),
        grid=(R // tm, nj),
        in_specs=[
            pl.BlockSpec((tm, D_MODEL), row),
            pl.BlockSpec((CONV_HALO, D_MODEL), lambda i, j: (jnp.maximum(i * hb - 1, 0), 0)),
            pl.BlockSpec((CONV_HALO, D_MODEL), lambda i, j: (jnp.minimum((i + 1) * hb, last_hb), 0)),
            pl.BlockSpec((1, D_MODEL), lambda i, j: (0, 0)),
            pl.BlockSpec((D_MODEL, tf), lambda i, j: (0, j)),
            pl.BlockSpec((D_MODEL, tf), lambda i, j: (0, j + nj)),
            pl.BlockSpec((3, tf), lambda i, j: (0, j)),
            pl.BlockSpec((1, tf), lambda i, j: (0, j)),
            pl.BlockSpec((tf, D_MODEL), lambda i, j: (j, 0)),
        ],
        out_specs=pl.BlockSpec((tm, D_MODEL), row),
        out_shape=jax.ShapeDtypeStruct((R, D_MODEL), F32),
        scratch_shapes=[pltpu.VMEM((tm + 2 * CONV_HALO, D_MODEL), BF16)],
        compiler_params=_cparams("parallel", "arbitrary"),
        name="ffn",
    )(hs, hs, hs, gain, w_up, w_up, w_conv, b_conv, w_down)


def _rope_tables(n_tok, LP):
    rows = n_tok // GRID_W
    pad = LP - N_META - n_tok
    z_meta = jnp.zeros((N_META,), F32)
    z_pad = jnp.zeros((pad,), F32)
    pos_r = jnp.concatenate([z_meta, jnp.repeat(jnp.arange(rows, dtype=F32), GRID_W), z_pad])
    pos_c = jnp.concatenate([z_meta, jnp.tile(jnp.arange(GRID_W, dtype=F32), rows), z_pad])
    inv_freq = 1.0 / (ROPE_THETA ** (jnp.arange(ROPE_PAIRS, dtype=F32) / ROPE_PAIRS))
    ang_r = pos_r[:, None] * inv_freq[None, :]
    ang_c = pos_c[:, None] * inv_freq[None, :]
    cos_t = jnp.concatenate([jnp.cos(ang_r), jnp.cos(ang_r), jnp.cos(ang_c), jnp.cos(ang_c)], axis=-1)
    sin_t = jnp.concatenate([-jnp.sin(ang_r), jnp.sin(ang_r), -jnp.sin(ang_c), jnp.sin(ang_c)], axis=-1)
    return cos_t, sin_t


def _tiles(LP):
    big = 1280 if LP % 1280 == 0 else ROW_ALIGN
    return dict(tm_in=ROW_ALIGN, tq=256 if LP % 256 == 0 else 128, tk=big, tm_mix=ROW_ALIGN, tn_mix=512,
                tm_ffn=ROW_ALIGN, tf=512)


def kernel(x, meta_tokens, norm_mix, norm_ffn, w_in, b_gate, q_norm, k_norm, w_attn_br, w_four, w_out,
           w_up, w_conv, b_conv, w_down):
    B, n_tok, D = x.shape
    depth = w_in.shape[0]
    L = n_tok + N_META
    assert D == D_MODEL and n_tok % GRID_W == 0 and L % DFT_N2 == 0
    LP = -(-L // ROW_ALIGN) * ROW_ALIGN
    R = B * LP
    t = _tiles(LP)

    cos_t, sin_t = _rope_tables(n_tok, LP)
    dft = _dft_constants(L, LP)

    meta = jnp.broadcast_to(meta_tokens.astype(x.dtype)[None], (B, N_META, D))
    hs = jnp.concatenate([meta, x, jnp.zeros((B, LP - L, D), x.dtype)], axis=1).reshape(R, D)

    for i in range(depth):
        w_qkvf = w_in[i, :, :QKVF_WIDTH].astype(BF16)
        w_gates = w_in[i, :, QKVF_WIDTH:].astype(BF16)
        q, k, vt, f = _inproj(hs, norm_mix[i][None], w_qkvf, q_norm[i][None], k_norm[i][None],
                              cos_t, sin_t, tm=t["tm_in"], tiles_per_batch=LP // t["tm_in"])
        attn = _attention(q.reshape(B, LP, ATTN_WIDTH), k.reshape(B, LP, KV_WIDTH), vt,
                          L=L, tq=t["tq"], tk=t["tk"]).reshape(R, ATTN_WIDTH)
        four = _fourier(f.reshape(B, LP, FOURIER_WIDTH), dft, L=L).reshape(R, FOURIER_WIDTH)
        hs = _mix(hs, norm_mix[i][None], attn, four, w_gates, b_gate[i][None],
                  w_attn_br[i].astype(BF16), w_four[i].astype(BF16), w_out[i].astype(BF16),
                  tm=t["tm_mix"], tn=t["tn_mix"], L=L, LP=LP)
        hs = _ffn(hs, norm_ffn[i][None], w_up[i].astype(BF16), w_conv[i], b_conv[i][None],
                  w_down[i].astype(BF16), tm=t["tm_ffn"], tf=t["tf"])

    return hs.reshape(B, LP, D)[:, N_META:L]
```

```python
import functools
import math

import jax
import jax.numpy as jnp
import numpy as np
from jax import lax
from jax.experimental import pallas as pl
from jax.experimental.pallas import tpu as pltpu

F32 = jnp.float32
BF16 = jnp.bfloat16

D_MODEL = 2048
N_HEADS = 8
N_KV_HEADS = 2
HEAD_DIM = 128
GQA_GROUP = N_HEADS // N_KV_HEADS
ATTN_WIDTH = N_HEADS * HEAD_DIM
KV_WIDTH = N_KV_HEADS * HEAD_DIM
N_FOURIER_GROUPS = 8
FOURIER_GROUP_CH = 128
FOURIER_WIDTH = N_FOURIER_GROUPS * FOURIER_GROUP_CH
QKVF_WIDTH = ATTN_WIDTH + 2 * KV_WIDTH + FOURIER_WIDTH
OFF_K = ATTN_WIDTH
OFF_V = OFF_K + KV_WIDTH
OFF_F = OFF_V + KV_WIDTH
D_FF = 5632
N_META = 16
GRID_W = 64
NORM_EPS = 1e-6
ROPE_THETA = 10000.0
ROPE_PAIRS = HEAD_DIM // 4

DFT_N2 = 80
ROW_ALIGN = 640
CONV_HALO = 16
MASK_VALUE = -1e30
VMEM_LIMIT = 56 * 1024 * 1024


def _cparams(*sem):
    return pltpu.CompilerParams(dimension_semantics=sem, vmem_limit_bytes=VMEM_LIMIT)


def _rms(x):
    return x * lax.rsqrt(jnp.mean(x * x, axis=-1, keepdims=True) + NORM_EPS)


def _inproj_kernel(x_ref, g_ref, w_ref, qn_ref, kn_ref, cos_ref, sin_ref,
                   q_ref, k_ref, vt_ref, f_ref):
    h = (_rms(x_ref[...]) * g_ref[...]).astype(BF16)
    proj = jnp.dot(h, w_ref[...], preferred_element_type=F32)
    cos = cos_ref[...]
    sin = sin_ref[...]
    lane = lax.broadcasted_iota(jnp.int32, cos.shape, 1)
    first_half = (lane % (HEAD_DIM // 2)) < ROPE_PAIRS

    def norm_rope(t, gain):
        t = _rms(t) * gain
        partner = jnp.where(first_half,
                            pltpu.roll(t, HEAD_DIM - ROPE_PAIRS, 1),
                            pltpu.roll(t, ROPE_PAIRS, 1))
        return t * cos + partner * sin

    scale = math.log2(math.e) / math.sqrt(HEAD_DIM)
    for hh in range(N_HEADS):
        c = hh * HEAD_DIM
        q_ref[:, c:c + HEAD_DIM] = (norm_rope(proj[:, c:c + HEAD_DIM], qn_ref[...]) * scale).astype(BF16)
    for hh in range(N_KV_HEADS):
        c = hh * HEAD_DIM
        k_ref[:, c:c + HEAD_DIM] = norm_rope(proj[:, OFF_K + c:OFF_K + c + HEAD_DIM], kn_ref[...]).astype(BF16)
    vt_ref[...] = proj[:, OFF_V:OFF_F].T.astype(BF16)
    f_ref[...] = proj[:, OFF_F:].astype(BF16)


def _inproj(hs, gain, w_qkvf, qn, kn, cos_t, sin_t, *, tm, tiles_per_batch):
    R = hs.shape[0]
    row = lambda i: (i, 0)
    fixed = lambda i: (0, 0)
    tab = lambda i: (i % tiles_per_batch, 0)
    return pl.pallas_call(
        _inproj_kernel,
        grid=(R // tm,),
        in_specs=[
            pl.BlockSpec((tm, D_MODEL), row),
            pl.BlockSpec((1, D_MODEL), fixed),
            pl.BlockSpec((D_MODEL, QKVF_WIDTH), fixed),
            pl.BlockSpec((1, HEAD_DIM), fixed),
            pl.BlockSpec((1, HEAD_DIM), fixed),
            pl.BlockSpec((tm, HEAD_DIM), tab),
            pl.BlockSpec((tm, HEAD_DIM), tab),
        ],
        out_specs=[
            pl.BlockSpec((tm, ATTN_WIDTH), row),
            pl.BlockSpec((tm, KV_WIDTH), row),
            pl.BlockSpec((KV_WIDTH, tm), lambda i: (0, i)),
            pl.BlockSpec((tm, FOURIER_WIDTH), row),
        ],
        out_shape=[
            jax.ShapeDtypeStruct((R, ATTN_WIDTH), BF16),
            jax.ShapeDtypeStruct((R, KV_WIDTH), BF16),
            jax.ShapeDtypeStruct((KV_WIDTH, R), BF16),
            jax.ShapeDtypeStruct((R, FOURIER_WIDTH), BF16),
        ],
        compiler_params=_cparams("parallel"),
        name="inproj",
    )(hs, gain, w_qkvf, qn, kn, cos_t, sin_t)


def _attn_kernel(q_ref, k_ref, vt_ref, o_ref, s_ref, mt_ref, m_ref, l_ref, acc_ref, *, tk, n_full, tail_valid):
    has_tail = tail_valid > 0
    n_stage = n_full + (1 if has_tail else 0)
    off = 1 if has_tail else 0
    m_ref[...] = jnp.full(m_ref.shape, MASK_VALUE, F32)
    l_ref[...] = jnp.zeros(l_ref.shape, F32)
    acc_ref[...] = jnp.zeros(acc_ref.shape, F32)

    def tile_start(t):
        if isinstance(t, int):
            return n_full * tk if (has_tail and t == 0) else (t - off) * tk
        return pl.multiple_of((t - off) * tk, tk)

    def scores(t, slot, masked):
        kj = k_ref[0, pl.ds(tile_start(t), tk), :]
        for g in range(GQA_GROUP):
            c = g * HEAD_DIM
            st = lax.dot_general(kj, q_ref[0, :, c:c + HEAD_DIM], (((1,), (1,)), ((), ())),
                                 preferred_element_type=F32)
            if masked:
                key = lax.broadcasted_iota(jnp.int32, st.shape, 0)
                st = jnp.where(key < tail_valid, st, MASK_VALUE)
            s_ref[slot, g] = st
            mt_ref[slot, g] = jnp.max(st, axis=0, keepdims=True)

    def softmax_pv(t, slot):
        vtj = vt_ref[:, pl.ds(tile_start(t), tk)]
        for g in range(GQA_GROUP):
            st = s_ref[slot, g]
            m_prev = m_ref[g]
            m_new = jnp.maximum(m_prev, mt_ref[slot, g])
            alpha = jnp.exp2(m_prev - m_new)
            p = jnp.exp2(st - m_new)
            l_ref[g] = alpha * l_ref[g] + jnp.sum(p, axis=0, keepdims=True)
            acc_ref[g] = alpha * acc_ref[g] + jnp.dot(vtj, p.astype(BF16), preferred_element_type=F32)
            m_ref[g] = m_new

    def stage(t, slot):
        if not isinstance(t, int) or t + 1 < n_stage:
            scores(t + 1, 1 - slot, False)
        softmax_pv(t, slot)

    scores(0, 0, has_tail)
    stage(0, 0)
    n_pairs = max(0, (n_stage - 2) // 2)
    if n_pairs > 0:
        def body(p, carry):
            stage(1 + 2 * p, 1)
            stage(2 + 2 * p, 0)
            return carry
        lax.fori_loop(0, n_pairs, body, 0)
    for t in range(1 + 2 * n_pairs, n_stage):
        stage(t, t % 2)
    for g in range(GQA_GROUP):
        c = g * HEAD_DIM
        o_ref[0, :, c:c + HEAD_DIM] = (acc_ref[g] / l_ref[g]).T.astype(BF16)


def _attention(q, k, vt, *, L, tq, tk):
    B, LP, _ = q.shape
    gw = GQA_GROUP * HEAD_DIM
    n_full = L // tk
    tail_valid = L - n_full * tk
    kern = functools.partial(_attn_kernel, tk=tk, n_full=n_full, tail_valid=tail_valid)
    return pl.pallas_call(
        kern,
        grid=(B, N_KV_HEADS, LP // tq),
        in_specs=[
            pl.BlockSpec((1, tq, gw), lambda b, h, i: (b, i, h)),
            pl.BlockSpec((1, LP, HEAD_DIM), lambda b, h, i: (b, 0, h)),
            pl.BlockSpec((HEAD_DIM, LP), lambda b, h, i: (h, b)),
        ],
        out_specs=pl.BlockSpec((1, tq, gw), lambda b, h, i: (b, i, h)),
        out_shape=jax.ShapeDtypeStruct((B, LP, ATTN_WIDTH), BF16),
        scratch_shapes=[
            pltpu.VMEM((2, GQA_GROUP, tk, tq), F32),
            pltpu.VMEM((2, GQA_GROUP, 1, tq), F32),
            pltpu.VMEM((GQA_GROUP, 1, tq), F32),
            pltpu.VMEM((GQA_GROUP, 1, tq), F32),
            pltpu.VMEM((GQA_GROUP, HEAD_DIM, tq), F32),
        ],
        compiler_params=_cparams("parallel", "parallel", "parallel"),
        name="attention",
    )(q, k, vt)


def _fourier_kernel(f_ref, cs_ref, f1r_ref, f1i_ref, g_ref, o_ref, wr_ref, wi_ref, os_ref,
                    *, L, LP, n1, n1p):
    chunk = ROW_ALIGN

    def channel_dft(c, carry):
        r0 = pl.multiple_of(c * chunk, chunk)
        z = jnp.dot(f_ref[0, pl.ds(r0, chunk), :], cs_ref[...], preferred_element_type=F32)
        wr_ref[pl.ds(r0, chunk), :] = z[:, :FOURIER_GROUP_CH]
        wi_ref[pl.ds(r0, chunk), :] = z[:, FOURIER_GROUP_CH:]
        return carry

    lax.fori_loop(0, LP // chunk, channel_dft, 0, unroll=13 if (LP // chunk) % 13 == 0 else 2)

    def row_dft(n2, carry):
        rows = pl.ds(n2, n1p, stride=DFT_N2)
        zr = wr_ref[rows, :].astype(BF16)
        zi = wi_ref[rows, :].astype(BF16)
        a = (jnp.dot(f1r_ref[...], zr, preferred_element_type=F32)
             + jnp.dot(f1i_ref[...], zi, preferred_element_type=F32))
        wr_ref[rows, :] = a[:n1p]
        wi_ref[rows, :] = a[n1p:]
        return carry

    lax.fori_loop(0, DFT_N2, row_dft, 0, unroll=16)

    os_ref[L:, :] = jnp.zeros((LP - L, FOURIER_GROUP_CH), F32)

    def col_dft(k1, carry):
        r0 = pl.multiple_of(k1 * DFT_N2, DFT_N2)
        a = jnp.concatenate([wr_ref[pl.ds(r0, DFT_N2), :].astype(BF16),
                             wi_ref[pl.ds(r0, DFT_N2), :].astype(BF16)], axis=0)
        os_ref[pl.ds(k1, DFT_N2, stride=n1), :] = jnp.dot(g_ref[k1], a, preferred_element_type=F32)
        return carry

    lax.fori_loop(0, n1, col_dft, 0, unroll=(41 if n1 % 41 == 0 else 5) if n1 % 5 == 0 else 1)

    def emit(c, carry):
        r0 = pl.multiple_of(c * chunk, chunk)
        o_ref[0, pl.ds(r0, chunk), :] = os_ref[pl.ds(r0, chunk), :].astype(BF16)
        return carry

    lax.fori_loop(0, LP // chunk, emit, 0)


def _fourier(f, consts, *, L):
    B, LP, _ = f.shape
    n1 = L // DFT_N2
    n1p = LP // DFT_N2
    cs, f1r, f1i, g = consts
    slab = pl.BlockSpec((1, LP, FOURIER_GROUP_CH), lambda b, c: (b, 0, c))
    fixed2 = lambda b, c: (0, 0)
    return pl.pallas_call(
        functools.partial(_fourier_kernel, L=L, LP=LP, n1=n1, n1p=n1p),
        grid=(B, N_FOURIER_GROUPS),
        in_specs=[
            slab,
            pl.BlockSpec((FOURIER_GROUP_CH, 2 * FOURIER_GROUP_CH), fixed2),
            pl.BlockSpec((2 * n1p, n1p), fixed2),
            pl.BlockSpec((2 * n1p, n1p), fixed2),
            pl.BlockSpec((n1, DFT_N2, 2 * DFT_N2), lambda b, c: (0, 0, 0), pipeline_mode=pl.Buffered(1)),
        ],
        out_specs=slab,
        out_shape=jax.ShapeDtypeStruct((B, LP, FOURIER_WIDTH), BF16),
        scratch_shapes=[pltpu.VMEM((LP, FOURIER_GROUP_CH), F32)] * 3,
        compiler_params=_cparams("parallel", "parallel"),
        name="fourier",
    )(f, cs, f1r, f1i, g)


def _dft_constants(L, LP):
    n1 = L // DFT_N2
    n1p = LP // DFT_N2
    k1 = np.arange(n1)
    ang1 = 2.0 * np.pi * np.outer(k1, k1) / n1
    c1, s1 = np.cos(ang1), np.sin(ang1)
    f1r = np.zeros((2 * n1p, n1p))
    f1i = np.zeros((2 * n1p, n1p))
    f1r[:n1, :n1] = c1
    f1r[n1p:n1p + n1, :n1] = -s1
    f1i[:n1, :n1] = s1
    f1i[n1p:n1p + n1, :n1] = c1
    n2 = np.arange(DFT_N2)
    k = k1[:, None, None] + n1 * n2[None, :, None]
    ang = 2.0 * np.pi * ((k * n2[None, None, :]) % L) / L
    g = np.concatenate([np.cos(ang), np.sin(ang)], axis=2) / math.sqrt(L)
    c = np.arange(FOURIER_GROUP_CH)
    angc = 2.0 * np.pi * np.outer(c, c) / FOURIER_GROUP_CH
    cs = np.concatenate([np.cos(angc), -np.sin(angc)], axis=1) / math.sqrt(FOURIER_GROUP_CH)
    return (jnp.asarray(cs, BF16), jnp.asarray(f1r, BF16), jnp.asarray(f1i, BF16), jnp.asarray(g, BF16))


def _mix_kernel(x_ref, g_ref, attn_ref, four_ref, wga_ref, wgf_ref, bga_ref, bgf_ref,
                wa_ref, wf_ref, wo_ref, o_ref, h_ref, *, tm, L, LP):
    j = pl.program_id(1)

    @pl.when(j == 0)
    def _():
        x = x_ref[...]
        h_ref[...] = (_rms(x) * g_ref[...]).astype(BF16)
        o_ref[...] = x

    h = h_ref[...]
    g_attn = jax.nn.sigmoid(jnp.dot(h, wga_ref[...], preferred_element_type=F32) + bga_ref[...])
    g_four = jax.nn.sigmoid(jnp.dot(h, wgf_ref[...], preferred_element_type=F32) + bgf_ref[...])
    a_br = jnp.dot(attn_ref[...], wa_ref[...], preferred_element_type=F32)
    s_br = jnp.dot(four_ref[...], wf_ref[...], preferred_element_type=F32)
    merged = (g_attn * a_br + g_four * s_br).astype(BF16)
    o_ref[...] += jnp.dot(merged, wo_ref[...], preferred_element_type=F32)

    @pl.when(j == pl.num_programs(1) - 1)
    def _():
        r = pl.program_id(0) * tm + lax.broadcasted_iota(jnp.int32, (tm, 1), 0)
        o_ref[...] = jnp.where((r % LP) < L, o_ref[...], 0.0)


def _mix(hs, gain, attn, four, w_gates, b_gates, w_a, w_f, w_o, *, tm, tn, L, LP):
    R = hs.shape[0]
    nj = D_MODEL // tn
    row = lambda i, j: (i, 0)
    return pl.pallas_call(
        functools.partial(_mix_kernel, tm=tm, L=L, LP=LP),
        grid=(R // tm, nj),
        in_specs=[
            pl.BlockSpec((tm, D_MODEL), row),
            pl.BlockSpec((1, D_MODEL), lambda i, j: (0, 0)),
            pl.BlockSpec((tm, ATTN_WIDTH), row),
            pl.BlockSpec((tm, FOURIER_WIDTH), row),
            pl.BlockSpec((D_MODEL, tn), lambda i, j: (0, j)),
            pl.BlockSpec((D_MODEL, tn), lambda i, j: (0, j + nj)),
            pl.BlockSpec((1, tn), lambda i, j: (0, j)),
            pl.BlockSpec((1, tn), lambda i, j: (0, j + nj)),
            pl.BlockSpec((ATTN_WIDTH, tn), lambda i, j: (0, j)),
            pl.BlockSpec((FOURIER_WIDTH, tn), lambda i, j: (0, j)),
            pl.BlockSpec((tn, D_MODEL), lambda i, j: (j, 0)),
        ],
        out_specs=pl.BlockSpec((tm, D_MODEL), row),
        out_shape=jax.ShapeDtypeStruct((R, D_MODEL), F32),
        scratch_shapes=[pltpu.VMEM((tm, D_MODEL), BF16)],
        compiler_params=_cparams("parallel", "arbitrary"),
        name="mix_out",
    )(hs, gain, attn, four, w_gates, w_gates, b_gates, b_gates, w_a, w_f, w_o)


def _ffn_kernel(x_ref, xp_ref, xn_ref, g_ref, wg_ref, wv_ref, wc_ref, bc_ref, wd_ref,
                o_ref, h_ref, *, tm):
    i = pl.program_id(0)
    j = pl.program_id(1)
    H = CONV_HALO

    @pl.when(j == 0)
    def _():
        x = x_ref[...]
        g = g_ref[...]
        h_ref[H:H + tm, :] = (_rms(x) * g).astype(BF16)
        hp = jnp.where(i > 0, _rms(xp_ref[...]) * g, 0.0)
        hn = jnp.where(i < pl.num_programs(0) - 1, _rms(xn_ref[...]) * g, 0.0)
        h_ref[0:H, :] = hp.astype(BF16)
        h_ref[H + tm:, :] = hn.astype(BF16)
        o_ref[...] = x

    h = h_ref[...]
    up_g = jnp.dot(h, wg_ref[...], preferred_element_type=F32)
    n = tm + 2 * H
    wc = wc_ref[...]
    prev = pltpu.roll(up_g, 1, 0)[H:H + tm]
    nxt = pltpu.roll(up_g, n - 1, 0)[H:H + tm]
    u = prev * wc[0:1] + up_g[H:H + tm] * wc[1:2] + nxt * wc[2:3] + bc_ref[...]
    val = jnp.dot(h[H:H + tm], wv_ref[...], preferred_element_type=F32)
    act = (u * jax.nn.sigmoid(u) * val).astype(BF16)
    o_ref[...] += jnp.dot(act, wd_ref[...], preferred_element_type=F32)


def _ffn(hs, gain, w_up, w_conv, b_conv, w_down, *, tm, tf):
    R = hs.shape[0]
    nj = D_FF // tf
    hb = tm // CONV_HALO
    last_hb = R // CONV_HALO - 1
    row = lambda i, j: (i, 0)
    return pl.pallas_call(
        functools.partial(_ffn_kernel, tm=tm),
        grid=(R // tm, nj),
        in_specs=[
            pl.BlockSpec((tm, D_MODEL), row),
            pl.BlockSpec((CONV_HALO, D_MODEL), lambda i, j: (jnp.maximum(i * hb - 1, 0), 0)),
            pl.BlockSpec((CONV_HALO, D_MODEL), lambda i, j: (jnp.minimum((i + 1) * hb, last_hb), 0)),
            pl.BlockSpec((1, D_MODEL), lambda i, j: (0, 0)),
            pl.BlockSpec((D_MODEL, tf), lambda i, j: (0, j)),
            pl.BlockSpec((D_MODEL, tf), lambda i, j: (0, j + nj)),
            pl.BlockSpec((3, tf), lambda i, j: (0, j)),
            pl.BlockSpec((1, tf), lambda i, j: (0, j)),
            pl.BlockSpec((tf, D_MODEL), lambda i, j: (j, 0)),
        ],
        out_specs=pl.BlockSpec((tm, D_MODEL), row),
        out_shape=jax.ShapeDtypeStruct((R, D_MODEL), F32),
        scratch_shapes=[pltpu.VMEM((tm + 2 * CONV_HALO, D_MODEL), BF16)],
        compiler_params=_cparams("parallel", "arbitrary"),
        name="ffn",
    )(hs, hs, hs, gain, w_up, w_up, w_conv, b_conv, w_down)


def _rope_tables(n_tok, LP):
    rows = n_tok // GRID_W
    pad = LP - N_META - n_tok
    pos_r = np.concatenate([np.zeros(N_META), np.repeat(np.arange(rows), GRID_W), np.zeros(pad)])
    pos_c = np.concatenate([np.zeros(N_META), np.tile(np.arange(GRID_W), rows), np.zeros(pad)])
    inv_freq = 1.0 / (ROPE_THETA ** (np.arange(ROPE_PAIRS) / ROPE_PAIRS))
    ang_r = pos_r[:, None] * inv_freq[None, :]
    ang_c = pos_c[:, None] * inv_freq[None, :]
    cos_t = np.concatenate([np.cos(ang_r), np.cos(ang_r), np.cos(ang_c), np.cos(ang_c)], axis=-1)
    sin_t = np.concatenate([-np.sin(ang_r), np.sin(ang_r), -np.sin(ang_c), np.sin(ang_c)], axis=-1)
    return jnp.asarray(cos_t, F32), jnp.asarray(sin_t, F32)


def _tiles(LP):
    big = 1280 if LP % 1280 == 0 else ROW_ALIGN
    return dict(tm_in=ROW_ALIGN, tq=256 if LP % 256 == 0 else 128, tk=big, tm_mix=ROW_ALIGN, tn_mix=512,
                tm_ffn=ROW_ALIGN, tf=512)


def kernel(x, meta_tokens, norm_mix, norm_ffn, w_in, b_gate, q_norm, k_norm, w_attn_br, w_four, w_out,
           w_up, w_conv, b_conv, w_down):
    B, n_tok, D = x.shape
    depth = w_in.shape[0]
    L = n_tok + N_META
    assert D == D_MODEL and n_tok % GRID_W == 0 and L % DFT_N2 == 0
    LP = -(-L // ROW_ALIGN) * ROW_ALIGN
    R = B * LP
    t = _tiles(LP)

    cos_t, sin_t = _rope_tables(n_tok, LP)
    dft = _dft_constants(L, LP)

    meta = jnp.broadcast_to(meta_tokens.astype(x.dtype)[None], (B, N_META, D))
    hs = jnp.concatenate([meta, x, jnp.zeros((B, LP - L, D), x.dtype)], axis=1).reshape(R, D)

    for i in range(depth):
        w_qkvf = w_in[i, :, :QKVF_WIDTH].astype(BF16)
        w_gates = w_in[i, :, QKVF_WIDTH:].astype(BF16)
        q, k, vt, f = _inproj(hs, norm_mix[i][None], w_qkvf, q_norm[i][None], k_norm[i][None],
                              cos_t, sin_t, tm=t["tm_in"], tiles_per_batch=LP // t["tm_in"])
        attn = _attention(q.reshape(B, LP, ATTN_WIDTH), k.reshape(B, LP, KV_WIDTH), vt,
                          L=L, tq=t["tq"], tk=t["tk"] if i == 0 else 640).reshape(R, ATTN_WIDTH)
        four = _fourier(f.reshape(B, LP, FOURIER_WIDTH), dft, L=L).reshape(R, FOURIER_WIDTH)
        hs = _mix(hs, norm_mix[i][None], attn, four, w_gates, b_gate[i][None],
                  w_attn_br[i].astype(BF16), w_four[i].astype(BF16), w_out[i].astype(BF16),
                  tm=t["tm_mix"], tn=t["tn_mix"], L=L, LP=LP)
        hs = _ffn(hs, norm_ffn[i][None], w_up[i].astype(BF16), w_conv[i], b_conv[i][None],
                  w_down[i].astype(BF16), tm=t["tm_ffn"], tf=t["tf"])

    return hs.reshape(B, LP, D)[:, N_META:L]
```

```python
import functools
import math

import jax
import jax.numpy as jnp
import numpy as np
from jax import lax
from jax.experimental import pallas as pl
from jax.experimental.pallas import tpu as pltpu

F32 = jnp.float32
BF16 = jnp.bfloat16

D_MODEL = 2048
N_HEADS = 8
N_KV_HEADS = 2
HEAD_DIM = 128
GQA_GROUP = N_HEADS // N_KV_HEADS
ATTN_WIDTH = N_HEADS * HEAD_DIM
KV_WIDTH = N_KV_HEADS * HEAD_DIM
N_FOURIER_GROUPS = 8
FOURIER_GROUP_CH = 128
FOURIER_WIDTH = N_FOURIER_GROUPS * FOURIER_GROUP_CH
QKVF_WIDTH = ATTN_WIDTH + 2 * KV_WIDTH + FOURIER_WIDTH
OFF_K = ATTN_WIDTH
OFF_V = OFF_K + KV_WIDTH
OFF_F = OFF_V + KV_WIDTH
D_FF = 5632
N_META = 16
GRID_W = 64
NORM_EPS = 1e-6
ROPE_THETA = 10000.0
ROPE_PAIRS = HEAD_DIM // 4

DFT_N2 = 80
ROW_ALIGN = 640
CONV_HALO = 16
MASK_VALUE = -1e30
VMEM_LIMIT = 56 * 1024 * 1024


def _cparams(*sem):
    return pltpu.CompilerParams(dimension_semantics=sem, vmem_limit_bytes=VMEM_LIMIT)


def _rms(x):
    return x * lax.rsqrt(jnp.mean(x * x, axis=-1, keepdims=True) + NORM_EPS)


def _inproj_kernel(x_ref, g_ref, w_ref, qn_ref, kn_ref, cos_ref, sin_ref,
                   q_ref, k_ref, vt_ref, f_ref, h_ref):
    h = (_rms(x_ref[...]) * g_ref[...]).astype(BF16)
    h_ref[...] = h
    proj = jnp.dot(h, w_ref[...], preferred_element_type=F32)
    cos = cos_ref[...]
    sin = sin_ref[...]
    lane = lax.broadcasted_iota(jnp.int32, cos.shape, 1)
    first_half = (lane % (HEAD_DIM // 2)) < ROPE_PAIRS

    def norm_rope(t, gain):
        t = _rms(t) * gain
        partner = jnp.where(first_half,
                            pltpu.roll(t, HEAD_DIM - ROPE_PAIRS, 1),
                            pltpu.roll(t, ROPE_PAIRS, 1))
        return t * cos + partner * sin

    scale = math.log2(math.e) / math.sqrt(HEAD_DIM)
    for hh in range(N_HEADS):
        c = hh * HEAD_DIM
        q_ref[:, c:c + HEAD_DIM] = (norm_rope(proj[:, c:c + HEAD_DIM], qn_ref[...]) * scale).astype(BF16)
    for hh in range(N_KV_HEADS):
        c = hh * HEAD_DIM
        k_ref[:, c:c + HEAD_DIM] = norm_rope(proj[:, OFF_K + c:OFF_K + c + HEAD_DIM], kn_ref[...]).astype(BF16)
    vt_ref[...] = proj[:, OFF_V:OFF_F].T.astype(BF16)
    f_ref[...] = proj[:, OFF_F:].astype(BF16)


def _inproj(hs, gain, w_qkvf, qn, kn, cos_t, sin_t, *, tm, tiles_per_batch):
    R = hs.shape[0]
    row = lambda i: (i, 0)
    fixed = lambda i: (0, 0)
    tab = lambda i: (i % tiles_per_batch, 0)
    return pl.pallas_call(
        _inproj_kernel,
        grid=(R // tm,),
        in_specs=[
            pl.BlockSpec((tm, D_MODEL), row),
            pl.BlockSpec((1, D_MODEL), fixed),
            pl.BlockSpec((D_MODEL, QKVF_WIDTH), fixed),
            pl.BlockSpec((1, HEAD_DIM), fixed),
            pl.BlockSpec((1, HEAD_DIM), fixed),
            pl.BlockSpec((tm, HEAD_DIM), tab),
            pl.BlockSpec((tm, HEAD_DIM), tab),
        ],
        out_specs=[
            pl.BlockSpec((tm, ATTN_WIDTH), row),
            pl.BlockSpec((tm, KV_WIDTH), row),
            pl.BlockSpec((KV_WIDTH, tm), lambda i: (0, i)),
            pl.BlockSpec((tm, FOURIER_WIDTH), row),
            pl.BlockSpec((tm, D_MODEL), row),
        ],
        out_shape=[
            jax.ShapeDtypeStruct((R, ATTN_WIDTH), BF16),
            jax.ShapeDtypeStruct((R, KV_WIDTH), BF16),
            jax.ShapeDtypeStruct((KV_WIDTH, R), BF16),
            jax.ShapeDtypeStruct((R, FOURIER_WIDTH), BF16),
            jax.ShapeDtypeStruct((R, D_MODEL), BF16),
        ],
        compiler_params=_cparams("parallel"),
        name="inproj",
    )(hs, gain, w_qkvf, qn, kn, cos_t, sin_t)


def _attn_kernel(q_ref, k_ref, vt_ref, o_ref, s_ref, mt_ref, m_ref, l_ref, acc_ref, *, tk, n_full, tail_valid):
    has_tail = tail_valid > 0
    n_stage = n_full + (1 if has_tail else 0)
    off = 1 if has_tail else 0
    m_ref[...] = jnp.full(m_ref.shape, MASK_VALUE, F32)
    l_ref[...] = jnp.zeros(l_ref.shape, F32)
    acc_ref[...] = jnp.zeros(acc_ref.shape, F32)

    def tile_start(t):
        if isinstance(t, int):
            return n_full * tk if (has_tail and t == 0) else (t - off) * tk
        return pl.multiple_of((t - off) * tk, tk)

    def scores(t, slot, masked):
        kj = k_ref[0, pl.ds(tile_start(t), tk), :]
        for g in range(GQA_GROUP):
            c = g * HEAD_DIM
            st = lax.dot_general(kj, q_ref[0, :, c:c + HEAD_DIM], (((1,), (1,)), ((), ())),
                                 preferred_element_type=F32)
            if masked:
                key = lax.broadcasted_iota(jnp.int32, st.shape, 0)
                st = jnp.where(key < tail_valid, st, MASK_VALUE)
            s_ref[slot, g] = st
            mt_ref[slot, g] = jnp.max(st, axis=0, keepdims=True)

    def softmax_pv(t, slot):
        vtj = vt_ref[:, pl.ds(tile_start(t), tk)]
        for g in range(GQA_GROUP):
            st = s_ref[slot, g]
            m_prev = m_ref[g]
            m_new = jnp.maximum(m_prev, mt_ref[slot, g])
            alpha = jnp.exp2(m_prev - m_new)
            p = jnp.exp2(st - m_new)
            l_ref[g] = alpha * l_ref[g] + jnp.sum(p, axis=0, keepdims=True)
            acc_ref[g] = alpha * acc_ref[g] + jnp.dot(vtj, p.astype(BF16), preferred_element_type=F32)
            m_ref[g] = m_new

    def stage(t, slot):
        if not isinstance(t, int) or t + 1 < n_stage:
            scores(t + 1, 1 - slot, False)
        softmax_pv(t, slot)

    scores(0, 0, has_tail)
    stage(0, 0)
    n_pairs = max(0, (n_stage - 2) // 2)
    if n_pairs > 0:
        def body(p, carry):
            stage(1 + 2 * p, 1)
            stage(2 + 2 * p, 0)
            return carry
        lax.fori_loop(0, n_pairs, body, 0)
    for t in range(1 + 2 * n_pairs, n_stage):
        stage(t, t % 2)
    for g in range(GQA_GROUP):
        c = g * HEAD_DIM
        o_ref[0, :, c:c + HEAD_DIM] = (acc_ref[g] / l_ref[g]).T.astype(BF16)


def _attention(q, k, vt, *, L, tq, tk):
    B, LP, _ = q.shape
    gw = GQA_GROUP * HEAD_DIM
    n_full = L // tk
    tail_valid = L - n_full * tk
    kern = functools.partial(_attn_kernel, tk=tk, n_full=n_full, tail_valid=tail_valid)
    return pl.pallas_call(
        kern,
        grid=(B, N_KV_HEADS, LP // tq),
        in_specs=[
            pl.BlockSpec((1, tq, gw), lambda b, h, i: (b, i, h)),
            pl.BlockSpec((1, LP, HEAD_DIM), lambda b, h, i: (b, 0, h)),
            pl.BlockSpec((HEAD_DIM, LP), lambda b, h, i: (h, b)),
        ],
        out_specs=pl.BlockSpec((1, tq, gw), lambda b, h, i: (b, i, h)),
        out_shape=jax.ShapeDtypeStruct((B, LP, ATTN_WIDTH), BF16),
        scratch_shapes=[
            pltpu.VMEM((2, GQA_GROUP, tk, tq), F32),
            pltpu.VMEM((2, GQA_GROUP, 1, tq), F32),
            pltpu.VMEM((GQA_GROUP, 1, tq), F32),
            pltpu.VMEM((GQA_GROUP, 1, tq), F32),
            pltpu.VMEM((GQA_GROUP, HEAD_DIM, tq), F32),
        ],
        compiler_params=_cparams("parallel", "parallel", "parallel"),
        name="attention",
    )(q, k, vt)


def _fourier_kernel(f_ref, cs_ref, f1r_ref, f1i_ref, g_ref, o_ref, wr_ref, wi_ref, os_ref,
                    *, L, LP, n1, n1p):
    chunk = ROW_ALIGN

    def channel_dft(c, carry):
        r0 = pl.multiple_of(c * chunk, chunk)
        z = jnp.dot(f_ref[0, pl.ds(r0, chunk), :], cs_ref[...], preferred_element_type=F32)
        wr_ref[pl.ds(r0, chunk), :] = z[:, :FOURIER_GROUP_CH]
        wi_ref[pl.ds(r0, chunk), :] = z[:, FOURIER_GROUP_CH:]
        return carry

    lax.fori_loop(0, LP // chunk, channel_dft, 0, unroll=13 if (LP // chunk) % 13 == 0 else 2)

    def row_dft(n2, carry):
        rows = pl.ds(n2, n1p, stride=DFT_N2)
        zr = wr_ref[rows, :].astype(BF16)
        zi = wi_ref[rows, :].astype(BF16)
        a = (jnp.dot(f1r_ref[...], zr, preferred_element_type=F32)
             + jnp.dot(f1i_ref[...], zi, preferred_element_type=F32))
        wr_ref[rows, :] = a[:n1p]
        wi_ref[rows, :] = a[n1p:]
        return carry

    lax.fori_loop(0, DFT_N2, row_dft, 0, unroll=16)

    os_ref[L:, :] = jnp.zeros((LP - L, FOURIER_GROUP_CH), F32)

    def col_dft(k1, carry):
        r0 = pl.multiple_of(k1 * DFT_N2, DFT_N2)
        a = jnp.concatenate([wr_ref[pl.ds(r0, DFT_N2), :].astype(BF16),
                             wi_ref[pl.ds(r0, DFT_N2), :].astype(BF16)], axis=0)
        os_ref[pl.ds(k1, DFT_N2, stride=n1), :] = jnp.dot(g_ref[k1], a, preferred_element_type=F32)
        return carry

    lax.fori_loop(0, n1, col_dft, 0, unroll=(41 if n1 % 41 == 0 else 5) if n1 % 5 == 0 else 1)

    def emit(c, carry):
        r0 = pl.multiple_of(c * chunk, chunk)
        o_ref[0, pl.ds(r0, chunk), :] = os_ref[pl.ds(r0, chunk), :].astype(BF16)
        return carry

    lax.fori_loop(0, LP // chunk, emit, 0)


def _fourier(f, consts, *, L):
    B, LP, _ = f.shape
    n1 = L // DFT_N2
    n1p = LP // DFT_N2
    cs, f1r, f1i, g = consts
    slab = pl.BlockSpec((1, LP, FOURIER_GROUP_CH), lambda b, c: (b, 0, c))
    fixed2 = lambda b, c: (0, 0)
    return pl.pallas_call(
        functools.partial(_fourier_kernel, L=L, LP=LP, n1=n1, n1p=n1p),
        grid=(B, N_FOURIER_GROUPS),
        in_specs=[
            slab,
            pl.BlockSpec((FOURIER_GROUP_CH, 2 * FOURIER_GROUP_CH), fixed2),
            pl.BlockSpec((2 * n1p, n1p), fixed2),
            pl.BlockSpec((2 * n1p, n1p), fixed2),
            pl.BlockSpec((n1, DFT_N2, 2 * DFT_N2), lambda b, c: (0, 0, 0), pipeline_mode=pl.Buffered(1)),
        ],
        out_specs=slab,
        out_shape=jax.ShapeDtypeStruct((B, LP, FOURIER_WIDTH), BF16),
        scratch_shapes=[pltpu.VMEM((LP, FOURIER_GROUP_CH), F32)] * 3,
        compiler_params=_cparams("parallel", "parallel"),
        name="fourier",
    )(f, cs, f1r, f1i, g)


def _dft_constants(L, LP):
    n1 = L // DFT_N2
    n1p = LP // DFT_N2
    k1 = np.arange(n1)
    ang1 = 2.0 * np.pi * np.outer(k1, k1) / n1
    c1, s1 = np.cos(ang1), np.sin(ang1)
    f1r = np.zeros((2 * n1p, n1p))
    f1i = np.zeros((2 * n1p, n1p))
    f1r[:n1, :n1] = c1
    f1r[n1p:n1p + n1, :n1] = -s1
    f1i[:n1, :n1] = s1
    f1i[n1p:n1p + n1, :n1] = c1
    n2 = np.arange(DFT_N2)
    k = k1[:, None, None] + n1 * n2[None, :, None]
    ang = 2.0 * np.pi * ((k * n2[None, None, :]) % L) / L
    g = np.concatenate([np.cos(ang), np.sin(ang)], axis=2) / math.sqrt(L)
    c = np.arange(FOURIER_GROUP_CH)
    angc = 2.0 * np.pi * np.outer(c, c) / FOURIER_GROUP_CH
    cs = np.concatenate([np.cos(angc), -np.sin(angc)], axis=1) / math.sqrt(FOURIER_GROUP_CH)
    return (jnp.asarray(cs, BF16), jnp.asarray(f1r, BF16), jnp.asarray(f1i, BF16), jnp.asarray(g, BF16))


def _mix_kernel(x_ref, h_ref, attn_ref, four_ref, wga_ref, wgf_ref, bga_ref, bgf_ref,
                wa_ref, wf_ref, wo_ref, o_ref, *, tm, L, LP):
    j = pl.program_id(1)

    @pl.when(j == 0)
    def _():
        o_ref[...] = x_ref[...]

    h = h_ref[...]
    g_attn = jax.nn.sigmoid(jnp.dot(h, wga_ref[...], preferred_element_type=F32) + bga_ref[...])
    g_four = jax.nn.sigmoid(jnp.dot(h, wgf_ref[...], preferred_element_type=F32) + bgf_ref[...])
    a_br = jnp.dot(attn_ref[...], wa_ref[...], preferred_element_type=F32)
    s_br = jnp.dot(four_ref[...], wf_ref[...], preferred_element_type=F32)
    merged = (g_attn * a_br + g_four * s_br).astype(BF16)
    o_ref[...] += jnp.dot(merged, wo_ref[...], preferred_element_type=F32)

    @pl.when(j == pl.num_programs(1) - 1)
    def _():
        r = (pl.program_id(0) % (LP // tm)) * tm + lax.broadcasted_iota(jnp.int32, (tm, 1), 0)
        o_ref[...] = jnp.where(r < L, o_ref[...], 0.0)


def _mix(hs, h, attn, four, w_gates, b_gates, w_a, w_f, w_o, *, tm, tn, L, LP):
    R = hs.shape[0]
    nj = D_MODEL // tn
    row = lambda i, j: (i, 0)
    return pl.pallas_call(
        functools.partial(_mix_kernel, tm=tm, L=L, LP=LP),
        grid=(R // tm, nj),
        in_specs=[
            pl.BlockSpec((tm, D_MODEL), row),
            pl.BlockSpec((tm, D_MODEL), row),
            pl.BlockSpec((tm, ATTN_WIDTH), row),
            pl.BlockSpec((tm, FOURIER_WIDTH), row),
            pl.BlockSpec((D_MODEL, tn), lambda i, j: (0, j)),
            pl.BlockSpec((D_MODEL, tn), lambda i, j: (0, j + nj)),
            pl.BlockSpec((1, tn), lambda i, j: (0, j)),
            pl.BlockSpec((1, tn), lambda i, j: (0, j + nj)),
            pl.BlockSpec((ATTN_WIDTH, tn), lambda i, j: (0, j)),
            pl.BlockSpec((FOURIER_WIDTH, tn), lambda i, j: (0, j)),
            pl.BlockSpec((tn, D_MODEL), lambda i, j: (j, 0)),
        ],
        out_specs=pl.BlockSpec((tm, D_MODEL), row),
        out_shape=jax.ShapeDtypeStruct((R, D_MODEL), F32),
        compiler_params=_cparams("parallel", "arbitrary"),
        name="mix_out",
    )(hs, h, attn, four, w_gates, w_gates, b_gates, b_gates, w_a, w_f, w_o)


def _ffn_kernel(x_ref, xp_ref, xn_ref, g_ref, wg_ref, wv_ref, wc_ref, bc_ref, wd_ref,
                o_ref, h_ref, *, tm):
    i = pl.program_id(0)
    j = pl.program_id(1)
    H = CONV_HALO

    @pl.when(j == 0)
    def _():
        x = x_ref[...]
        g = g_ref[...]
        h_ref[H:H + tm, :] = (_rms(x) * g).astype(BF16)
        hp = jnp.where(i > 0, _rms(xp_ref[...]) * g, 0.0)
        hn = jnp.where(i < pl.num_programs(0) - 1, _rms(xn_ref[...]) * g, 0.0)
        h_ref[0:H, :] = hp.astype(BF16)
        h_ref[H + tm:, :] = hn.astype(BF16)
        o_ref[...] = x

    h = h_ref[...]
    up_g = jnp.dot(h, wg_ref[...], preferred_element_type=F32)
    n = tm + 2 * H
    wc = wc_ref[...]
    prev = pltpu.roll(up_g, 1, 0)[H:H + tm]
    nxt = pltpu.roll(up_g, n - 1, 0)[H:H + tm]
    u = prev * wc[0:1] + up_g[H:H + tm] * wc[1:2] + nxt * wc[2:3] + bc_ref[...]
    val = jnp.dot(h[H:H + tm], wv_ref[...], preferred_element_type=F32)
    act = (u * jax.nn.sigmoid(u) * val).astype(BF16)
    o_ref[...] += jnp.dot(act, wd_ref[...], preferred_element_type=F32)


def _ffn(hs, gain, w_up, w_conv, b_conv, w_down, *, tm, tf):
    R = hs.shape[0]
    nj = D_FF // tf
    hb = tm // CONV_HALO
    last_hb = R // CONV_HALO - 1
    row = lambda i, j: (i, 0)
    return pl.pallas_call(
        functools.partial(_ffn_kernel, tm=tm),
        grid=(R // tm, nj),
        in_specs=[
            pl.BlockSpec((tm, D_MODEL), row),
            pl.BlockSpec((CONV_HALO, D_MODEL), lambda i, j: (jnp.maximum(i * hb - 1, 0), 0)),
            pl.BlockSpec((CONV_HALO, D_MODEL), lambda i, j: (jnp.minimum((i + 1) * hb, last_hb), 0)),
            pl.BlockSpec((1, D_MODEL), lambda i, j: (0, 0)),
            pl.BlockSpec((D_MODEL, tf), lambda i, j: (0, j)),
            pl.BlockSpec((D_MODEL, tf), lambda i, j: (0, j + nj)),
            pl.BlockSpec((3, tf), lambda i, j: (0, j)),
            pl.BlockSpec((1, tf), lambda i, j: (0, j)),
            pl.BlockSpec((tf, D_MODEL), lambda i, j: (j, 0)),
        ],
        out_specs=pl.BlockSpec((tm, D_MODEL), row),
        out_shape=jax.ShapeDtypeStruct((R, D_MODEL), F32),
        scratch_shapes=[pltpu.VMEM((tm + 2 * CONV_HALO, D_MODEL), BF16)],
        compiler_params=_cparams("parallel", "arbitrary"),
        name="ffn",
    )(hs, hs, hs, gain, w_up, w_up, w_conv, b_conv, w_down)


def _rope_tables(n_tok, LP):
    rows = n_tok // GRID_W
    pad = LP - N_META - n_tok
    pos_r = np.concatenate([np.zeros(N_META), np.repeat(np.arange(rows), GRID_W), np.zeros(pad)])
    pos_c = np.concatenate([np.zeros(N_META), np.tile(np.arange(GRID_W), rows), np.zeros(pad)])
    inv_freq = 1.0 / (ROPE_THETA ** (np.arange(ROPE_PAIRS) / ROPE_PAIRS))
    ang_r = pos_r[:, None] * inv_freq[None, :]
    ang_c = pos_c[:, None] * inv_freq[None, :]
    cos_t = np.concatenate([np.cos(ang_r), np.cos(ang_r), np.cos(ang_c), np.cos(ang_c)], axis=-1)
    sin_t = np.concatenate([-np.sin(ang_r), np.sin(ang_r), -np.sin(ang_c), np.sin(ang_c)], axis=-1)
    return jnp.asarray(cos_t, F32), jnp.asarray(sin_t, F32)


def _tiles(LP):
    big = 1280 if LP % 1280 == 0 else ROW_ALIGN
    return dict(tm_in=ROW_ALIGN, tq=256 if LP % 256 == 0 else 128, tk=big, tm_mix=ROW_ALIGN, tn_mix=512,
                tm_ffn=ROW_ALIGN, tf=512)


def kernel(x, meta_tokens, norm_mix, norm_ffn, w_in, b_gate, q_norm, k_norm, w_attn_br, w_four, w_out,
           w_up, w_conv, b_conv, w_down):
    B, n_tok, D = x.shape
    depth = w_in.shape[0]
    L = n_tok + N_META
    assert D == D_MODEL and n_tok % GRID_W == 0 and L % DFT_N2 == 0
    LP = -(-L // ROW_ALIGN) * ROW_ALIGN
    R = B * LP
    t = _tiles(LP)

    cos_t, sin_t = _rope_tables(n_tok, LP)
    dft = _dft_constants(L, LP)

    meta = jnp.broadcast_to(meta_tokens.astype(x.dtype)[None], (B, N_META, D))
    hs = jnp.concatenate([meta, x, jnp.zeros((B, LP - L, D), x.dtype)], axis=1).reshape(R, D)

    for i in range(depth):
        w_qkvf = w_in[i, :, :QKVF_WIDTH].astype(BF16)
        w_gates = w_in[i, :, QKVF_WIDTH:].astype(BF16)
        q, k, vt, f, h = _inproj(hs, norm_mix[i][None], w_qkvf, q_norm[i][None], k_norm[i][None],
                                 cos_t, sin_t, tm=t["tm_in"], tiles_per_batch=LP // t["tm_in"])
        attn = _attention(q.reshape(B, LP, ATTN_WIDTH), k.reshape(B, LP, KV_WIDTH), vt,
                          L=L, tq=t["tq"], tk=t["tk"]).reshape(R, ATTN_WIDTH)
        four = _fourier(f.reshape(B, LP, FOURIER_WIDTH), dft, L=L).reshape(R, FOURIER_WIDTH)
        hs = _mix(hs, h, attn, four, w_gates, b_gate[i][None],
                  w_attn_br[i].astype(BF16), w_four[i].astype(BF16), w_out[i].astype(BF16),
                  tm=t["tm_mix"], tn=t["tn_mix"], L=L, LP=LP)
        hs = _ffn(hs, norm_ffn[i][None], w_up[i].astype(BF16), w_conv[i], b_conv[i][None],
                  w_down[i].astype(BF16), tm=t["tm_ffn"], tf=t["tf"])

    return hs.reshape(B, LP, D)[:, N_META:L]
```

```python
import functools
import math

import jax
import jax.numpy as jnp
import numpy as np
from jax import lax
from jax.experimental import pallas as pl
from jax.experimental.pallas import tpu as pltpu

F32 = jnp.float32
BF16 = jnp.bfloat16

D_MODEL = 2048
N_HEADS = 8
N_KV_HEADS = 2
HEAD_DIM = 128
GQA_GROUP = N_HEADS // N_KV_HEADS
ATTN_WIDTH = N_HEADS * HEAD_DIM
KV_WIDTH = N_KV_HEADS * HEAD_DIM
N_FOURIER_GROUPS = 8
FOURIER_GROUP_CH = 128
FOURIER_WIDTH = N_FOURIER_GROUPS * FOURIER_GROUP_CH
QKVF_WIDTH = ATTN_WIDTH + 2 * KV_WIDTH + FOURIER_WIDTH
OFF_K = ATTN_WIDTH
OFF_V = OFF_K + KV_WIDTH
OFF_F = OFF_V + KV_WIDTH
D_FF = 5632
N_META = 16
GRID_W = 64
NORM_EPS = 1e-6
ROPE_THETA = 10000.0
ROPE_PAIRS = HEAD_DIM // 4

DFT_N2 = 80
ROW_TILES_PER_BATCH = 25
PACK_ROWS = 16
LANES = 128
CONV_HALO = PACK_ROWS
MASK_VALUE = -1e30
VMEM_LIMIT = 56 * 1024 * 1024


def _cparams(*sem):
    return pltpu.CompilerParams(dimension_semantics=sem, vmem_limit_bytes=VMEM_LIMIT)


def _rms(x):
    return x * lax.rsqrt(jnp.mean(x * x, axis=-1, keepdims=True) + NORM_EPS)


def _round_up(n, m):
    return -(-n // m) * m


def _inproj_kernel(x_ref, g_ref, w_ref, qn_ref, kn_ref, cos_ref, sin_ref,
                   q_ref, k_ref, v_ref, f_ref, h_ref):
    h = (_rms(x_ref[...]) * g_ref[...]).astype(BF16)
    h_ref[...] = h
    proj = jnp.dot(h, w_ref[...], preferred_element_type=F32)
    cos = cos_ref[...]
    sin = sin_ref[...]
    lane = lax.broadcasted_iota(jnp.int32, cos.shape, 1)
    first_half = (lane % (HEAD_DIM // 2)) < ROPE_PAIRS

    def norm_rope(t, gain):
        t = _rms(t) * gain
        partner = jnp.where(first_half,
                            pltpu.roll(t, HEAD_DIM - ROPE_PAIRS, 1),
                            pltpu.roll(t, ROPE_PAIRS, 1))
        return t * cos + partner * sin

    scale = math.log2(math.e) / math.sqrt(HEAD_DIM)
    for hh in range(N_HEADS):
        c = hh * HEAD_DIM
        q_ref[:, c:c + HEAD_DIM] = (norm_rope(proj[:, c:c + HEAD_DIM], qn_ref[...]) * scale).astype(BF16)
    for hh in range(N_KV_HEADS):
        c = hh * HEAD_DIM
        k_ref[:, c:c + HEAD_DIM] = norm_rope(proj[:, OFF_K + c:OFF_K + c + HEAD_DIM], kn_ref[...]).astype(BF16)
    v_ref[...] = proj[:, OFF_V:OFF_F].astype(BF16)
    f_ref[...] = proj[:, OFF_F:].astype(BF16)


def _inproj(hs, gain, w_qkvf, qn, kn, cos_t, sin_t, *, tm, tiles_per_batch):
    R = hs.shape[0]
    row = lambda i: (i, 0)
    fixed = lambda i: (0, 0)
    tab = lambda i: (i % tiles_per_batch, 0)
    widths = (ATTN_WIDTH, KV_WIDTH, KV_WIDTH, FOURIER_WIDTH, D_MODEL)
    return pl.pallas_call(
        _inproj_kernel,
        grid=(R // tm,),
        in_specs=[
            pl.BlockSpec((tm, D_MODEL), row),
            pl.BlockSpec((1, D_MODEL), fixed),
            pl.BlockSpec((D_MODEL, QKVF_WIDTH), fixed),
            pl.BlockSpec((1, HEAD_DIM), fixed),
            pl.BlockSpec((1, HEAD_DIM), fixed),
            pl.BlockSpec((tm, HEAD_DIM), tab),
            pl.BlockSpec((tm, HEAD_DIM), tab),
        ],
        out_specs=[pl.BlockSpec((tm, w), row) for w in widths],
        out_shape=[jax.ShapeDtypeStruct((R, w), BF16) for w in widths],
        compiler_params=_cparams("parallel"),
        name="inproj",
    )(hs, gain, w_qkvf, qn, kn, cos_t, sin_t)


def _attn_kernel(q_ref, k_ref, v_ref, o_ref, vt_ref, s_ref, mt_ref, m_ref, l_ref, acc_ref,
                 *, L, tk, n_full, tail, tail_pad):
    has_tail = tail > 0
    n_stage = n_full + (1 if has_tail else 0)
    off = 1 if has_tail else 0
    tail_start = n_full * tk

    def padded_tail(ref):
        rows = ref[0, tail_start:L, :]
        if tail_pad == tail:
            return rows
        return jnp.concatenate([rows, jnp.zeros((tail_pad - tail, HEAD_DIM), rows.dtype)], axis=0)

    @pl.when(pl.program_id(2) == 0)
    def _():
        for j in range(n_full):
            vt_ref[:, j * tk:(j + 1) * tk] = v_ref[0, j * tk:(j + 1) * tk, :].astype(F32).T.astype(BF16)
        if has_tail:
            vt_ref[:, tail_start:tail_start + tail_pad] = padded_tail(v_ref).astype(F32).T.astype(BF16)

    m_ref[...] = jnp.full(m_ref.shape, MASK_VALUE, F32)
    l_ref[...] = jnp.zeros(l_ref.shape, F32)
    acc_ref[...] = jnp.zeros(acc_ref.shape, F32)

    def is_tail(t):
        return has_tail and isinstance(t, int) and t == 0

    def tile_start(t):
        if isinstance(t, int):
            return tail_start if is_tail(t) else (t - off) * tk
        return pl.multiple_of((t - off) * tk, tk)

    def scores(t, slot):
        rows = tail_pad if is_tail(t) else tk
        kj = padded_tail(k_ref) if is_tail(t) else k_ref[0, pl.ds(tile_start(t), tk), :]
        for g in range(GQA_GROUP):
            c = g * HEAD_DIM
            st = lax.dot_general(kj, q_ref[0, :, c:c + HEAD_DIM], (((1,), (1,)), ((), ())),
                                 preferred_element_type=F32)
            if is_tail(t) and tail_pad != tail:
                key = lax.broadcasted_iota(jnp.int32, st.shape, 0)
                st = jnp.where(key < tail, st, MASK_VALUE)
            s_ref[slot, g, 0:rows, :] = st
            mt_ref[slot, g] = jnp.max(st, axis=0, keepdims=True)

    def softmax_pv(t, slot):
        rows = tail_pad if is_tail(t) else tk
        vtj = vt_ref[:, pl.ds(tile_start(t), rows)]
        for g in range(GQA_GROUP):
            st = s_ref[slot, g, 0:rows, :]
            m_prev = m_ref[g]
            m_new = jnp.maximum(m_prev, mt_ref[slot, g])
            alpha = jnp.exp2(m_prev - m_new)
            p = jnp.exp2(st - m_new)
            l_ref[g] = alpha * l_ref[g] + jnp.sum(p, axis=0, keepdims=True)
            acc_ref[g] = alpha * acc_ref[g] + jnp.dot(vtj, p.astype(BF16), preferred_element_type=F32)
            m_ref[g] = m_new

    def stage(t, slot):
        if not isinstance(t, int) or t + 1 < n_stage:
            scores(t + 1, 1 - slot)
        softmax_pv(t, slot)

    scores(0, 0)
    stage(0, 0)
    n_pairs = max(0, (n_stage - 2) // 2)
    if n_pairs > 0:
        def body(p, carry):
            stage(1 + 2 * p, 1)
            stage(2 + 2 * p, 0)
            return carry
        lax.fori_loop(0, n_pairs, body, 0)
    for t in range(1 + 2 * n_pairs, n_stage):
        stage(t, t % 2)
    for g in range(GQA_GROUP):
        c = g * HEAD_DIM
        o_ref[0, :, c:c + HEAD_DIM] = (acc_ref[g] / l_ref[g]).T.astype(BF16)


def _attention(q, k, v, *, tq, tk):
    B, L, _ = q.shape
    gw = GQA_GROUP * HEAD_DIM
    n_full = L // tk
    tail = L - n_full * tk
    tail_pad = _round_up(tail, LANES)
    kern = functools.partial(_attn_kernel, L=L, tk=tk, n_full=n_full, tail=tail, tail_pad=tail_pad)
    kv_spec = pl.BlockSpec((1, L, HEAD_DIM), lambda b, h, i: (b, 0, h))
    return pl.pallas_call(
        kern,
        grid=(B, N_KV_HEADS, pl.cdiv(L, tq)),
        in_specs=[pl.BlockSpec((1, tq, gw), lambda b, h, i: (b, i, h)), kv_spec, kv_spec],
        out_specs=pl.BlockSpec((1, tq, gw), lambda b, h, i: (b, i, h)),
        out_shape=jax.ShapeDtypeStruct((B, L, ATTN_WIDTH), BF16),
        scratch_shapes=[
            pltpu.VMEM((HEAD_DIM, n_full * tk + tail_pad), BF16),
            pltpu.VMEM((2, GQA_GROUP, max(tk, tail_pad), tq), F32),
            pltpu.VMEM((2, GQA_GROUP, 1, tq), F32),
            pltpu.VMEM((GQA_GROUP, 1, tq), F32),
            pltpu.VMEM((GQA_GROUP, 1, tq), F32),
            pltpu.VMEM((GQA_GROUP, HEAD_DIM, tq), F32),
        ],
        compiler_params=_cparams("parallel", "parallel", "arbitrary"),
        name="attention",
    )(q, k, v)


def _fourier_kernel(f_ref, cs_ref, f1r_ref, f1i_ref, g_ref, o_ref, wr_ref, wi_ref, os_ref,
                    *, L, n1, n1p, chunk):
    n_chunks = L // chunk

    def channel_dft(c, carry):
        r0 = pl.multiple_of(c * chunk, PACK_ROWS)
        z = jnp.dot(f_ref[0, pl.ds(r0, chunk), :], cs_ref[...], preferred_element_type=F32)
        wr_ref[pl.ds(r0, chunk), :] = z[:, :FOURIER_GROUP_CH]
        wi_ref[pl.ds(r0, chunk), :] = z[:, FOURIER_GROUP_CH:]
        return carry

    lax.fori_loop(0, n_chunks, channel_dft, 0, unroll=5 if n_chunks % 5 == 0 else 1)
    if n1p > n1:
        zeros = jnp.zeros(((n1p - n1) * DFT_N2, FOURIER_GROUP_CH), F32)
        wr_ref[L:, :] = zeros
        wi_ref[L:, :] = zeros

    def row_dft(n2, carry):
        rows = pl.ds(n2, n1p, stride=DFT_N2)
        zr = wr_ref[rows, :].astype(BF16)
        zi = wi_ref[rows, :].astype(BF16)
        a = (jnp.dot(f1r_ref[...], zr, preferred_element_type=F32)
             + jnp.dot(f1i_ref[...], zi, preferred_element_type=F32))
        wr_ref[rows, :] = a[:n1p]
        wi_ref[rows, :] = a[n1p:]
        return carry

    lax.fori_loop(0, DFT_N2, row_dft, 0, unroll=16)

    def col_dft(k1, carry):
        r0 = pl.multiple_of(k1 * DFT_N2, DFT_N2)
        a = jnp.concatenate([wr_ref[pl.ds(r0, DFT_N2), :].astype(BF16),
                             wi_ref[pl.ds(r0, DFT_N2), :].astype(BF16)], axis=0)
        os_ref[pl.ds(k1, DFT_N2, stride=n1), :] = jnp.dot(g_ref[k1], a, preferred_element_type=F32)
        return carry

    lax.fori_loop(0, n1, col_dft, 0, unroll=(41 if n1 % 41 == 0 else 5) if n1 % 5 == 0 else 1)

    def emit(c, carry):
        r0 = pl.multiple_of(c * chunk, PACK_ROWS)
        o_ref[0, pl.ds(r0, chunk), :] = os_ref[pl.ds(r0, chunk), :].astype(BF16)
        return carry

    lax.fori_loop(0, n_chunks, emit, 0)


def _fourier(f, consts, *, chunk):
    B, L, _ = f.shape
    n1 = L // DFT_N2
    n1p = _round_up(n1, 8)
    cs, f1r, f1i, g = consts
    slab = pl.BlockSpec((1, L, FOURIER_GROUP_CH), lambda b, c: (b, 0, c))
    fixed2 = lambda b, c: (0, 0)
    return pl.pallas_call(
        functools.partial(_fourier_kernel, L=L, n1=n1, n1p=n1p, chunk=chunk),
        grid=(B, N_FOURIER_GROUPS),
        in_specs=[
            slab,
            pl.BlockSpec((FOURIER_GROUP_CH, 2 * FOURIER_GROUP_CH), fixed2),
            pl.BlockSpec((2 * n1p, n1p), fixed2),
            pl.BlockSpec((2 * n1p, n1p), fixed2),
            pl.BlockSpec((n1, DFT_N2, 2 * DFT_N2), lambda b, c: (0, 0, 0), pipeline_mode=pl.Buffered(1)),
        ],
        out_specs=slab,
        out_shape=jax.ShapeDtypeStruct((B, L, FOURIER_WIDTH), BF16),
        scratch_shapes=[pltpu.VMEM((n1p * DFT_N2, FOURIER_GROUP_CH), F32)] * 2
        + [pltpu.VMEM((L, FOURIER_GROUP_CH), F32)],
        compiler_params=_cparams("parallel", "parallel"),
        name="fourier",
    )(f, cs, f1r, f1i, g)


def _dft_constants(L):
    n1 = L // DFT_N2
    n1p = _round_up(n1, 8)
    k1 = np.arange(n1)
    ang1 = 2.0 * np.pi * np.outer(k1, k1) / n1
    c1, s1 = np.cos(ang1), np.sin(ang1)
    f1r = np.zeros((2 * n1p, n1p))
    f1i = np.zeros((2 * n1p, n1p))
    f1r[:n1, :n1] = c1
    f1r[n1p:n1p + n1, :n1] = -s1
    f1i[:n1, :n1] = s1
    f1i[n1p:n1p + n1, :n1] = c1
    n2 = np.arange(DFT_N2)
    k = k1[:, None, None] + n1 * n2[None, :, None]
    ang = 2.0 * np.pi * ((k * n2[None, None, :]) % L) / L
    g = np.concatenate([np.cos(ang), np.sin(ang)], axis=2) / math.sqrt(L)
    c = np.arange(FOURIER_GROUP_CH)
    angc = 2.0 * np.pi * np.outer(c, c) / FOURIER_GROUP_CH
    cs = np.concatenate([np.cos(angc), -np.sin(angc)], axis=1) / math.sqrt(FOURIER_GROUP_CH)
    return (jnp.asarray(cs, BF16), jnp.asarray(f1r, BF16), jnp.asarray(f1i, BF16), jnp.asarray(g, BF16))


def _mix_kernel(x_ref, h_ref, attn_ref, four_ref, wga_ref, wgf_ref, bga_ref, bgf_ref,
                wa_ref, wf_ref, wo_ref, o_ref):
    @pl.when(pl.program_id(1) == 0)
    def _():
        o_ref[...] = x_ref[...]

    h = h_ref[...]
    g_attn = jax.nn.sigmoid(jnp.dot(h, wga_ref[...], preferred_element_type=F32) + bga_ref[...])
    g_four = jax.nn.sigmoid(jnp.dot(h, wgf_ref[...], preferred_element_type=F32) + bgf_ref[...])
    a_br = jnp.dot(attn_ref[...], wa_ref[...], preferred_element_type=F32)
    s_br = jnp.dot(four_ref[...], wf_ref[...], preferred_element_type=F32)
    merged = (g_attn * a_br + g_four * s_br).astype(BF16)
    o_ref[...] += jnp.dot(merged, wo_ref[...], preferred_element_type=F32)


def _mix(hs, h, attn, four, w_gates, b_gates, w_a, w_f, w_o, *, tm, tn):
    R = hs.shape[0]
    nj = D_MODEL // tn
    row = lambda i, j: (i, 0)
    return pl.pallas_call(
        _mix_kernel,
        grid=(R // tm, nj),
        in_specs=[
            pl.BlockSpec((tm, D_MODEL), row),
            pl.BlockSpec((tm, D_MODEL), row),
            pl.BlockSpec((tm, ATTN_WIDTH), row),
            pl.BlockSpec((tm, FOURIER_WIDTH), row),
            pl.BlockSpec((D_MODEL, tn), lambda i, j: (0, j)),
            pl.BlockSpec((D_MODEL, tn), lambda i, j: (0, j + nj)),
            pl.BlockSpec((1, tn), lambda i, j: (0, j)),
            pl.BlockSpec((1, tn), lambda i, j: (0, j + nj)),
            pl.BlockSpec((ATTN_WIDTH, tn), lambda i, j: (0, j)),
            pl.BlockSpec((FOURIER_WIDTH, tn), lambda i, j: (0, j)),
            pl.BlockSpec((tn, D_MODEL), lambda i, j: (j, 0)),
        ],
        out_specs=pl.BlockSpec((tm, D_MODEL), row),
        out_shape=jax.ShapeDtypeStruct((R, D_MODEL), F32),
        compiler_params=_cparams("parallel", "arbitrary"),
        name="mix_out",
    )(hs, h, attn, four, w_gates, w_gates, b_gates, b_gates, w_a, w_f, w_o)


def _ffn_kernel(x_ref, xp_ref, xn_ref, g_ref, wg_ref, wv_ref, wc_ref, bc_ref, wd_ref,
                o_ref, h_ref, *, tm, tiles_per_batch):
    j = pl.program_id(1)
    H = CONV_HALO

    @pl.when(j == 0)
    def _():
        x = x_ref[...]
        g = g_ref[...]
        h_ref[H:H + tm, :] = (_rms(x) * g).astype(BF16)
        tile = pl.program_id(0) % tiles_per_batch
        hp = jnp.where(tile > 0, _rms(xp_ref[...]) * g, 0.0)
        hn = jnp.where(tile < tiles_per_batch - 1, _rms(xn_ref[...]) * g, 0.0)
        h_ref[0:H, :] = hp.astype(BF16)
        h_ref[H + tm:, :] = hn.astype(BF16)
        o_ref[...] = x

    h = h_ref[...]
    up_g = jnp.dot(h, wg_ref[...], preferred_element_type=F32)
    n = tm + 2 * H
    wc = wc_ref[...]
    prev = pltpu.roll(up_g, 1, 0)[H:H + tm]
    nxt = pltpu.roll(up_g, n - 1, 0)[H:H + tm]
    u = prev * wc[0:1] + up_g[H:H + tm] * wc[1:2] + nxt * wc[2:3] + bc_ref[...]
    val = jnp.dot(h[H:H + tm], wv_ref[...], preferred_element_type=F32)
    act = (u * jax.nn.sigmoid(u) * val).astype(BF16)
    o_ref[...] += jnp.dot(act, wd_ref[...], preferred_element_type=F32)


def _ffn(hs, gain, w_up, w_conv, b_conv, w_down, *, tm, tf, tiles_per_batch):
    R = hs.shape[0]
    nj = D_FF // tf
    hb = tm // CONV_HALO
    last_hb = R // CONV_HALO - 1
    row = lambda i, j: (i, 0)
    return pl.pallas_call(
        functools.partial(_ffn_kernel, tm=tm, tiles_per_batch=tiles_per_batch),
        grid=(R // tm, nj),
        in_specs=[
            pl.BlockSpec((tm, D_MODEL), row),
            pl.BlockSpec((CONV_HALO, D_MODEL), lambda i, j: (jnp.maximum(i * hb - 1, 0), 0)),
            pl.BlockSpec((CONV_HALO, D_MODEL), lambda i, j: (jnp.minimum((i + 1) * hb, last_hb), 0)),
            pl.BlockSpec((1, D_MODEL), lambda i, j: (0, 0)),
            pl.BlockSpec((D_MODEL, tf), lambda i, j: (0, j)),
            pl.BlockSpec((D_MODEL, tf), lambda i, j: (0, j + nj)),
            pl.BlockSpec((3, tf), lambda i, j: (0, j)),
            pl.BlockSpec((1, tf), lambda i, j: (0, j)),
            pl.BlockSpec((tf, D_MODEL), lambda i, j: (j, 0)),
        ],
        out_specs=pl.BlockSpec((tm, D_MODEL), row),
        out_shape=jax.ShapeDtypeStruct((R, D_MODEL), F32),
        scratch_shapes=[pltpu.VMEM((tm + 2 * CONV_HALO, D_MODEL), BF16)],
        compiler_params=_cparams("parallel", "arbitrary"),
        name="ffn",
    )(hs, hs, hs, gain, w_up, w_up, w_conv, b_conv, w_down)


def _rope_tables(n_tok):
    rows = n_tok // GRID_W
    pos_r = np.concatenate([np.zeros(N_META), np.repeat(np.arange(rows), GRID_W)])
    pos_c = np.concatenate([np.zeros(N_META), np.tile(np.arange(GRID_W), rows)])
    inv_freq = 1.0 / (ROPE_THETA ** (np.arange(ROPE_PAIRS) / ROPE_PAIRS))
    ang_r = pos_r[:, None] * inv_freq[None, :]
    ang_c = pos_c[:, None] * inv_freq[None, :]
    cos_t = np.concatenate([np.cos(ang_r), np.cos(ang_r), np.cos(ang_c), np.cos(ang_c)], axis=-1)
    sin_t = np.concatenate([-np.sin(ang_r), np.sin(ang_r), -np.sin(ang_c), np.sin(ang_c)], axis=-1)
    return jnp.asarray(cos_t, F32), jnp.asarray(sin_t, F32)


def _tiles(L):
    tm = L // ROW_TILES_PER_BATCH
    return dict(tm=tm, tq=256, tk=1280, tn_mix=512, tf=512)


def kernel(x, meta_tokens, norm_mix, norm_ffn, w_in, b_gate, q_norm, k_norm, w_attn_br, w_four, w_out,
           w_up, w_conv, b_conv, w_down):
    B, n_tok, D = x.shape
    depth = w_in.shape[0]
    L = n_tok + N_META
    t = _tiles(L)
    tm = t["tm"]
    assert D == D_MODEL and n_tok % GRID_W == 0 and L % DFT_N2 == 0
    assert L % tm == 0 and tm % PACK_ROWS == 0
    R = B * L
    tiles_per_batch = L // tm

    cos_t, sin_t = _rope_tables(n_tok)
    dft = _dft_constants(L)

    meta = jnp.broadcast_to(meta_tokens.astype(x.dtype)[None], (B, N_META, D))
    hs = jnp.concatenate([meta, x], axis=1).reshape(R, D)

    for i in range(depth):
        w_qkvf = w_in[i, :, :QKVF_WIDTH].astype(BF16)
        w_gates = w_in[i, :, QKVF_WIDTH:].astype(BF16)
        q, k, v, f, h = _inproj(hs, norm_mix[i][None], w_qkvf, q_norm[i][None], k_norm[i][None],
                                cos_t, sin_t, tm=tm, tiles_per_batch=tiles_per_batch)
        attn = _attention(q.reshape(B, L, ATTN_WIDTH), k.reshape(B, L, KV_WIDTH), v.reshape(B, L, KV_WIDTH),
                          tq=t["tq"], tk=t["tk"]).reshape(R, ATTN_WIDTH)
        four = _fourier(f.reshape(B, L, FOURIER_WIDTH), dft, chunk=tm).reshape(R, FOURIER_WIDTH)
        hs = _mix(hs, h, attn, four, w_gates, b_gate[i][None],
                  w_attn_br[i].astype(BF16), w_four[i].astype(BF16), w_out[i].astype(BF16),
                  tm=tm, tn=t["tn_mix"])
        hs = _ffn(hs, norm_ffn[i][None], w_up[i].astype(BF16), w_conv[i], b_conv[i][None],
                  w_down[i].astype(BF16), tm=tm, tf=t["tf"], tiles_per_batch=tiles_per_batch)

    return hs.reshape(B, L, D)[:, N_META:]
```

```python
import functools
import math

import jax
import jax.numpy as jnp
import numpy as np
from jax import lax
from jax.experimental import pallas as pl
from jax.experimental.pallas import tpu as pltpu

F32 = jnp.float32
BF16 = jnp.bfloat16

D_MODEL = 2048
N_HEADS = 8
N_KV_HEADS = 2
HEAD_DIM = 128
GQA_GROUP = N_HEADS // N_KV_HEADS
ATTN_WIDTH = N_HEADS * HEAD_DIM
KV_WIDTH = N_KV_HEADS * HEAD_DIM
N_FOURIER_GROUPS = 8
FOURIER_GROUP_CH = 128
FOURIER_WIDTH = N_FOURIER_GROUPS * FOURIER_GROUP_CH
QKVF_WIDTH = ATTN_WIDTH + 2 * KV_WIDTH + FOURIER_WIDTH
OFF_K = ATTN_WIDTH
OFF_V = OFF_K + KV_WIDTH
OFF_F = OFF_V + KV_WIDTH
D_FF = 5632
N_META = 16
GRID_W = 64
NORM_EPS = 1e-6
ROPE_THETA = 10000.0
ROPE_PAIRS = HEAD_DIM // 4

DFT_N2 = 80
ROW_TILES_PER_BATCH = 25
PACK_ROWS = 16
LANES = 128
CONV_HALO = PACK_ROWS
MASK_VALUE = -1e30
VMEM_LIMIT = 56 * 1024 * 1024


def _cparams(*sem):
    return pltpu.CompilerParams(dimension_semantics=sem, vmem_limit_bytes=VMEM_LIMIT)


def _rms(x):
    return x * lax.rsqrt(jnp.mean(x * x, axis=-1, keepdims=True) + NORM_EPS)


def _round_up(n, m):
    return -(-n // m) * m


def _meta_rows(x_ref, meta_ref, tiles_per_batch):
    split = x_ref.shape[0] - N_META
    last = pl.program_id(0) % tiles_per_batch == tiles_per_batch - 1
    return jnp.where(last, meta_ref[...], x_ref[split:, :])


def _inproj_kernel(x_ref, meta_ref, g_ref, w_ref, qn_ref, kn_ref, cos_ref, sin_ref,
                   q_ref, k_ref, v_ref, f_ref, h_ref, *, raw_input, tiles_per_batch):
    g = g_ref[...]
    if raw_input:
        split = x_ref.shape[0] - N_META
        h_ref[:split, :] = (_rms(x_ref[:split, :]) * g).astype(BF16)
        h_ref[split:, :] = (_rms(_meta_rows(x_ref, meta_ref, tiles_per_batch)) * g).astype(BF16)
    else:
        h_ref[...] = (_rms(x_ref[...]) * g).astype(BF16)
    proj = jnp.dot(h_ref[...], w_ref[...], preferred_element_type=F32)
    cos = cos_ref[...]
    sin = sin_ref[...]
    lane = lax.broadcasted_iota(jnp.int32, cos.shape, 1)
    first_half = (lane % (HEAD_DIM // 2)) < ROPE_PAIRS

    def norm_rope(t, gain):
        t = _rms(t) * gain
        partner = jnp.where(first_half,
                            pltpu.roll(t, HEAD_DIM - ROPE_PAIRS, 1),
                            pltpu.roll(t, ROPE_PAIRS, 1))
        return t * cos + partner * sin

    scale = math.log2(math.e) / math.sqrt(HEAD_DIM)
    for hh in range(N_HEADS):
        c = hh * HEAD_DIM
        q_ref[:, c:c + HEAD_DIM] = (norm_rope(proj[:, c:c + HEAD_DIM], qn_ref[...]) * scale).astype(BF16)
    for hh in range(N_KV_HEADS):
        c = hh * HEAD_DIM
        k_ref[:, c:c + HEAD_DIM] = norm_rope(proj[:, OFF_K + c:OFF_K + c + HEAD_DIM], kn_ref[...]).astype(BF16)
    v_ref[...] = proj[:, OFF_V:OFF_F].astype(BF16)
    f_ref[...] = proj[:, OFF_F:].astype(BF16)


def _inproj(xs, meta, gain, w_qkvf, qn, kn, cos_t, sin_t, *, tm, tiles_per_batch, raw_input):
    R = xs.shape[0] * tiles_per_batch * tm
    row = lambda i: (i, 0)
    fixed = lambda i: (0, 0)
    tab = lambda i: (i % tiles_per_batch, 0)
    widths = (ATTN_WIDTH, KV_WIDTH, KV_WIDTH, FOURIER_WIDTH, D_MODEL)
    return pl.pallas_call(
        functools.partial(_inproj_kernel, raw_input=raw_input, tiles_per_batch=tiles_per_batch),
        grid=(R // tm,),
        in_specs=[
            pl.BlockSpec((pl.Squeezed(), tm, D_MODEL), lambda i: (i // tiles_per_batch, i % tiles_per_batch, 0)),
            pl.BlockSpec((N_META, D_MODEL), fixed),
            pl.BlockSpec((1, D_MODEL), fixed),
            pl.BlockSpec((D_MODEL, QKVF_WIDTH), fixed),
            pl.BlockSpec((1, HEAD_DIM), fixed),
            pl.BlockSpec((1, HEAD_DIM), fixed),
            pl.BlockSpec((tm, HEAD_DIM), tab),
            pl.BlockSpec((tm, HEAD_DIM), tab),
        ],
        out_specs=[pl.BlockSpec((tm, w), row) for w in widths],
        out_shape=[jax.ShapeDtypeStruct((R, w), BF16) for w in widths],
        compiler_params=_cparams("parallel"),
        name="inproj",
    )(xs, meta, gain, w_qkvf, qn, kn, cos_t, sin_t)


def _attn_kernel(q_ref, k_ref, v_ref, o_ref, vt_ref, s_ref, mt_ref, m_ref, l_ref, acc_ref,
                 *, L, tk, n_full, tail, tail_pad):
    has_tail = tail > 0
    n_stage = n_full + (1 if has_tail else 0)
    off = 1 if has_tail else 0
    tail_start = n_full * tk

    def padded_tail(ref):
        rows = ref[0, tail_start:L, :]
        if tail_pad == tail:
            return rows
        return jnp.concatenate([rows, jnp.zeros((tail_pad - tail, HEAD_DIM), rows.dtype)], axis=0)

    @pl.when(pl.program_id(2) == 0)
    def _():
        for j in range(n_full):
            vt_ref[:, j * tk:(j + 1) * tk] = v_ref[0, j * tk:(j + 1) * tk, :].astype(F32).T.astype(BF16)
        if has_tail:
            vt_ref[:, tail_start:tail_start + tail_pad] = padded_tail(v_ref).astype(F32).T.astype(BF16)

    m_ref[...] = jnp.full(m_ref.shape, MASK_VALUE, F32)
    l_ref[...] = jnp.zeros(l_ref.shape, F32)
    acc_ref[...] = jnp.zeros(acc_ref.shape, F32)

    def is_tail(t):
        return has_tail and isinstance(t, int) and t == 0

    def tile_start(t):
        if isinstance(t, int):
            return tail_start if is_tail(t) else (t - off) * tk
        return pl.multiple_of((t - off) * tk, tk)

    def scores(t, slot):
        rows = tail_pad if is_tail(t) else tk
        kj = padded_tail(k_ref) if is_tail(t) else k_ref[0, pl.ds(tile_start(t), tk), :]
        for g in range(GQA_GROUP):
            c = g * HEAD_DIM
            st = lax.dot_general(kj, q_ref[0, :, c:c + HEAD_DIM], (((1,), (1,)), ((), ())),
                                 preferred_element_type=F32)
            if is_tail(t) and tail_pad != tail:
                key = lax.broadcasted_iota(jnp.int32, st.shape, 0)
                st = jnp.where(key < tail, st, MASK_VALUE)
            s_ref[slot, g, 0:rows, :] = st
            mt_ref[slot, g] = jnp.max(st, axis=0, keepdims=True)

    def softmax_pv(t, slot):
        rows = tail_pad if is_tail(t) else tk
        vtj = vt_ref[:, pl.ds(tile_start(t), rows)]
        for g in range(GQA_GROUP):
            st = s_ref[slot, g, 0:rows, :]
            m_prev = m_ref[g]
            m_new = jnp.maximum(m_prev, mt_ref[slot, g])
            alpha = jnp.exp2(m_prev - m_new)
            p = jnp.exp2(st - m_new)
            l_ref[g] = alpha * l_ref[g] + jnp.sum(p, axis=0, keepdims=True)
            acc_ref[g] = alpha * acc_ref[g] + jnp.dot(vtj, p.astype(BF16), preferred_element_type=F32)
            m_ref[g] = m_new

    def stage(t, slot):
        if not isinstance(t, int) or t + 1 < n_stage:
            scores(t + 1, 1 - slot)
        softmax_pv(t, slot)

    scores(0, 0)
    stage(0, 0)
    n_pairs = max(0, (n_stage - 2) // 2)
    if n_pairs > 0:
        def body(p, carry):
            stage(1 + 2 * p, 1)
            stage(2 + 2 * p, 0)
            return carry
        lax.fori_loop(0, n_pairs, body, 0)
    for t in range(1 + 2 * n_pairs, n_stage):
        stage(t, t % 2)
    for g in range(GQA_GROUP):
        c = g * HEAD_DIM
        o_ref[0, :, c:c + HEAD_DIM] = (acc_ref[g] / l_ref[g]).T.astype(BF16)


def _attention(q, k, v, *, tq, tk):
    B, L, _ = q.shape
    gw = GQA_GROUP * HEAD_DIM
    n_full = L // tk
    tail = L - n_full * tk
    tail_pad = _round_up(tail, LANES)
    kern = functools.partial(_attn_kernel, L=L, tk=tk, n_full=n_full, tail=tail, tail_pad=tail_pad)
    kv_spec = pl.BlockSpec((1, L, HEAD_DIM), lambda b, h, i: (b, 0, h))
    return pl.pallas_call(
        kern,
        grid=(B, N_KV_HEADS, pl.cdiv(L, tq)),
        in_specs=[pl.BlockSpec((1, tq, gw), lambda b, h, i: (b, i, h)), kv_spec, kv_spec],
        out_specs=pl.BlockSpec((1, tq, gw), lambda b, h, i: (b, i, h)),
        out_shape=jax.ShapeDtypeStruct((B, L, ATTN_WIDTH), BF16),
        scratch_shapes=[
            pltpu.VMEM((HEAD_DIM, n_full * tk + tail_pad), BF16),
            pltpu.VMEM((2, GQA_GROUP, max(tk, tail_pad), tq), F32),
            pltpu.VMEM((2, GQA_GROUP, 1, tq), F32),
            pltpu.VMEM((GQA_GROUP, 1, tq), F32),
            pltpu.VMEM((GQA_GROUP, 1, tq), F32),
            pltpu.VMEM((GQA_GROUP, HEAD_DIM, tq), F32),
        ],
        compiler_params=_cparams("parallel", "parallel", "arbitrary"),
        name="attention",
    )(q, k, v)


def _fourier_kernel(f_ref, cs_ref, f1r_ref, f1i_ref, g_ref, o_ref, wr_ref, wi_ref, os_ref,
                    *, L, n1, n1p, chunk):
    n_chunks = L // chunk
    last = L - chunk
    split = chunk - N_META

    def channel_dft_rows(r0, rows):
        return jnp.dot(f_ref[0, pl.ds(r0, rows), :], cs_ref[...], preferred_element_type=F32)

    def channel_dft(c, carry):
        r0 = pl.multiple_of(c * chunk, PACK_ROWS)
        z = channel_dft_rows(r0, chunk)
        wr_ref[pl.ds(r0 + N_META, chunk), :] = z[:, :FOURIER_GROUP_CH]
        wi_ref[pl.ds(r0 + N_META, chunk), :] = z[:, FOURIER_GROUP_CH:]
        return carry

    lax.fori_loop(0, n_chunks - 1, channel_dft, 0, unroll=4 if (n_chunks - 1) % 4 == 0 else 1)
    z = channel_dft_rows(last, chunk)
    wr_ref[last + N_META:L, :] = z[:split, :FOURIER_GROUP_CH]
    wi_ref[last + N_META:L, :] = z[:split, FOURIER_GROUP_CH:]
    wr_ref[0:N_META, :] = z[split:, :FOURIER_GROUP_CH]
    wi_ref[0:N_META, :] = z[split:, FOURIER_GROUP_CH:]
    if n1p > n1:
        zeros = jnp.zeros(((n1p - n1) * DFT_N2, FOURIER_GROUP_CH), F32)
        wr_ref[L:, :] = zeros
        wi_ref[L:, :] = zeros

    def row_dft(n2, carry):
        rows = pl.ds(n2, n1p, stride=DFT_N2)
        zr = wr_ref[rows, :].astype(BF16)
        zi = wi_ref[rows, :].astype(BF16)
        a = (jnp.dot(f1r_ref[...], zr, preferred_element_type=F32)
             + jnp.dot(f1i_ref[...], zi, preferred_element_type=F32))
        wr_ref[rows, :] = a[:n1p]
        wi_ref[rows, :] = a[n1p:]
        return carry

    lax.fori_loop(0, DFT_N2, row_dft, 0, unroll=16)

    def col_dft(k1, carry):
        r0 = pl.multiple_of(k1 * DFT_N2, DFT_N2)
        a = jnp.concatenate([wr_ref[pl.ds(r0, DFT_N2), :].astype(BF16),
                             wi_ref[pl.ds(r0, DFT_N2), :].astype(BF16)], axis=0)
        os_ref[pl.ds(k1, DFT_N2, stride=n1), :] = jnp.dot(g_ref[k1], a, preferred_element_type=F32)
        return carry

    lax.fori_loop(0, n1, col_dft, 0, unroll=(41 if n1 % 41 == 0 else 5) if n1 % 5 == 0 else 1)

    def emit(c, carry):
        r0 = pl.multiple_of(c * chunk, PACK_ROWS)
        o_ref[0, pl.ds(r0, chunk), :] = os_ref[pl.ds(r0 + N_META, chunk), :].astype(BF16)
        return carry

    lax.fori_loop(0, n_chunks - 1, emit, 0)
    o_ref[0, last:L - N_META, :] = os_ref[last + N_META:L, :].astype(BF16)
    o_ref[0, L - N_META:L, :] = os_ref[0:N_META, :].astype(BF16)


def _fourier(f, consts, *, chunk):
    B, L, _ = f.shape
    n1 = L // DFT_N2
    n1p = _round_up(n1, 8)
    cs, f1r, f1i, g = consts
    slab = pl.BlockSpec((1, L, FOURIER_GROUP_CH), lambda b, c: (b, 0, c))
    fixed2 = lambda b, c: (0, 0)
    return pl.pallas_call(
        functools.partial(_fourier_kernel, L=L, n1=n1, n1p=n1p, chunk=chunk),
        grid=(B, N_FOURIER_GROUPS),
        in_specs=[
            slab,
            pl.BlockSpec((FOURIER_GROUP_CH, 2 * FOURIER_GROUP_CH), fixed2),
            pl.BlockSpec((2 * n1p, n1p), fixed2),
            pl.BlockSpec((2 * n1p, n1p), fixed2),
            pl.BlockSpec((n1, DFT_N2, 2 * DFT_N2), lambda b, c: (0, 0, 0), pipeline_mode=pl.Buffered(1)),
        ],
        out_specs=slab,
        out_shape=jax.ShapeDtypeStruct((B, L, FOURIER_WIDTH), BF16),
        scratch_shapes=[pltpu.VMEM((n1p * DFT_N2, FOURIER_GROUP_CH), F32)] * 2
        + [pltpu.VMEM((L, FOURIER_GROUP_CH), F32)],
        compiler_params=_cparams("parallel", "parallel"),
        name="fourier",
    )(f, cs, f1r, f1i, g)


def _dft_constants(L):
    n1 = L // DFT_N2
    n1p = _round_up(n1, 8)
    k1 = np.arange(n1)
    ang1 = 2.0 * np.pi * np.outer(k1, k1) / n1
    c1, s1 = np.cos(ang1), np.sin(ang1)
    f1r = np.zeros((2 * n1p, n1p))
    f1i = np.zeros((2 * n1p, n1p))
    f1r[:n1, :n1] = c1
    f1r[n1p:n1p + n1, :n1] = -s1
    f1i[:n1, :n1] = s1
    f1i[n1p:n1p + n1, :n1] = c1
    n2 = np.arange(DFT_N2)
    k = k1[:, None, None] + n1 * n2[None, :, None]
    ang = 2.0 * np.pi * ((k * n2[None, None, :]) % L) / L
    g = np.concatenate([np.cos(ang), np.sin(ang)], axis=2) / math.sqrt(L)
    c = np.arange(FOURIER_GROUP_CH)
    angc = 2.0 * np.pi * np.outer(c, c) / FOURIER_GROUP_CH
    cs = np.concatenate([np.cos(angc), -np.sin(angc)], axis=1) / math.sqrt(FOURIER_GROUP_CH)
    return (jnp.asarray(cs, BF16), jnp.asarray(f1r, BF16), jnp.asarray(f1i, BF16), jnp.asarray(g, BF16))


def _mix_kernel(x_ref, meta_ref, h_ref, attn_ref, four_ref, wga_ref, wgf_ref, bga_ref, bgf_ref,
                wa_ref, wf_ref, wo_ref, o_ref, *, raw_input, tiles_per_batch):
    @pl.when(pl.program_id(1) == 0)
    def _():
        if raw_input:
            split = x_ref.shape[0] - N_META
            o_ref[:split, :] = x_ref[:split, :]
            o_ref[split:, :] = _meta_rows(x_ref, meta_ref, tiles_per_batch)
        else:
            o_ref[...] = x_ref[...]

    h = h_ref[...]
    g_attn = jax.nn.sigmoid(jnp.dot(h, wga_ref[...], preferred_element_type=F32) + bga_ref[...])
    g_four = jax.nn.sigmoid(jnp.dot(h, wgf_ref[...], preferred_element_type=F32) + bgf_ref[...])
    a_br = jnp.dot(attn_ref[...], wa_ref[...], preferred_element_type=F32)
    s_br = jnp.dot(four_ref[...], wf_ref[...], preferred_element_type=F32)
    merged = (g_attn * a_br + g_four * s_br).astype(BF16)
    o_ref[...] += jnp.dot(merged, wo_ref[...], preferred_element_type=F32)


def _mix(xs, meta, h, attn, four, w_gates, b_gates, w_a, w_f, w_o, *, tm, tn, tiles_per_batch, raw_input):
    R = h.shape[0]
    nj = D_MODEL // tn
    row = lambda i, j: (i, 0)
    return pl.pallas_call(
        functools.partial(_mix_kernel, raw_input=raw_input, tiles_per_batch=tiles_per_batch),
        grid=(R // tm, nj),
        in_specs=[
            pl.BlockSpec((pl.Squeezed(), tm, D_MODEL),
                         lambda i, j: (i // tiles_per_batch, i % tiles_per_batch, 0)),
            pl.BlockSpec((N_META, D_MODEL), lambda i, j: (0, 0)),
            pl.BlockSpec((tm, D_MODEL), row),
            pl.BlockSpec((tm, ATTN_WIDTH), row),
            pl.BlockSpec((tm, FOURIER_WIDTH), row),
            pl.BlockSpec((D_MODEL, tn), lambda i, j: (0, j)),
            pl.BlockSpec((D_MODEL, tn), lambda i, j: (0, j + nj)),
            pl.BlockSpec((1, tn), lambda i, j: (0, j)),
            pl.BlockSpec((1, tn), lambda i, j: (0, j + nj)),
            pl.BlockSpec((ATTN_WIDTH, tn), lambda i, j: (0, j)),
            pl.BlockSpec((FOURIER_WIDTH, tn), lambda i, j: (0, j)),
            pl.BlockSpec((tn, D_MODEL), lambda i, j: (j, 0)),
        ],
        out_specs=pl.BlockSpec((tm, D_MODEL), row),
        out_shape=jax.ShapeDtypeStruct((R, D_MODEL), F32),
        compiler_params=_cparams("parallel", "arbitrary"),
        name="mix_out",
    )(xs, meta, h, attn, four, w_gates, w_gates, b_gates, b_gates, w_a, w_f, w_o)


def _ffn_kernel(x_ref, xp_ref, xn_ref, g_ref, wg_ref, wv_ref, wc_ref, bc_ref, wd_ref,
                o_ref, h_ref, *, tm, tiles_per_batch):
    j = pl.program_id(1)
    H = CONV_HALO

    @pl.when(j == 0)
    def _():
        x = x_ref[...]
        g = g_ref[...]
        h_ref[H:H + tm, :] = (_rms(x) * g).astype(BF16)
        h_ref[0:H, :] = (_rms(xp_ref[...]) * g).astype(BF16)
        h_ref[H + tm:, :] = (_rms(xn_ref[...]) * g).astype(BF16)
        o_ref[...] = x

    h = h_ref[...]
    up_g = jnp.dot(h, wg_ref[...], preferred_element_type=F32)
    n = tm + 2 * H
    wc = wc_ref[...]
    prev = pltpu.roll(up_g, 1, 0)[H:H + tm]
    nxt = pltpu.roll(up_g, n - 1, 0)[H:H + tm]
    row = lax.broadcasted_iota(jnp.int32, (tm, 1), 0)
    seam = jnp.where(pl.program_id(0) % tiles_per_batch == tiles_per_batch - 1, tm - N_META, -1)
    prev = jnp.where(row == seam, 0.0, prev)
    nxt = jnp.where(row == seam - 1, 0.0, nxt)
    u = prev * wc[0:1] + up_g[H:H + tm] * wc[1:2] + nxt * wc[2:3] + bc_ref[...]
    val = jnp.dot(h[H:H + tm], wv_ref[...], preferred_element_type=F32)
    act = (u * jax.nn.sigmoid(u) * val).astype(BF16)
    o_ref[...] += jnp.dot(act, wd_ref[...], preferred_element_type=F32)


def _ffn(hs, gain, w_up, w_conv, b_conv, w_down, *, tm, tf, tiles_per_batch, n_tok_out=None):
    R = hs.shape[0]
    nj = D_FF // tf
    hb = tm // CONV_HALO
    nb = tiles_per_batch * hb
    row = lambda i, j: (i, 0)

    def prev_block(i, j):
        first = i % tiles_per_batch == 0
        return (jnp.where(first, (i // tiles_per_batch) * nb + nb - 1, i * hb - 1), 0)

    def next_block(i, j):
        last = i % tiles_per_batch == tiles_per_batch - 1
        return (jnp.where(last, (i // tiles_per_batch) * nb, (i + 1) * hb), 0)

    if n_tok_out is None:
        out_spec = pl.BlockSpec((tm, D_MODEL), row)
        out_shape = jax.ShapeDtypeStruct((R, D_MODEL), F32)
    else:
        out_spec = pl.BlockSpec((pl.Squeezed(), tm, D_MODEL),
                                lambda i, j: (i // tiles_per_batch, i % tiles_per_batch, 0))
        out_shape = jax.ShapeDtypeStruct((R // (tiles_per_batch * tm), n_tok_out, D_MODEL), F32)
    return pl.pallas_call(
        functools.partial(_ffn_kernel, tm=tm, tiles_per_batch=tiles_per_batch),
        grid=(R // tm, nj),
        in_specs=[
            pl.BlockSpec((tm, D_MODEL), row),
            pl.BlockSpec((CONV_HALO, D_MODEL), prev_block),
            pl.BlockSpec((CONV_HALO, D_MODEL), next_block),
            pl.BlockSpec((1, D_MODEL), lambda i, j: (0, 0)),
            pl.BlockSpec((D_MODEL, tf), lambda i, j: (0, j)),
            pl.BlockSpec((D_MODEL, tf), lambda i, j: (0, j + nj)),
            pl.BlockSpec((3, tf), lambda i, j: (0, j)),
            pl.BlockSpec((1, tf), lambda i, j: (0, j)),
            pl.BlockSpec((tf, D_MODEL), lambda i, j: (j, 0)),
        ],
        out_specs=out_spec,
        out_shape=out_shape,
        scratch_shapes=[pltpu.VMEM((tm + 2 * CONV_HALO, D_MODEL), BF16)],
        compiler_params=_cparams("parallel", "arbitrary"),
        name="ffn",
    )(hs, hs, hs, gain, w_up, w_up, w_conv, b_conv, w_down)


def _rope_tables(n_tok):
    rows = n_tok // GRID_W
    pos_r = np.concatenate([np.repeat(np.arange(rows), GRID_W), np.zeros(N_META)])
    pos_c = np.concatenate([np.tile(np.arange(GRID_W), rows), np.zeros(N_META)])
    inv_freq = 1.0 / (ROPE_THETA ** (np.arange(ROPE_PAIRS) / ROPE_PAIRS))
    ang_r = pos_r[:, None] * inv_freq[None, :]
    ang_c = pos_c[:, None] * inv_freq[None, :]
    cos_t = np.concatenate([np.cos(ang_r), np.cos(ang_r), np.cos(ang_c), np.cos(ang_c)], axis=-1)
    sin_t = np.concatenate([-np.sin(ang_r), np.sin(ang_r), -np.sin(ang_c), np.sin(ang_c)], axis=-1)
    return jnp.asarray(cos_t, F32), jnp.asarray(sin_t, F32)


def _tiles(L):
    tm = L // ROW_TILES_PER_BATCH
    return dict(tm=tm, tq=256, tk=1280, tn_mix=512, tf=512)


def kernel(x, meta_tokens, norm_mix, norm_ffn, w_in, b_gate, q_norm, k_norm, w_attn_br, w_four, w_out,
           w_up, w_conv, b_conv, w_down):
    B, n_tok, D = x.shape
    depth = w_in.shape[0]
    L = n_tok + N_META
    t = _tiles(L)
    tm = t["tm"]
    assert D == D_MODEL and n_tok % GRID_W == 0 and L % DFT_N2 == 0
    assert L % tm == 0 and tm % PACK_ROWS == 0
    R = B * L
    tiles_per_batch = L // tm

    cos_t, sin_t = _rope_tables(n_tok)
    dft = _dft_constants(L)

    meta = meta_tokens.astype(x.dtype)
    xs = x
    for i in range(depth):
        raw = i == 0
        w_qkvf = w_in[i, :, :QKVF_WIDTH].astype(BF16)
        w_gates = w_in[i, :, QKVF_WIDTH:].astype(BF16)
        q, k, v, f, h = _inproj(xs, meta, norm_mix[i][None], w_qkvf, q_norm[i][None], k_norm[i][None],
                                cos_t, sin_t, tm=tm, tiles_per_batch=tiles_per_batch, raw_input=raw)
        attn = _attention(q.reshape(B, L, ATTN_WIDTH), k.reshape(B, L, KV_WIDTH), v.reshape(B, L, KV_WIDTH),
                          tq=t["tq"], tk=t["tk"]).reshape(R, ATTN_WIDTH)
        four = _fourier(f.reshape(B, L, FOURIER_WIDTH), dft, chunk=tm).reshape(R, FOURIER_WIDTH)
        hs = _mix(xs, meta, h, attn, four, w_gates, b_gate[i][None],
                  w_attn_br[i].astype(BF16), w_four[i].astype(BF16), w_out[i].astype(BF16),
                  tm=tm, tn=t["tn_mix"], tiles_per_batch=tiles_per_batch, raw_input=raw)
        hs = _ffn(hs, norm_ffn[i][None], w_up[i].astype(BF16), w_conv[i], b_conv[i][None],
                  w_down[i].astype(BF16), tm=tm, tf=t["tf"], tiles_per_batch=tiles_per_batch,
                  n_tok_out=n_tok if i == depth - 1 else None)
        xs = hs.reshape(B, L, D) if i < depth - 1 else hs
    return xs
```

```python
import functools
import math

import jax
import jax.numpy as jnp
import numpy as np
from jax import lax
from jax.experimental import pallas as pl
from jax.experimental.pallas import tpu as pltpu

F32 = jnp.float32
BF16 = jnp.bfloat16

D_MODEL = 2048
N_HEADS = 8
N_KV_HEADS = 2
HEAD_DIM = 128
GQA_GROUP = N_HEADS // N_KV_HEADS
ATTN_WIDTH = N_HEADS * HEAD_DIM
KV_WIDTH = N_KV_HEADS * HEAD_DIM
N_FOURIER_GROUPS = 8
FOURIER_GROUP_CH = 128
FOURIER_WIDTH = N_FOURIER_GROUPS * FOURIER_GROUP_CH
QKVF_WIDTH = ATTN_WIDTH + 2 * KV_WIDTH + FOURIER_WIDTH
OFF_K = ATTN_WIDTH
OFF_V = OFF_K + KV_WIDTH
OFF_F = OFF_V + KV_WIDTH
D_FF = 5632
N_META = 16
GRID_W = 64
NORM_EPS = 1e-6
ROPE_THETA = 10000.0
ROPE_PAIRS = HEAD_DIM // 4

DFT_N2 = 80
ROW_TILES_PER_BATCH = 25
PACK_ROWS = 16
LANES = 128
CONV_HALO = PACK_ROWS
MASK_VALUE = -1e30
VMEM_LIMIT = 56 * 1024 * 1024


def _cparams(*sem):
    return pltpu.CompilerParams(dimension_semantics=sem, vmem_limit_bytes=VMEM_LIMIT)


def _rms(x):
    return x * lax.rsqrt(jnp.mean(x * x, axis=-1, keepdims=True) + NORM_EPS)


def _round_up(n, m):
    return -(-n // m) * m


def _meta_rows(x_ref, meta_ref, tiles_per_batch):
    split = x_ref.shape[0] - N_META
    last = pl.program_id(0) % tiles_per_batch == tiles_per_batch - 1
    return jnp.where(last, meta_ref[...], x_ref[split:, :])


def _inproj_kernel(x_ref, meta_ref, g_ref, w_ref, qn_ref, kn_ref, cos_ref, sin_ref,
                   q_ref, k_ref, v_ref, f_ref, h_ref, *, raw_input, tiles_per_batch):
    g = g_ref[...]
    if raw_input:
        split = x_ref.shape[0] - N_META
        h_ref[:split, :] = (_rms(x_ref[:split, :]) * g).astype(BF16)
        h_ref[split:, :] = (_rms(_meta_rows(x_ref, meta_ref, tiles_per_batch)) * g).astype(BF16)
    else:
        h_ref[...] = (_rms(x_ref[...]) * g).astype(BF16)
    proj = jnp.dot(h_ref[...], w_ref[...], preferred_element_type=F32)
    cos = cos_ref[...]
    sin = sin_ref[...]
    lane = lax.broadcasted_iota(jnp.int32, cos.shape, 1)
    first_half = (lane % (HEAD_DIM // 2)) < ROPE_PAIRS

    def norm_rope(t, gain):
        t = _rms(t) * gain
        partner = jnp.where(first_half,
                            pltpu.roll(t, HEAD_DIM - ROPE_PAIRS, 1),
                            pltpu.roll(t, ROPE_PAIRS, 1))
        return t * cos + partner * sin

    scale = math.log2(math.e) / math.sqrt(HEAD_DIM)
    for hh in range(N_HEADS):
        c = hh * HEAD_DIM
        q_ref[:, c:c + HEAD_DIM] = (norm_rope(proj[:, c:c + HEAD_DIM], qn_ref[...]) * scale).astype(BF16)
    for hh in range(N_KV_HEADS):
        c = hh * HEAD_DIM
        k_ref[:, c:c + HEAD_DIM] = norm_rope(proj[:, OFF_K + c:OFF_K + c + HEAD_DIM], kn_ref[...]).astype(BF16)
    v_ref[...] = proj[:, OFF_V:OFF_F].astype(BF16)
    f_ref[...] = proj[:, OFF_F:].astype(BF16)


def _inproj(xs, meta, gain, w_qkvf, qn, kn, cos_t, sin_t, *, tm, tiles_per_batch, raw_input):
    R = xs.shape[0] * tiles_per_batch * tm
    row = lambda i: (i, 0)
    fixed = lambda i: (0, 0)
    tab = lambda i: (i % tiles_per_batch, 0)
    widths = (ATTN_WIDTH, KV_WIDTH, KV_WIDTH, FOURIER_WIDTH, D_MODEL)
    return pl.pallas_call(
        functools.partial(_inproj_kernel, raw_input=raw_input, tiles_per_batch=tiles_per_batch),
        grid=(R // tm,),
        in_specs=[
            pl.BlockSpec((pl.Squeezed(), tm, D_MODEL), lambda i: (i // tiles_per_batch, i % tiles_per_batch, 0)),
            pl.BlockSpec((N_META, D_MODEL), fixed),
            pl.BlockSpec((1, D_MODEL), fixed),
            pl.BlockSpec((D_MODEL, QKVF_WIDTH), fixed),
            pl.BlockSpec((1, HEAD_DIM), fixed),
            pl.BlockSpec((1, HEAD_DIM), fixed),
            pl.BlockSpec((tm, HEAD_DIM), tab),
            pl.BlockSpec((tm, HEAD_DIM), tab),
        ],
        out_specs=[pl.BlockSpec((tm, w), row) for w in widths],
        out_shape=[jax.ShapeDtypeStruct((R, w), BF16) for w in widths],
        compiler_params=_cparams("parallel"),
        name="inproj",
    )(xs, meta, gain, w_qkvf, qn, kn, cos_t, sin_t)


def _attn_kernel(q_ref, k_ref, v_ref, o_ref, vt_ref, qt_ref, s_ref, mt_ref, m_ref, l_ref, acc_ref,
                 *, L, tk, n_full, tail, tail_pad, qt):
    has_tail = tail > 0
    n_stage = n_full + (1 if has_tail else 0)
    off = 1 if has_tail else 0
    tail_start = n_full * tk

    def padded_tail(ref):
        rows = ref[0, tail_start:L, :]
        if tail_pad == tail:
            return rows
        return jnp.concatenate([rows, jnp.zeros((tail_pad - tail, HEAD_DIM), rows.dtype)], axis=0)

    @pl.when(pl.program_id(2) == 0)
    def _():
        for j in range(n_full):
            vt_ref[:, j * tk:(j + 1) * tk] = v_ref[0, j * tk:(j + 1) * tk, :].astype(F32).T.astype(BF16)
        if has_tail:
            vt_ref[:, tail_start:tail_start + tail_pad] = padded_tail(v_ref).astype(F32).T.astype(BF16)

    m_ref[...] = jnp.full(m_ref.shape, MASK_VALUE, F32)
    l_ref[...] = jnp.zeros(l_ref.shape, F32)
    acc_ref[...] = jnp.zeros(acc_ref.shape, F32)
    if qt:
        for g in range(GQA_GROUP):
            qt_ref[g] = q_ref[0, :, g * HEAD_DIM:(g + 1) * HEAD_DIM].astype(F32).T.astype(BF16)

    def is_tail(t):
        return has_tail and isinstance(t, int) and t == 0

    def tile_start(t):
        if isinstance(t, int):
            return tail_start if is_tail(t) else (t - off) * tk
        return pl.multiple_of((t - off) * tk, tk)

    def scores(t, slot):
        rows = tail_pad if is_tail(t) else tk
        kj = padded_tail(k_ref) if is_tail(t) else k_ref[0, pl.ds(tile_start(t), tk), :]
        for g in range(GQA_GROUP):
            c = g * HEAD_DIM
            if qt:
                st = jnp.dot(kj, qt_ref[g], preferred_element_type=F32)
            else:
                st = lax.dot_general(kj, q_ref[0, :, c:c + HEAD_DIM], (((1,), (1,)), ((), ())),
                                     preferred_element_type=F32)
            if is_tail(t) and tail_pad != tail:
                key = lax.broadcasted_iota(jnp.int32, st.shape, 0)
                st = jnp.where(key < tail, st, MASK_VALUE)
            s_ref[slot, g, 0:rows, :] = st
            mt_ref[slot, g] = jnp.max(st, axis=0, keepdims=True)

    def softmax_pv(t, slot):
        rows = tail_pad if is_tail(t) else tk
        vtj = vt_ref[:, pl.ds(tile_start(t), rows)]
        for g in range(GQA_GROUP):
            st = s_ref[slot, g, 0:rows, :]
            m_prev = m_ref[g]
            m_new = jnp.maximum(m_prev, mt_ref[slot, g])
            alpha = jnp.exp2(m_prev - m_new)
            p = jnp.exp2(st - m_new)
            l_ref[g] = alpha * l_ref[g] + jnp.sum(p, axis=0, keepdims=True)
            acc_ref[g] = alpha * acc_ref[g] + jnp.dot(vtj, p.astype(BF16), preferred_element_type=F32)
            m_ref[g] = m_new

    def stage(t, slot):
        if not isinstance(t, int) or t + 1 < n_stage:
            scores(t + 1, 1 - slot)
        softmax_pv(t, slot)

    scores(0, 0)
    stage(0, 0)
    n_pairs = max(0, (n_stage - 2) // 2)
    if n_pairs > 0:
        def body(p, carry):
            stage(1 + 2 * p, 1)
            stage(2 + 2 * p, 0)
            return carry
        lax.fori_loop(0, n_pairs, body, 0)
    for t in range(1 + 2 * n_pairs, n_stage):
        stage(t, t % 2)
    for g in range(GQA_GROUP):
        c = g * HEAD_DIM
        o_ref[0, :, c:c + HEAD_DIM] = (acc_ref[g] / l_ref[g]).T.astype(BF16)


def _attention(q, k, v, *, tq, tk, qt):
    B, L, _ = q.shape
    gw = GQA_GROUP * HEAD_DIM
    n_full = L // tk
    tail = L - n_full * tk
    tail_pad = _round_up(tail, LANES)
    kern = functools.partial(_attn_kernel, L=L, tk=tk, n_full=n_full, tail=tail, tail_pad=tail_pad, qt=qt)
    kv_spec = pl.BlockSpec((1, L, HEAD_DIM), lambda b, h, i: (b, 0, h))
    return pl.pallas_call(
        kern,
        grid=(B, N_KV_HEADS, pl.cdiv(L, tq)),
        in_specs=[pl.BlockSpec((1, tq, gw), lambda b, h, i: (b, i, h)), kv_spec, kv_spec],
        out_specs=pl.BlockSpec((1, tq, gw), lambda b, h, i: (b, i, h)),
        out_shape=jax.ShapeDtypeStruct((B, L, ATTN_WIDTH), BF16),
        scratch_shapes=[
            pltpu.VMEM((HEAD_DIM, n_full * tk + tail_pad), BF16),
            pltpu.VMEM((GQA_GROUP, HEAD_DIM, tq), BF16),
            pltpu.VMEM((2, GQA_GROUP, max(tk, tail_pad), tq), F32),
            pltpu.VMEM((2, GQA_GROUP, 1, tq), F32),
            pltpu.VMEM((GQA_GROUP, 1, tq), F32),
            pltpu.VMEM((GQA_GROUP, 1, tq), F32),
            pltpu.VMEM((GQA_GROUP, HEAD_DIM, tq), F32),
        ],
        compiler_params=_cparams("parallel", "parallel", "arbitrary"),
        name="attention",
    )(q, k, v)


def _fourier_kernel(f_ref, cs_ref, f1r_ref, f1i_ref, g_ref, o_ref, wr_ref, wi_ref, os_ref,
                    *, L, n1, n1p, chunk):
    n_chunks = L // chunk
    last = L - chunk
    split = chunk - N_META

    def channel_dft_rows(r0, rows):
        return jnp.dot(f_ref[0, pl.ds(r0, rows), :], cs_ref[...], preferred_element_type=F32)

    def channel_dft(c, carry):
        r0 = pl.multiple_of(c * chunk, PACK_ROWS)
        z = channel_dft_rows(r0, chunk)
        wr_ref[pl.ds(r0 + N_META, chunk), :] = z[:, :FOURIER_GROUP_CH]
        wi_ref[pl.ds(r0 + N_META, chunk), :] = z[:, FOURIER_GROUP_CH:]
        return carry

    lax.fori_loop(0, n_chunks - 1, channel_dft, 0, unroll=4 if (n_chunks - 1) % 4 == 0 else 1)
    z = channel_dft_rows(last, chunk)
    wr_ref[last + N_META:L, :] = z[:split, :FOURIER_GROUP_CH]
    wi_ref[last + N_META:L, :] = z[:split, FOURIER_GROUP_CH:]
    wr_ref[0:N_META, :] = z[split:, :FOURIER_GROUP_CH]
    wi_ref[0:N_META, :] = z[split:, FOURIER_GROUP_CH:]
    if n1p > n1:
        zeros = jnp.zeros(((n1p - n1) * DFT_N2, FOURIER_GROUP_CH), F32)
        wr_ref[L:, :] = zeros
        wi_ref[L:, :] = zeros

    def row_dft(n2, carry):
        rows = pl.ds(n2, n1p, stride=DFT_N2)
        zr = wr_ref[rows, :].astype(BF16)
        zi = wi_ref[rows, :].astype(BF16)
        a = (jnp.dot(f1r_ref[...], zr, preferred_element_type=F32)
             + jnp.dot(f1i_ref[...], zi, preferred_element_type=F32))
        wr_ref[rows, :] = a[:n1p]
        wi_ref[rows, :] = a[n1p:]
        return carry

    lax.fori_loop(0, DFT_N2, row_dft, 0, unroll=16)

    def col_dft(k1, carry):
        r0 = pl.multiple_of(k1 * DFT_N2, DFT_N2)
        a = jnp.concatenate([wr_ref[pl.ds(r0, DFT_N2), :].astype(BF16),
                             wi_ref[pl.ds(r0, DFT_N2), :].astype(BF16)], axis=0)
        os_ref[pl.ds(k1, DFT_N2, stride=n1), :] = jnp.dot(g_ref[k1], a, preferred_element_type=F32)
        return carry

    lax.fori_loop(0, n1, col_dft, 0, unroll=(41 if n1 % 41 == 0 else 5) if n1 % 5 == 0 else 1)

    def emit(c, carry):
        r0 = pl.multiple_of(c * chunk, PACK_ROWS)
        o_ref[0, pl.ds(r0, chunk), :] = os_ref[pl.ds(r0 + N_META, chunk), :].astype(BF16)
        return carry

    lax.fori_loop(0, n_chunks - 1, emit, 0)
    o_ref[0, last:L - N_META, :] = os_ref[last + N_META:L, :].astype(BF16)
    o_ref[0, L - N_META:L, :] = os_ref[0:N_META, :].astype(BF16)


def _fourier(f, consts, *, chunk):
    B, L, _ = f.shape
    n1 = L // DFT_N2
    n1p = _round_up(n1, 8)
    cs, f1r, f1i, g = consts
    slab = pl.BlockSpec((1, L, FOURIER_GROUP_CH), lambda b, c: (b, 0, c))
    fixed2 = lambda b, c: (0, 0)
    return pl.pallas_call(
        functools.partial(_fourier_kernel, L=L, n1=n1, n1p=n1p, chunk=chunk),
        grid=(B, N_FOURIER_GROUPS),
        in_specs=[
            slab,
            pl.BlockSpec((FOURIER_GROUP_CH, 2 * FOURIER_GROUP_CH), fixed2),
            pl.BlockSpec((2 * n1p, n1p), fixed2),
            pl.BlockSpec((2 * n1p, n1p), fixed2),
            pl.BlockSpec((n1, DFT_N2, 2 * DFT_N2), lambda b, c: (0, 0, 0), pipeline_mode=pl.Buffered(1)),
        ],
        out_specs=slab,
        out_shape=jax.ShapeDtypeStruct((B, L, FOURIER_WIDTH), BF16),
        scratch_shapes=[pltpu.VMEM((n1p * DFT_N2, FOURIER_GROUP_CH), F32)] * 2
        + [pltpu.VMEM((L, FOURIER_GROUP_CH), F32)],
        compiler_params=_cparams("parallel", "parallel"),
        name="fourier",
    )(f, cs, f1r, f1i, g)


def _dft_constants(L):
    n1 = L // DFT_N2
    n1p = _round_up(n1, 8)
    k1 = np.arange(n1)
    ang1 = 2.0 * np.pi * np.outer(k1, k1) / n1
    c1, s1 = np.cos(ang1), np.sin(ang1)
    f1r = np.zeros((2 * n1p, n1p))
    f1i = np.zeros((2 * n1p, n1p))
    f1r[:n1, :n1] = c1
    f1r[n1p:n1p + n1, :n1] = -s1
    f1i[:n1, :n1] = s1
    f1i[n1p:n1p + n1, :n1] = c1
    n2 = np.arange(DFT_N2)
    k = k1[:, None, None] + n1 * n2[None, :, None]
    ang = 2.0 * np.pi * ((k * n2[None, None, :]) % L) / L
    g = np.concatenate([np.cos(ang), np.sin(ang)], axis=2) / math.sqrt(L)
    c = np.arange(FOURIER_GROUP_CH)
    angc = 2.0 * np.pi * np.outer(c, c) / FOURIER_GROUP_CH
    cs = np.concatenate([np.cos(angc), -np.sin(angc)], axis=1) / math.sqrt(FOURIER_GROUP_CH)
    return (jnp.asarray(cs, BF16), jnp.asarray(f1r, BF16), jnp.asarray(f1i, BF16), jnp.asarray(g, BF16))


def _mix_kernel(x_ref, meta_ref, h_ref, attn_ref, four_ref, wga_ref, wgf_ref, bga_ref, bgf_ref,
                wa_ref, wf_ref, wo_ref, o_ref, *, raw_input, tiles_per_batch):
    @pl.when(pl.program_id(1) == 0)
    def _():
        if raw_input:
            split = x_ref.shape[0] - N_META
            o_ref[:split, :] = x_ref[:split, :]
            o_ref[split:, :] = _meta_rows(x_ref, meta_ref, tiles_per_batch)
        else:
            o_ref[...] = x_ref[...]

    h = h_ref[...]
    g_attn = jax.nn.sigmoid(jnp.dot(h, wga_ref[...], preferred_element_type=F32) + bga_ref[...])
    g_four = jax.nn.sigmoid(jnp.dot(h, wgf_ref[...], preferred_element_type=F32) + bgf_ref[...])
    a_br = jnp.dot(attn_ref[...], wa_ref[...], preferred_element_type=F32)
    s_br = jnp.dot(four_ref[...], wf_ref[...], preferred_element_type=F32)
    merged = (g_attn * a_br + g_four * s_br).astype(BF16)
    o_ref[...] += jnp.dot(merged, wo_ref[...], preferred_element_type=F32)


def _mix(xs, meta, h, attn, four, w_gates, b_gates, w_a, w_f, w_o, *, tm, tn, tiles_per_batch, raw_input):
    R = h.shape[0]
    nj = D_MODEL // tn
    row = lambda i, j: (i, 0)
    return pl.pallas_call(
        functools.partial(_mix_kernel, raw_input=raw_input, tiles_per_batch=tiles_per_batch),
        grid=(R // tm, nj),
        in_specs=[
            pl.BlockSpec((pl.Squeezed(), tm, D_MODEL),
                         lambda i, j: (i // tiles_per_batch, i % tiles_per_batch, 0)),
            pl.BlockSpec((N_META, D_MODEL), lambda i, j: (0, 0)),
            pl.BlockSpec((tm, D_MODEL), row),
            pl.BlockSpec((tm, ATTN_WIDTH), row),
            pl.BlockSpec((tm, FOURIER_WIDTH), row),
            pl.BlockSpec((D_MODEL, tn), lambda i, j: (0, j)),
            pl.BlockSpec((D_MODEL, tn), lambda i, j: (0, j + nj)),
            pl.BlockSpec((1, tn), lambda i, j: (0, j)),
            pl.BlockSpec((1, tn), lambda i, j: (0, j + nj)),
            pl.BlockSpec((ATTN_WIDTH, tn), lambda i, j: (0, j)),
            pl.BlockSpec((FOURIER_WIDTH, tn), lambda i, j: (0, j)),
            pl.BlockSpec((tn, D_MODEL), lambda i, j: (j, 0)),
        ],
        out_specs=pl.BlockSpec((tm, D_MODEL), row),
        out_shape=jax.ShapeDtypeStruct((R, D_MODEL), F32),
        compiler_params=_cparams("parallel", "arbitrary"),
        name="mix_out",
    )(xs, meta, h, attn, four, w_gates, w_gates, b_gates, b_gates, w_a, w_f, w_o)


def _ffn_kernel(x_ref, xp_ref, xn_ref, g_ref, wg_ref, wv_ref, wc_ref, bc_ref, wd_ref,
                o_ref, h_ref, *, tm, tiles_per_batch):
    j = pl.program_id(1)
    H = CONV_HALO

    @pl.when(j == 0)
    def _():
        x = x_ref[...]
        g = g_ref[...]
        h_ref[H:H + tm, :] = (_rms(x) * g).astype(BF16)
        h_ref[0:H, :] = (_rms(xp_ref[...]) * g).astype(BF16)
        h_ref[H + tm:, :] = (_rms(xn_ref[...]) * g).astype(BF16)
        o_ref[...] = x

    h = h_ref[...]
    up_g = jnp.dot(h, wg_ref[...], preferred_element_type=F32)
    n = tm + 2 * H
    wc = wc_ref[...]
    prev = pltpu.roll(up_g, 1, 0)[H:H + tm]
    nxt = pltpu.roll(up_g, n - 1, 0)[H:H + tm]
    row = lax.broadcasted_iota(jnp.int32, (tm, 1), 0)
    seam = jnp.where(pl.program_id(0) % tiles_per_batch == tiles_per_batch - 1, tm - N_META, -1)
    prev = jnp.where(row == seam, 0.0, prev)
    nxt = jnp.where(row == seam - 1, 0.0, nxt)
    u = prev * wc[0:1] + up_g[H:H + tm] * wc[1:2] + nxt * wc[2:3] + bc_ref[...]
    val = jnp.dot(h[H:H + tm], wv_ref[...], preferred_element_type=F32)
    act = (u * jax.nn.sigmoid(u) * val).astype(BF16)
    o_ref[...] += jnp.dot(act, wd_ref[...], preferred_element_type=F32)


def _ffn(hs, gain, w_up, w_conv, b_conv, w_down, *, tm, tf, tiles_per_batch, n_tok_out=None):
    R = hs.shape[0]
    nj = D_FF // tf
    hb = tm // CONV_HALO
    nb = tiles_per_batch * hb
    row = lambda i, j: (i, 0)

    def prev_block(i, j):
        first = i % tiles_per_batch == 0
        return (jnp.where(first, (i // tiles_per_batch) * nb + nb - 1, i * hb - 1), 0)

    def next_block(i, j):
        last = i % tiles_per_batch == tiles_per_batch - 1
        return (jnp.where(last, (i // tiles_per_batch) * nb, (i + 1) * hb), 0)

    if n_tok_out is None:
        out_spec = pl.BlockSpec((tm, D_MODEL), row)
        out_shape = jax.ShapeDtypeStruct((R, D_MODEL), F32)
    else:
        out_spec = pl.BlockSpec((pl.Squeezed(), tm, D_MODEL),
                                lambda i, j: (i // tiles_per_batch, i % tiles_per_batch, 0))
        out_shape = jax.ShapeDtypeStruct((R // (tiles_per_batch * tm), n_tok_out, D_MODEL), F32)
    return pl.pallas_call(
        functools.partial(_ffn_kernel, tm=tm, tiles_per_batch=tiles_per_batch),
        grid=(R // tm, nj),
        in_specs=[
            pl.BlockSpec((tm, D_MODEL), row),
            pl.BlockSpec((CONV_HALO, D_MODEL), prev_block),
            pl.BlockSpec((CONV_HALO, D_MODEL), next_block),
            pl.BlockSpec((1, D_MODEL), lambda i, j: (0, 0)),
            pl.BlockSpec((D_MODEL, tf), lambda i, j: (0, j)),
            pl.BlockSpec((D_MODEL, tf), lambda i, j: (0, j + nj)),
            pl.BlockSpec((3, tf), lambda i, j: (0, j)),
            pl.BlockSpec((1, tf), lambda i, j: (0, j)),
            pl.BlockSpec((tf, D_MODEL), lambda i, j: (j, 0)),
        ],
        out_specs=out_spec,
        out_shape=out_shape,
        scratch_shapes=[pltpu.VMEM((tm + 2 * CONV_HALO, D_MODEL), BF16)],
        compiler_params=_cparams("parallel", "arbitrary"),
        name="ffn",
    )(hs, hs, hs, gain, w_up, w_up, w_conv, b_conv, w_down)


def _rope_tables(n_tok):
    rows = n_tok // GRID_W
    pos_r = np.concatenate([np.repeat(np.arange(rows), GRID_W), np.zeros(N_META)])
    pos_c = np.concatenate([np.tile(np.arange(GRID_W), rows), np.zeros(N_META)])
    inv_freq = 1.0 / (ROPE_THETA ** (np.arange(ROPE_PAIRS) / ROPE_PAIRS))
    ang_r = pos_r[:, None] * inv_freq[None, :]
    ang_c = pos_c[:, None] * inv_freq[None, :]
    cos_t = np.concatenate([np.cos(ang_r), np.cos(ang_r), np.cos(ang_c), np.cos(ang_c)], axis=-1)
    sin_t = np.concatenate([-np.sin(ang_r), np.sin(ang_r), -np.sin(ang_c), np.sin(ang_c)], axis=-1)
    return jnp.asarray(cos_t, F32), jnp.asarray(sin_t, F32)


def _tiles(L):
    tm = L // ROW_TILES_PER_BATCH
    return dict(tm=tm, tq=256, tk=1280, tn_mix=512, tf=512)


def kernel(x, meta_tokens, norm_mix, norm_ffn, w_in, b_gate, q_norm, k_norm, w_attn_br, w_four, w_out,
           w_up, w_conv, b_conv, w_down):
    B, n_tok, D = x.shape
    depth = w_in.shape[0]
    L = n_tok + N_META
    t = _tiles(L)
    tm = t["tm"]
    assert D == D_MODEL and n_tok % GRID_W == 0 and L % DFT_N2 == 0
    assert L % tm == 0 and tm % PACK_ROWS == 0
    R = B * L
    tiles_per_batch = L // tm

    cos_t, sin_t = _rope_tables(n_tok)
    dft = _dft_constants(L)

    meta = meta_tokens.astype(x.dtype)
    xs = x
    for i in range(depth):
        raw = i == 0
        w_qkvf = w_in[i, :, :QKVF_WIDTH].astype(BF16)
        w_gates = w_in[i, :, QKVF_WIDTH:].astype(BF16)
        q, k, v, f, h = _inproj(xs, meta, norm_mix[i][None], w_qkvf, q_norm[i][None], k_norm[i][None],
                                cos_t, sin_t, tm=tm, tiles_per_batch=tiles_per_batch, raw_input=raw)
        attn = _attention(q.reshape(B, L, ATTN_WIDTH), k.reshape(B, L, KV_WIDTH), v.reshape(B, L, KV_WIDTH),
                          tq=t["tq"], tk=t["tk"], qt=(i == 1)).reshape(R, ATTN_WIDTH)
        four = _fourier(f.reshape(B, L, FOURIER_WIDTH), dft, chunk=tm).reshape(R, FOURIER_WIDTH)
        hs = _mix(xs, meta, h, attn, four, w_gates, b_gate[i][None],
                  w_attn_br[i].astype(BF16), w_four[i].astype(BF16), w_out[i].astype(BF16),
                  tm=tm, tn=t["tn_mix"], tiles_per_batch=tiles_per_batch, raw_input=raw)
        hs = _ffn(hs, norm_ffn[i][None], w_up[i].astype(BF16), w_conv[i], b_conv[i][None],
                  w_down[i].astype(BF16), tm=tm, tf=t["tf"], tiles_per_batch=tiles_per_batch,
                  n_tok_out=n_tok if i == depth - 1 else None)
        xs = hs.reshape(B, L, D) if i < depth - 1 else hs
    return xs
```

```python
import functools
import math

import jax
import jax.numpy as jnp
import numpy as np
from jax import lax
from jax.experimental import pallas as pl
from jax.experimental.pallas import tpu as pltpu

F32 = jnp.float32
BF16 = jnp.bfloat16

D_MODEL = 2048
N_HEADS = 8
N_KV_HEADS = 2
HEAD_DIM = 128
GQA_GROUP = N_HEADS // N_KV_HEADS
ATTN_WIDTH = N_HEADS * HEAD_DIM
KV_WIDTH = N_KV_HEADS * HEAD_DIM
N_FOURIER_GROUPS = 8
FOURIER_GROUP_CH = 128
FOURIER_WIDTH = N_FOURIER_GROUPS * FOURIER_GROUP_CH
QKVF_WIDTH = ATTN_WIDTH + 2 * KV_WIDTH + FOURIER_WIDTH
OFF_K = ATTN_WIDTH
OFF_V = OFF_K + KV_WIDTH
OFF_F = OFF_V + KV_WIDTH
D_FF = 5632
N_META = 16
GRID_W = 64
NORM_EPS = 1e-6
ROPE_THETA = 10000.0
ROPE_PAIRS = HEAD_DIM // 4

DFT_N2 = 80
ROW_TILES_PER_BATCH = 25
PACK_ROWS = 16
LANES = 128
CONV_HALO = PACK_ROWS
MASK_VALUE = -1e30
VMEM_LIMIT = 56 * 1024 * 1024


def _cparams(*sem):
    return pltpu.CompilerParams(dimension_semantics=sem, vmem_limit_bytes=VMEM_LIMIT)


def _rms(x):
    return x * lax.rsqrt(jnp.mean(x * x, axis=-1, keepdims=True) + NORM_EPS)


def _round_up(n, m):
    return -(-n // m) * m


def _meta_rows(x_ref, meta_ref, tiles_per_batch):
    split = x_ref.shape[0] - N_META
    last = pl.program_id(0) % tiles_per_batch == tiles_per_batch - 1
    return jnp.where(last, meta_ref[...], x_ref[split:, :])


def _inproj_kernel(x_ref, meta_ref, g_ref, w_ref, qn_ref, kn_ref, cos_ref, sin_ref,
                   q_ref, k_ref, v_ref, f_ref, h_ref, *, raw_input, tiles_per_batch):
    g = g_ref[...]
    if raw_input:
        split = x_ref.shape[0] - N_META
        h_ref[:split, :] = (_rms(x_ref[:split, :]) * g).astype(BF16)
        h_ref[split:, :] = (_rms(_meta_rows(x_ref, meta_ref, tiles_per_batch)) * g).astype(BF16)
    else:
        h_ref[...] = (_rms(x_ref[...]) * g).astype(BF16)
    proj = jnp.dot(h_ref[...], w_ref[...], preferred_element_type=F32)
    cos = cos_ref[...]
    sin = sin_ref[...]
    lane = lax.broadcasted_iota(jnp.int32, cos.shape, 1)
    first_half = (lane % (HEAD_DIM // 2)) < ROPE_PAIRS

    def norm_rope(t, gain):
        t = _rms(t) * gain
        partner = jnp.where(first_half,
                            pltpu.roll(t, HEAD_DIM - ROPE_PAIRS, 1),
                            pltpu.roll(t, ROPE_PAIRS, 1))
        return t * cos + partner * sin

    scale = math.log2(math.e) / math.sqrt(HEAD_DIM)
    for hh in range(N_HEADS):
        c = hh * HEAD_DIM
        q_ref[:, c:c + HEAD_DIM] = (norm_rope(proj[:, c:c + HEAD_DIM], qn_ref[...]) * scale).astype(BF16)
    for hh in range(N_KV_HEADS):
        c = hh * HEAD_DIM
        k_ref[:, c:c + HEAD_DIM] = norm_rope(proj[:, OFF_K + c:OFF_K + c + HEAD_DIM], kn_ref[...]).astype(BF16)
    v_ref[...] = proj[:, OFF_V:OFF_F].astype(BF16)
    f_ref[...] = proj[:, OFF_F:].astype(BF16)


def _inproj(xs, meta, gain, w_qkvf, qn, kn, cos_t, sin_t, *, tm, tiles_per_batch, raw_input):
    R = xs.shape[0] * tiles_per_batch * tm
    row = lambda i: (i, 0)
    fixed = lambda i: (0, 0)
    tab = lambda i: (i % tiles_per_batch, 0)
    widths = (ATTN_WIDTH, KV_WIDTH, KV_WIDTH, FOURIER_WIDTH, D_MODEL)
    return pl.pallas_call(
        functools.partial(_inproj_kernel, raw_input=raw_input, tiles_per_batch=tiles_per_batch),
        grid=(R // tm,),
        in_specs=[
            pl.BlockSpec((pl.Squeezed(), tm, D_MODEL), lambda i: (i // tiles_per_batch, i % tiles_per_batch, 0)),
            pl.BlockSpec((N_META, D_MODEL), fixed),
            pl.BlockSpec((1, D_MODEL), fixed),
            pl.BlockSpec((D_MODEL, QKVF_WIDTH), fixed),
            pl.BlockSpec((1, HEAD_DIM), fixed),
            pl.BlockSpec((1, HEAD_DIM), fixed),
            pl.BlockSpec((tm, HEAD_DIM), tab),
            pl.BlockSpec((tm, HEAD_DIM), tab),
        ],
        out_specs=[pl.BlockSpec((tm, w), row) for w in widths],
        out_shape=[jax.ShapeDtypeStruct((R, w), BF16) for w in widths],
        compiler_params=_cparams("parallel"),
        name="inproj",
    )(xs, meta, gain, w_qkvf, qn, kn, cos_t, sin_t)


def _attn_kernel(q_ref, k_ref, v_ref, o_ref, vt_ref, qt_ref, s_ref, mt_ref, m_ref, l_ref, acc_ref,
                 *, L, tk, n_full, tail, tail_pad, qt):
    has_tail = tail > 0
    n_stage = n_full + (1 if has_tail else 0)
    off = 1 if has_tail else 0
    tail_start = n_full * tk

    def padded_tail(ref):
        rows = ref[0, tail_start:L, :]
        if tail_pad == tail:
            return rows
        return jnp.concatenate([rows, jnp.zeros((tail_pad - tail, HEAD_DIM), rows.dtype)], axis=0)

    @pl.when(pl.program_id(2) == 0)
    def _():
        for j in range(n_full):
            vt_ref[:, j * tk:(j + 1) * tk] = v_ref[0, j * tk:(j + 1) * tk, :].astype(F32).T.astype(BF16)
        if has_tail:
            vt_ref[:, tail_start:tail_start + tail_pad] = padded_tail(v_ref).astype(F32).T.astype(BF16)

    m_ref[...] = jnp.full(m_ref.shape, MASK_VALUE, F32)
    l_ref[...] = jnp.zeros(l_ref.shape, F32)
    acc_ref[...] = jnp.zeros(acc_ref.shape, F32)
    if qt:
        for g in range(GQA_GROUP):
            qt_ref[g] = q_ref[0, :, g * HEAD_DIM:(g + 1) * HEAD_DIM].astype(F32).T.astype(BF16)

    def is_tail(t):
        return has_tail and isinstance(t, int) and t == 0

    def tile_start(t):
        if isinstance(t, int):
            return tail_start if is_tail(t) else (t - off) * tk
        return pl.multiple_of((t - off) * tk, tk)

    def scores(t, slot):
        rows = tail_pad if is_tail(t) else tk
        kj = padded_tail(k_ref) if is_tail(t) else k_ref[0, pl.ds(tile_start(t), tk), :]
        for g in range(GQA_GROUP):
            c = g * HEAD_DIM
            if qt:
                st = jnp.dot(kj, qt_ref[g], preferred_element_type=F32)
            else:
                st = lax.dot_general(kj, q_ref[0, :, c:c + HEAD_DIM], (((1,), (1,)), ((), ())),
                                     preferred_element_type=F32)
            if is_tail(t) and tail_pad != tail:
                key = lax.broadcasted_iota(jnp.int32, st.shape, 0)
                st = jnp.where(key < tail, st, MASK_VALUE)
            s_ref[slot, g, 0:rows, :] = st
            mt_ref[slot, g] = jnp.max(st, axis=0, keepdims=True)

    def softmax_pv(t, slot):
        rows = tail_pad if is_tail(t) else tk
        vtj = vt_ref[:, pl.ds(tile_start(t), rows)]
        for g in range(GQA_GROUP):
            st = s_ref[slot, g, 0:rows, :]
            m_prev = m_ref[g]
            m_new = jnp.maximum(m_prev, mt_ref[slot, g])
            alpha = jnp.exp2(m_prev - m_new)
            p = jnp.exp2(st - m_new)
            l_ref[g] = alpha * l_ref[g] + jnp.sum(p, axis=0, keepdims=True)
            acc_ref[g] = alpha * acc_ref[g] + jnp.dot(vtj, p.astype(BF16), preferred_element_type=F32)
            m_ref[g] = m_new

    def stage(t, slot):
        if not isinstance(t, int) or t + 1 < n_stage:
            scores(t + 1, 1 - slot)
        softmax_pv(t, slot)

    scores(0, 0)
    stage(0, 0)
    n_pairs = max(0, (n_stage - 2) // 2)
    if n_pairs > 0:
        def body(p, carry):
            stage(1 + 2 * p, 1)
            stage(2 + 2 * p, 0)
            return carry
        lax.fori_loop(0, n_pairs, body, 0)
    for t in range(1 + 2 * n_pairs, n_stage):
        stage(t, t % 2)
    for g in range(GQA_GROUP):
        c = g * HEAD_DIM
        o_ref[0, :, c:c + HEAD_DIM] = (acc_ref[g] / l_ref[g]).T.astype(BF16)


def _attention(q, k, v, *, tq, tk, qt, skew):
    B, L, _ = q.shape
    gw = GQA_GROUP * HEAD_DIM
    n_full = L // tk
    tail = L - n_full * tk
    tail_pad = _round_up(tail, LANES)
    kern = functools.partial(_attn_kernel, L=L, tk=tk, n_full=n_full, tail=tail, tail_pad=tail_pad, qt=qt)
    kv_spec = pl.BlockSpec((1, L, HEAD_DIM), lambda b, h, i: (b, 0, h))
    return pl.pallas_call(
        kern,
        grid=(B, N_KV_HEADS, pl.cdiv(L, tq)),
        in_specs=[pl.BlockSpec((1, tq, gw), lambda b, h, i: (b, i, h)), kv_spec, kv_spec],
        out_specs=pl.BlockSpec((1, tq, gw), lambda b, h, i: (b, i, h)),
        out_shape=jax.ShapeDtypeStruct((B, L, ATTN_WIDTH), BF16),
        scratch_shapes=[
            pltpu.VMEM((HEAD_DIM, n_full * tk + tail_pad), BF16),
            pltpu.VMEM((GQA_GROUP, HEAD_DIM, tq), BF16),
            pltpu.VMEM((2, GQA_GROUP, max(tk, tail_pad) + skew, tq), F32),
            pltpu.VMEM((2, GQA_GROUP, 1, tq), F32),
            pltpu.VMEM((GQA_GROUP, 1, tq), F32),
            pltpu.VMEM((GQA_GROUP, 1, tq), F32),
            pltpu.VMEM((GQA_GROUP, HEAD_DIM, tq), F32),
        ],
        compiler_params=_cparams("parallel", "parallel", "arbitrary"),
        name="attention",
    )(q, k, v)


def _fourier_kernel(f_ref, cs_ref, f1r_ref, f1i_ref, g_ref, o_ref, wr_ref, wi_ref, os_ref,
                    *, L, n1, n1p, chunk):
    n_chunks = L // chunk
    last = L - chunk
    split = chunk - N_META

    def channel_dft_rows(r0, rows):
        return jnp.dot(f_ref[0, pl.ds(r0, rows), :], cs_ref[...], preferred_element_type=F32)

    def channel_dft(c, carry):
        r0 = pl.multiple_of(c * chunk, PACK_ROWS)
        z = channel_dft_rows(r0, chunk)
        wr_ref[pl.ds(r0 + N_META, chunk), :] = z[:, :FOURIER_GROUP_CH]
        wi_ref[pl.ds(r0 + N_META, chunk), :] = z[:, FOURIER_GROUP_CH:]
        return carry

    lax.fori_loop(0, n_chunks - 1, channel_dft, 0, unroll=4 if (n_chunks - 1) % 4 == 0 else 1)
    z = channel_dft_rows(last, chunk)
    wr_ref[last + N_META:L, :] = z[:split, :FOURIER_GROUP_CH]
    wi_ref[last + N_META:L, :] = z[:split, FOURIER_GROUP_CH:]
    wr_ref[0:N_META, :] = z[split:, :FOURIER_GROUP_CH]
    wi_ref[0:N_META, :] = z[split:, FOURIER_GROUP_CH:]
    if n1p > n1:
        zeros = jnp.zeros(((n1p - n1) * DFT_N2, FOURIER_GROUP_CH), F32)
        wr_ref[L:, :] = zeros
        wi_ref[L:, :] = zeros

    def row_dft(n2, carry):
        rows = pl.ds(n2, n1p, stride=DFT_N2)
        zr = wr_ref[rows, :].astype(BF16)
        zi = wi_ref[rows, :].astype(BF16)
        a = (jnp.dot(f1r_ref[...], zr, preferred_element_type=F32)
             + jnp.dot(f1i_ref[...], zi, preferred_element_type=F32))
        wr_ref[rows, :] = a[:n1p]
        wi_ref[rows, :] = a[n1p:]
        return carry

    lax.fori_loop(0, DFT_N2, row_dft, 0, unroll=16)

    def col_dft(k1, carry):
        r0 = pl.multiple_of(k1 * DFT_N2, DFT_N2)
        a = jnp.concatenate([wr_ref[pl.ds(r0, DFT_N2), :].astype(BF16),
                             wi_ref[pl.ds(r0, DFT_N2), :].astype(BF16)], axis=0)
        os_ref[pl.ds(k1, DFT_N2, stride=n1), :] = jnp.dot(g_ref[k1], a, preferred_element_type=F32)
        return carry

    lax.fori_loop(0, n1, col_dft, 0, unroll=(41 if n1 % 41 == 0 else 5) if n1 % 5 == 0 else 1)

    def emit(c, carry):
        r0 = pl.multiple_of(c * chunk, PACK_ROWS)
        o_ref[0, pl.ds(r0, chunk), :] = os_ref[pl.ds(r0 + N_META, chunk), :].astype(BF16)
        return carry

    lax.fori_loop(0, n_chunks - 1, emit, 0)
    o_ref[0, last:L - N_META, :] = os_ref[last + N_META:L, :].astype(BF16)
    o_ref[0, L - N_META:L, :] = os_ref[0:N_META, :].astype(BF16)


def _fourier(f, consts, *, chunk):
    B, L, _ = f.shape
    n1 = L // DFT_N2
    n1p = _round_up(n1, 8)
    cs, f1r, f1i, g = consts
    slab = pl.BlockSpec((1, L, FOURIER_GROUP_CH), lambda b, c: (b, 0, c))
    fixed2 = lambda b, c: (0, 0)
    return pl.pallas_call(
        functools.partial(_fourier_kernel, L=L, n1=n1, n1p=n1p, chunk=chunk),
        grid=(B, N_FOURIER_GROUPS),
        in_specs=[
            slab,
            pl.BlockSpec((FOURIER_GROUP_CH, 2 * FOURIER_GROUP_CH), fixed2),
            pl.BlockSpec((2 * n1p, n1p), fixed2),
            pl.BlockSpec((2 * n1p, n1p), fixed2),
            pl.BlockSpec((n1, DFT_N2, 2 * DFT_N2), lambda b, c: (0, 0, 0), pipeline_mode=pl.Buffered(1)),
        ],
        out_specs=slab,
        out_shape=jax.ShapeDtypeStruct((B, L, FOURIER_WIDTH), BF16),
        scratch_shapes=[pltpu.VMEM((n1p * DFT_N2, FOURIER_GROUP_CH), F32)] * 2
        + [pltpu.VMEM((L, FOURIER_GROUP_CH), F32)],
        compiler_params=_cparams("parallel", "parallel"),
        name="fourier",
    )(f, cs, f1r, f1i, g)


def _dft_constants(L):
    n1 = L // DFT_N2
    n1p = _round_up(n1, 8)
    k1 = np.arange(n1)
    ang1 = 2.0 * np.pi * np.outer(k1, k1) / n1
    c1, s1 = np.cos(ang1), np.sin(ang1)
    f1r = np.zeros((2 * n1p, n1p))
    f1i = np.zeros((2 * n1p, n1p))
    f1r[:n1, :n1] = c1
    f1r[n1p:n1p + n1, :n1] = -s1
    f1i[:n1, :n1] = s1
    f1i[n1p:n1p + n1, :n1] = c1
    n2 = np.arange(DFT_N2)
    k = k1[:, None, None] + n1 * n2[None, :, None]
    ang = 2.0 * np.pi * ((k * n2[None, None, :]) % L) / L
    g = np.concatenate([np.cos(ang), np.sin(ang)], axis=2) / math.sqrt(L)
    c = np.arange(FOURIER_GROUP_CH)
    angc = 2.0 * np.pi * np.outer(c, c) / FOURIER_GROUP_CH
    cs = np.concatenate([np.cos(angc), -np.sin(angc)], axis=1) / math.sqrt(FOURIER_GROUP_CH)
    return (jnp.asarray(cs, BF16), jnp.asarray(f1r, BF16), jnp.asarray(f1i, BF16), jnp.asarray(g, BF16))


def _mix_kernel(x_ref, meta_ref, h_ref, attn_ref, four_ref, wga_ref, wgf_ref, bga_ref, bgf_ref,
                wa_ref, wf_ref, wo_ref, o_ref, *, raw_input, tiles_per_batch):
    @pl.when(pl.program_id(1) == 0)
    def _():
        if raw_input:
            split = x_ref.shape[0] - N_META
            o_ref[:split, :] = x_ref[:split, :]
            o_ref[split:, :] = _meta_rows(x_ref, meta_ref, tiles_per_batch)
        else:
            o_ref[...] = x_ref[...]

    h = h_ref[...]
    g_attn = jax.nn.sigmoid(jnp.dot(h, wga_ref[...], preferred_element_type=F32) + bga_ref[...])
    g_four = jax.nn.sigmoid(jnp.dot(h, wgf_ref[...], preferred_element_type=F32) + bgf_ref[...])
    a_br = jnp.dot(attn_ref[...], wa_ref[...], preferred_element_type=F32)
    s_br = jnp.dot(four_ref[...], wf_ref[...], preferred_element_type=F32)
    merged = (g_attn * a_br + g_four * s_br).astype(BF16)
    o_ref[...] += jnp.dot(merged, wo_ref[...], preferred_element_type=F32)


def _mix(xs, meta, h, attn, four, w_gates, b_gates, w_a, w_f, w_o, *, tm, tn, tiles_per_batch, raw_input):
    R = h.shape[0]
    nj = D_MODEL // tn
    row = lambda i, j: (i, 0)
    return pl.pallas_call(
        functools.partial(_mix_kernel, raw_input=raw_input, tiles_per_batch=tiles_per_batch),
        grid=(R // tm, nj),
        in_specs=[
            pl.BlockSpec((pl.Squeezed(), tm, D_MODEL),
                         lambda i, j: (i // tiles_per_batch, i % tiles_per_batch, 0)),
            pl.BlockSpec((N_META, D_MODEL), lambda i, j: (0, 0)),
            pl.BlockSpec((tm, D_MODEL), row),
            pl.BlockSpec((tm, ATTN_WIDTH), row),
            pl.BlockSpec((tm, FOURIER_WIDTH), row),
            pl.BlockSpec((D_MODEL, tn), lambda i, j: (0, j)),
            pl.BlockSpec((D_MODEL, tn), lambda i, j: (0, j + nj)),
            pl.BlockSpec((1, tn), lambda i, j: (0, j)),
            pl.BlockSpec((1, tn), lambda i, j: (0, j + nj)),
            pl.BlockSpec((ATTN_WIDTH, tn), lambda i, j: (0, j)),
            pl.BlockSpec((FOURIER_WIDTH, tn), lambda i, j: (0, j)),
            pl.BlockSpec((tn, D_MODEL), lambda i, j: (j, 0)),
        ],
        out_specs=pl.BlockSpec((tm, D_MODEL), row),
        out_shape=jax.ShapeDtypeStruct((R, D_MODEL), F32),
        compiler_params=_cparams("parallel", "arbitrary"),
        name="mix_out",
    )(xs, meta, h, attn, four, w_gates, w_gates, b_gates, b_gates, w_a, w_f, w_o)


def _ffn_kernel(x_ref, xp_ref, xn_ref, g_ref, wg_ref, wv_ref, wc_ref, bc_ref, wd_ref,
                o_ref, h_ref, *, tm, tiles_per_batch):
    j = pl.program_id(1)
    H = CONV_HALO

    @pl.when(j == 0)
    def _():
        x = x_ref[...]
        g = g_ref[...]
        h_ref[H:H + tm, :] = (_rms(x) * g).astype(BF16)
        h_ref[0:H, :] = (_rms(xp_ref[...]) * g).astype(BF16)
        h_ref[H + tm:, :] = (_rms(xn_ref[...]) * g).astype(BF16)
        o_ref[...] = x

    h = h_ref[...]
    up_g = jnp.dot(h, wg_ref[...], preferred_element_type=F32)
    n = tm + 2 * H
    wc = wc_ref[...]
    prev = pltpu.roll(up_g, 1, 0)[H:H + tm]
    nxt = pltpu.roll(up_g, n - 1, 0)[H:H + tm]
    row = lax.broadcasted_iota(jnp.int32, (tm, 1), 0)
    seam = jnp.where(pl.program_id(0) % tiles_per_batch == tiles_per_batch - 1, tm - N_META, -1)
    prev = jnp.where(row == seam, 0.0, prev)
    nxt = jnp.where(row == seam - 1, 0.0, nxt)
    u = prev * wc[0:1] + up_g[H:H + tm] * wc[1:2] + nxt * wc[2:3] + bc_ref[...]
    val = jnp.dot(h[H:H + tm], wv_ref[...], preferred_element_type=F32)
    act = (u * jax.nn.sigmoid(u) * val).astype(BF16)
    o_ref[...] += jnp.dot(act, wd_ref[...], preferred_element_type=F32)


def _ffn(hs, gain, w_up, w_conv, b_conv, w_down, *, tm, tf, tiles_per_batch, n_tok_out=None):
    R = hs.shape[0]
    nj = D_FF // tf
    hb = tm // CONV_HALO
    nb = tiles_per_batch * hb
    row = lambda i, j: (i, 0)

    def prev_block(i, j):
        first = i % tiles_per_batch == 0
        return (jnp.where(first, (i // tiles_per_batch) * nb + nb - 1, i * hb - 1), 0)

    def next_block(i, j):
        last = i % tiles_per_batch == tiles_per_batch - 1
        return (jnp.where(last, (i // tiles_per_batch) * nb, (i + 1) * hb), 0)

    if n_tok_out is None:
        out_spec = pl.BlockSpec((tm, D_MODEL), row)
        out_shape = jax.ShapeDtypeStruct((R, D_MODEL), F32)
    else:
        out_spec = pl.BlockSpec((pl.Squeezed(), tm, D_MODEL),
                                lambda i, j: (i // tiles_per_batch, i % tiles_per_batch, 0))
        out_shape = jax.ShapeDtypeStruct((R // (tiles_per_batch * tm), n_tok_out, D_MODEL), F32)
    return pl.pallas_call(
        functools.partial(_ffn_kernel, tm=tm, tiles_per_batch=tiles_per_batch),
        grid=(R // tm, nj),
        in_specs=[
            pl.BlockSpec((tm, D_MODEL), row),
            pl.BlockSpec((CONV_HALO, D_MODEL), prev_block),
            pl.BlockSpec((CONV_HALO, D_MODEL), next_block),
            pl.BlockSpec((1, D_MODEL), lambda i, j: (0, 0)),
            pl.BlockSpec((D_MODEL, tf), lambda i, j: (0, j)),
            pl.BlockSpec((D_MODEL, tf), lambda i, j: (0, j + nj)),
            pl.BlockSpec((3, tf), lambda i, j: (0, j)),
            pl.BlockSpec((1, tf), lambda i, j: (0, j)),
            pl.BlockSpec((tf, D_MODEL), lambda i, j: (j, 0)),
        ],
        out_specs=out_spec,
        out_shape=out_shape,
        scratch_shapes=[pltpu.VMEM((tm + 2 * CONV_HALO, D_MODEL), BF16)],
        compiler_params=_cparams("parallel", "arbitrary"),
        name="ffn",
    )(hs, hs, hs, gain, w_up, w_up, w_conv, b_conv, w_down)


def _rope_tables(n_tok):
    rows = n_tok // GRID_W
    pos_r = np.concatenate([np.repeat(np.arange(rows), GRID_W), np.zeros(N_META)])
    pos_c = np.concatenate([np.tile(np.arange(GRID_W), rows), np.zeros(N_META)])
    inv_freq = 1.0 / (ROPE_THETA ** (np.arange(ROPE_PAIRS) / ROPE_PAIRS))
    ang_r = pos_r[:, None] * inv_freq[None, :]
    ang_c = pos_c[:, None] * inv_freq[None, :]
    cos_t = np.concatenate([np.cos(ang_r), np.cos(ang_r), np.cos(ang_c), np.cos(ang_c)], axis=-1)
    sin_t = np.concatenate([-np.sin(ang_r), np.sin(ang_r), -np.sin(ang_c), np.sin(ang_c)], axis=-1)
    return jnp.asarray(cos_t, F32), jnp.asarray(sin_t, F32)


def _tiles(L):
    tm = L // ROW_TILES_PER_BATCH
    return dict(tm=tm, tq=256, tk=1280, tn_mix=512, tf=512)


def kernel(x, meta_tokens, norm_mix, norm_ffn, w_in, b_gate, q_norm, k_norm, w_attn_br, w_four, w_out,
           w_up, w_conv, b_conv, w_down):
    B, n_tok, D = x.shape
    depth = w_in.shape[0]
    L = n_tok + N_META
    t = _tiles(L)
    tm = t["tm"]
    assert D == D_MODEL and n_tok % GRID_W == 0 and L % DFT_N2 == 0
    assert L % tm == 0 and tm % PACK_ROWS == 0
    R = B * L
    tiles_per_batch = L // tm

    cos_t, sin_t = _rope_tables(n_tok)
    dft = _dft_constants(L)

    meta = meta_tokens.astype(x.dtype)
    xs = x
    for i in range(depth):
        raw = i == 0
        w_qkvf = w_in[i, :, :QKVF_WIDTH].astype(BF16)
        w_gates = w_in[i, :, QKVF_WIDTH:].astype(BF16)
        q, k, v, f, h = _inproj(xs, meta, norm_mix[i][None], w_qkvf, q_norm[i][None], k_norm[i][None],
                                cos_t, sin_t, tm=tm, tiles_per_batch=tiles_per_batch, raw_input=raw)
        attn = _attention(q.reshape(B, L, ATTN_WIDTH), k.reshape(B, L, KV_WIDTH), v.reshape(B, L, KV_WIDTH),
                          tq=t["tq"], tk=t["tk"], qt=True, skew=8 * i).reshape(R, ATTN_WIDTH)
        four = _fourier(f.reshape(B, L, FOURIER_WIDTH), dft, chunk=tm).reshape(R, FOURIER_WIDTH)
        hs = _mix(xs, meta, h, attn, four, w_gates, b_gate[i][None],
                  w_attn_br[i].astype(BF16), w_four[i].astype(BF16), w_out[i].astype(BF16),
                  tm=tm, tn=t["tn_mix"], tiles_per_batch=tiles_per_batch, raw_input=raw)
        hs = _ffn(hs, norm_ffn[i][None], w_up[i].astype(BF16), w_conv[i], b_conv[i][None],
                  w_down[i].astype(BF16), tm=tm, tf=t["tf"], tiles_per_batch=tiles_per_batch,
                  n_tok_out=n_tok if i == depth - 1 else None)
        xs = hs.reshape(B, L, D) if i < depth - 1 else hs
    return xs
```

```python
import functools
import math

import jax
import jax.numpy as jnp
import numpy as np
from jax import lax
from jax.experimental import pallas as pl
from jax.experimental.pallas import tpu as pltpu

F32 = jnp.float32
BF16 = jnp.bfloat16

D_MODEL = 2048
N_HEADS = 8
N_KV_HEADS = 2
HEAD_DIM = 128
GQA_GROUP = N_HEADS // N_KV_HEADS
ATTN_WIDTH = N_HEADS * HEAD_DIM
KV_WIDTH = N_KV_HEADS * HEAD_DIM
N_FOURIER_GROUPS = 8
FOURIER_GROUP_CH = 128
FOURIER_WIDTH = N_FOURIER_GROUPS * FOURIER_GROUP_CH
QKVF_WIDTH = ATTN_WIDTH + 2 * KV_WIDTH + FOURIER_WIDTH
OFF_K = ATTN_WIDTH
OFF_V = OFF_K + KV_WIDTH
OFF_F = OFF_V + KV_WIDTH
D_FF = 5632
N_META = 16
GRID_W = 64
NORM_EPS = 1e-6
ROPE_THETA = 10000.0
ROPE_PAIRS = HEAD_DIM // 4

DFT_N2 = 80
ROW_TILES_PER_BATCH = 25
PACK_ROWS = 16
LANES = 128
CONV_HALO = PACK_ROWS
MASK_VALUE = -1e30
VMEM_LIMIT = 56 * 1024 * 1024


def _cparams(*sem):
    return pltpu.CompilerParams(dimension_semantics=sem, vmem_limit_bytes=VMEM_LIMIT)


def _rms(x):
    return x * lax.rsqrt(jnp.mean(x * x, axis=-1, keepdims=True) + NORM_EPS)


def _round_up(n, m):
    return -(-n // m) * m


def _meta_rows(x_ref, meta_ref, tiles_per_batch):
    split = x_ref.shape[0] - N_META
    last = pl.program_id(0) % tiles_per_batch == tiles_per_batch - 1
    return jnp.where(last, meta_ref[...], x_ref[split:, :])


def _inproj_kernel(x_ref, meta_ref, g_ref, w_ref, qn_ref, kn_ref, cos_ref, sin_ref,
                   q_ref, k_ref, v_ref, f_ref, h_ref, *, raw_input, tiles_per_batch):
    g = g_ref[...]
    if raw_input:
        split = x_ref.shape[0] - N_META
        h_ref[:split, :] = (_rms(x_ref[:split, :]) * g).astype(BF16)
        h_ref[split:, :] = (_rms(_meta_rows(x_ref, meta_ref, tiles_per_batch)) * g).astype(BF16)
    else:
        h_ref[...] = (_rms(x_ref[...]) * g).astype(BF16)
    proj = jnp.dot(h_ref[...], w_ref[...], preferred_element_type=F32)
    cos = cos_ref[...]
    sin = sin_ref[...]
    lane = lax.broadcasted_iota(jnp.int32, cos.shape, 1)
    first_half = (lane % (HEAD_DIM // 2)) < ROPE_PAIRS

    def norm_rope(t, gain):
        t = _rms(t) * gain
        partner = jnp.where(first_half,
                            pltpu.roll(t, HEAD_DIM - ROPE_PAIRS, 1),
                            pltpu.roll(t, ROPE_PAIRS, 1))
        return t * cos + partner * sin

    scale = math.log2(math.e) / math.sqrt(HEAD_DIM)
    for hh in range(N_HEADS):
        c = hh * HEAD_DIM
        q_ref[:, c:c + HEAD_DIM] = (norm_rope(proj[:, c:c + HEAD_DIM], qn_ref[...]) * scale).astype(BF16)
    for hh in range(N_KV_HEADS):
        c = hh * HEAD_DIM
        k_ref[:, c:c + HEAD_DIM] = norm_rope(proj[:, OFF_K + c:OFF_K + c + HEAD_DIM], kn_ref[...]).astype(BF16)
    v_ref[...] = proj[:, OFF_V:OFF_F].astype(BF16)
    f_ref[...] = proj[:, OFF_F:].astype(BF16)


def _inproj(xs, meta, gain, w_qkvf, qn, kn, cos_t, sin_t, *, tm, tiles_per_batch, raw_input):
    R = xs.shape[0] * tiles_per_batch * tm
    row = lambda i: (i, 0)
    fixed = lambda i: (0, 0)
    tab = lambda i: (i % tiles_per_batch, 0)
    widths = (ATTN_WIDTH, KV_WIDTH, KV_WIDTH, FOURIER_WIDTH, D_MODEL)
    return pl.pallas_call(
        functools.partial(_inproj_kernel, raw_input=raw_input, tiles_per_batch=tiles_per_batch),
        grid=(R // tm,),
        in_specs=[
            pl.BlockSpec((pl.Squeezed(), tm, D_MODEL), lambda i: (i // tiles_per_batch, i % tiles_per_batch, 0)),
            pl.BlockSpec((N_META, D_MODEL), fixed),
            pl.BlockSpec((1, D_MODEL), fixed),
            pl.BlockSpec((D_MODEL, QKVF_WIDTH), fixed),
            pl.BlockSpec((1, HEAD_DIM), fixed),
            pl.BlockSpec((1, HEAD_DIM), fixed),
            pl.BlockSpec((tm, HEAD_DIM), tab),
            pl.BlockSpec((tm, HEAD_DIM), tab),
        ],
        out_specs=[pl.BlockSpec((tm, w), row) for w in widths],
        out_shape=[jax.ShapeDtypeStruct((R, w), BF16) for w in widths],
        compiler_params=_cparams("parallel"),
        name="inproj",
    )(xs, meta, gain, w_qkvf, qn, kn, cos_t, sin_t)


def _attn_kernel(q_ref, k_ref, v_ref, o_ref, vt_ref, qt_ref, s_ref, mt_ref, m_ref, l_ref, acc_ref,
                 *, L, tk, n_full, tail, tail_pad):
    has_tail = tail > 0
    n_stage = n_full + (1 if has_tail else 0)
    off = 1 if has_tail else 0
    tail_start = n_full * tk

    def padded_tail(ref):
        rows = ref[0, tail_start:L, :]
        if tail_pad == tail:
            return rows
        return jnp.concatenate([rows, jnp.zeros((tail_pad - tail, HEAD_DIM), rows.dtype)], axis=0)

    @pl.when(pl.program_id(2) == 0)
    def _():
        for j in range(n_full):
            vt_ref[:, j * tk:(j + 1) * tk] = v_ref[0, j * tk:(j + 1) * tk, :].astype(F32).T.astype(BF16)
        if has_tail:
            vt_ref[:, tail_start:tail_start + tail_pad] = padded_tail(v_ref).astype(F32).T.astype(BF16)

    m_ref[...] = jnp.full(m_ref.shape, MASK_VALUE, F32)
    l_ref[...] = jnp.zeros(l_ref.shape, F32)
    acc_ref[...] = jnp.zeros(acc_ref.shape, F32)
    for g in range(GQA_GROUP):
        qt_ref[g] = q_ref[0, :, g * HEAD_DIM:(g + 1) * HEAD_DIM].astype(F32).T.astype(BF16)

    def is_tail(t):
        return has_tail and isinstance(t, int) and t == 0

    def tile_start(t):
        if isinstance(t, int):
            return tail_start if is_tail(t) else (t - off) * tk
        return pl.multiple_of((t - off) * tk, tk)

    def scores(t, slot):
        rows = tail_pad if is_tail(t) else tk
        kj = padded_tail(k_ref) if is_tail(t) else k_ref[0, pl.ds(tile_start(t), tk), :]
        for g in range(GQA_GROUP):
            st = jnp.dot(kj, qt_ref[g], preferred_element_type=F32)
            if is_tail(t) and tail_pad != tail:
                key = lax.broadcasted_iota(jnp.int32, st.shape, 0)
                st = jnp.where(key < tail, st, MASK_VALUE)
            s_ref[slot, g, 0:rows, :] = st
            mt_ref[slot, g] = jnp.max(st, axis=0, keepdims=True)

    def softmax_pv(t, slot):
        rows = tail_pad if is_tail(t) else tk
        vtj = vt_ref[:, pl.ds(tile_start(t), rows)]
        for g in range(GQA_GROUP):
            st = s_ref[slot, g, 0:rows, :]
            m_prev = m_ref[g]
            m_new = jnp.maximum(m_prev, mt_ref[slot, g])
            alpha = jnp.exp2(m_prev - m_new)
            p = jnp.exp2(st - m_new)
            l_ref[g] = alpha * l_ref[g] + jnp.sum(p, axis=0, keepdims=True)
            acc_ref[g] = alpha * acc_ref[g] + jnp.dot(vtj, p.astype(BF16), preferred_element_type=F32)
            m_ref[g] = m_new

    def stage(t, slot):
        if not isinstance(t, int) or t + 1 < n_stage:
            scores(t + 1, 1 - slot)
        softmax_pv(t, slot)

    scores(0, 0)
    stage(0, 0)
    n_pairs = max(0, (n_stage - 2) // 2)
    if n_pairs > 0:
        def body(p, carry):
            stage(1 + 2 * p, 1)
            stage(2 + 2 * p, 0)
            return carry
        lax.fori_loop(0, n_pairs, body, 0)
    for t in range(1 + 2 * n_pairs, n_stage):
        stage(t, t % 2)
    for g in range(GQA_GROUP):
        c = g * HEAD_DIM
        o_ref[0, :, c:c + HEAD_DIM] = (acc_ref[g] / l_ref[g]).T.astype(BF16)


def _attention(q, k, v, *, tq, tk):
    B, L, _ = q.shape
    gw = GQA_GROUP * HEAD_DIM
    n_full = L // tk
    tail = L - n_full * tk
    tail_pad = _round_up(tail, LANES)
    kern = functools.partial(_attn_kernel, L=L, tk=tk, n_full=n_full, tail=tail, tail_pad=tail_pad)
    kv_spec = pl.BlockSpec((1, L, HEAD_DIM), lambda b, h, i: (b, 0, h))
    return pl.pallas_call(
        kern,
        grid=(B, N_KV_HEADS, pl.cdiv(L, tq)),
        in_specs=[pl.BlockSpec((1, tq, gw), lambda b, h, i: (b, i, h)), kv_spec, kv_spec],
        out_specs=pl.BlockSpec((1, tq, gw), lambda b, h, i: (b, i, h)),
        out_shape=jax.ShapeDtypeStruct((B, L, ATTN_WIDTH), BF16),
        scratch_shapes=[
            pltpu.VMEM((HEAD_DIM, n_full * tk + tail_pad), BF16),
            pltpu.VMEM((GQA_GROUP, HEAD_DIM, tq), BF16),
            pltpu.VMEM((2, GQA_GROUP, max(tk, tail_pad), tq), F32),
            pltpu.VMEM((2, GQA_GROUP, 1, tq), F32),
            pltpu.VMEM((GQA_GROUP, 1, tq), F32),
            pltpu.VMEM((GQA_GROUP, 1, tq), F32),
            pltpu.VMEM((GQA_GROUP, HEAD_DIM, tq), F32),
        ],
        compiler_params=_cparams("parallel", "parallel", "arbitrary"),
        name="attention",
    )(q, k, v)


def _fourier_kernel(f_ref, cs_ref, f1r_ref, f1i_ref, g_ref, o_ref, wr_ref, wi_ref, os_ref,
                    *, L, n1, n1p, chunk):
    n_chunks = L // chunk
    last = L - chunk
    split = chunk - N_META

    def channel_dft_rows(r0, rows):
        return jnp.dot(f_ref[0, pl.ds(r0, rows), :], cs_ref[...], preferred_element_type=F32)

    def channel_dft(c, carry):
        r0 = pl.multiple_of(c * chunk, PACK_ROWS)
        z = channel_dft_rows(r0, chunk)
        wr_ref[pl.ds(r0 + N_META, chunk), :] = z[:, :FOURIER_GROUP_CH]
        wi_ref[pl.ds(r0 + N_META, chunk), :] = z[:, FOURIER_GROUP_CH:]
        return carry

    lax.fori_loop(0, n_chunks - 1, channel_dft, 0, unroll=4 if (n_chunks - 1) % 4 == 0 else 1)
    z = channel_dft_rows(last, chunk)
    wr_ref[last + N_META:L, :] = z[:split, :FOURIER_GROUP_CH]
    wi_ref[last + N_META:L, :] = z[:split, FOURIER_GROUP_CH:]
    wr_ref[0:N_META, :] = z[split:, :FOURIER_GROUP_CH]
    wi_ref[0:N_META, :] = z[split:, FOURIER_GROUP_CH:]
    if n1p > n1:
        zeros = jnp.zeros(((n1p - n1) * DFT_N2, FOURIER_GROUP_CH), F32)
        wr_ref[L:, :] = zeros
        wi_ref[L:, :] = zeros

    def row_dft(n2, carry):
        rows = pl.ds(n2, n1p, stride=DFT_N2)
        zr = wr_ref[rows, :].astype(BF16)
        zi = wi_ref[rows, :].astype(BF16)
        a = (jnp.dot(f1r_ref[...], zr, preferred_element_type=F32)
             + jnp.dot(f1i_ref[...], zi, preferred_element_type=F32))
        wr_ref[rows, :] = a[:n1p]
        wi_ref[rows, :] = a[n1p:]
        return carry

    lax.fori_loop(0, DFT_N2, row_dft, 0, unroll=16)

    def col_dft(k1, carry):
        r0 = pl.multiple_of(k1 * DFT_N2, DFT_N2)
        a = jnp.concatenate([wr_ref[pl.ds(r0, DFT_N2), :].astype(BF16),
                             wi_ref[pl.ds(r0, DFT_N2), :].astype(BF16)], axis=0)
        os_ref[pl.ds(k1, DFT_N2, stride=n1), :] = jnp.dot(g_ref[k1], a, preferred_element_type=F32)
        return carry

    lax.fori_loop(0, n1, col_dft, 0, unroll=(41 if n1 % 41 == 0 else 5) if n1 % 5 == 0 else 1)

    def emit(c, carry):
        r0 = pl.multiple_of(c * chunk, PACK_ROWS)
        o_ref[0, pl.ds(r0, chunk), :] = os_ref[pl.ds(r0 + N_META, chunk), :].astype(BF16)
        return carry

    lax.fori_loop(0, n_chunks - 1, emit, 0)
    o_ref[0, last:L - N_META, :] = os_ref[last + N_META:L, :].astype(BF16)
    o_ref[0, L - N_META:L, :] = os_ref[0:N_META, :].astype(BF16)


def _fourier(f, consts, *, chunk):
    B, L, _ = f.shape
    n1 = L // DFT_N2
    n1p = _round_up(n1, 8)
    cs, f1r, f1i, g = consts
    slab = pl.BlockSpec((1, L, FOURIER_GROUP_CH), lambda b, c: (b, 0, c))
    fixed2 = lambda b, c: (0, 0)
    return pl.pallas_call(
        functools.partial(_fourier_kernel, L=L, n1=n1, n1p=n1p, chunk=chunk),
        grid=(B, N_FOURIER_GROUPS),
        in_specs=[
            slab,
            pl.BlockSpec((FOURIER_GROUP_CH, 2 * FOURIER_GROUP_CH), fixed2),
            pl.BlockSpec((2 * n1p, n1p), fixed2),
            pl.BlockSpec((2 * n1p, n1p), fixed2),
            pl.BlockSpec((n1, DFT_N2, 2 * DFT_N2), lambda b, c: (0, 0, 0), pipeline_mode=pl.Buffered(1)),
        ],
        out_specs=slab,
        out_shape=jax.ShapeDtypeStruct((B, L, FOURIER_WIDTH), BF16),
        scratch_shapes=[pltpu.VMEM((n1p * DFT_N2, FOURIER_GROUP_CH), F32)] * 2
        + [pltpu.VMEM((L, FOURIER_GROUP_CH), F32)],
        compiler_params=_cparams("parallel", "parallel"),
        name="fourier",
    )(f, cs, f1r, f1i, g)


def _dft_constants(L):
    n1 = L // DFT_N2
    n1p = _round_up(n1, 8)
    k1 = np.arange(n1)
    ang1 = 2.0 * np.pi * np.outer(k1, k1) / n1
    c1, s1 = np.cos(ang1), np.sin(ang1)
    f1r = np.zeros((2 * n1p, n1p))
    f1i = np.zeros((2 * n1p, n1p))
    f1r[:n1, :n1] = c1
    f1r[n1p:n1p + n1, :n1] = -s1
    f1i[:n1, :n1] = s1
    f1i[n1p:n1p + n1, :n1] = c1
    n2 = np.arange(DFT_N2)
    k = k1[:, None, None] + n1 * n2[None, :, None]
    ang = 2.0 * np.pi * ((k * n2[None, None, :]) % L) / L
    g = np.concatenate([np.cos(ang), np.sin(ang)], axis=2) / math.sqrt(L)
    c = np.arange(FOURIER_GROUP_CH)
    angc = 2.0 * np.pi * np.outer(c, c) / FOURIER_GROUP_CH
    cs = np.concatenate([np.cos(angc), -np.sin(angc)], axis=1) / math.sqrt(FOURIER_GROUP_CH)
    return (jnp.asarray(cs, BF16), jnp.asarray(f1r, BF16), jnp.asarray(f1i, BF16), jnp.asarray(g, BF16))


def _mix_kernel(x_ref, meta_ref, h_ref, attn_ref, four_ref, wga_ref, wgf_ref, bga_ref, bgf_ref,
                wa_ref, wf_ref, wo_ref, o_ref, *, raw_input, tiles_per_batch):
    @pl.when(pl.program_id(1) == 0)
    def _():
        if raw_input:
            split = x_ref.shape[0] - N_META
            o_ref[:split, :] = x_ref[:split, :]
            o_ref[split:, :] = _meta_rows(x_ref, meta_ref, tiles_per_batch)
        else:
            o_ref[...] = x_ref[...]

    h = h_ref[...]
    g_attn = jax.nn.sigmoid(jnp.dot(h, wga_ref[...], preferred_element_type=F32) + bga_ref[...])
    g_four = jax.nn.sigmoid(jnp.dot(h, wgf_ref[...], preferred_element_type=F32) + bgf_ref[...])
    a_br = jnp.dot(attn_ref[...], wa_ref[...], preferred_element_type=F32)
    s_br = jnp.dot(four_ref[...], wf_ref[...], preferred_element_type=F32)
    merged = (g_attn * a_br + g_four * s_br).astype(BF16)
    o_ref[...] += jnp.dot(merged, wo_ref[...], preferred_element_type=F32)


def _mix(xs, meta, h, attn, four, w_gates, b_gates, w_a, w_f, w_o, *, tm, tn, tiles_per_batch, raw_input):
    R = h.shape[0]
    nj = D_MODEL // tn
    row = lambda i, j: (i, 0)
    return pl.pallas_call(
        functools.partial(_mix_kernel, raw_input=raw_input, tiles_per_batch=tiles_per_batch),
        grid=(R // tm, nj),
        in_specs=[
            pl.BlockSpec((pl.Squeezed(), tm, D_MODEL),
                         lambda i, j: (i // tiles_per_batch, i % tiles_per_batch, 0)),
            pl.BlockSpec((N_META, D_MODEL), lambda i, j: (0, 0)),
            pl.BlockSpec((tm, D_MODEL), row),
            pl.BlockSpec((tm, ATTN_WIDTH), row),
            pl.BlockSpec((tm, FOURIER_WIDTH), row),
            pl.BlockSpec((D_MODEL, tn), lambda i, j: (0, j)),
            pl.BlockSpec((D_MODEL, tn), lambda i, j: (0, j + nj)),
            pl.BlockSpec((1, tn), lambda i, j: (0, j)),
            pl.BlockSpec((1, tn), lambda i, j: (0, j + nj)),
            pl.BlockSpec((ATTN_WIDTH, tn), lambda i, j: (0, j)),
            pl.BlockSpec((FOURIER_WIDTH, tn), lambda i, j: (0, j)),
            pl.BlockSpec((tn, D_MODEL), lambda i, j: (j, 0)),
        ],
        out_specs=pl.BlockSpec((tm, D_MODEL), row),
        out_shape=jax.ShapeDtypeStruct((R, D_MODEL), F32),
        compiler_params=_cparams("parallel", "arbitrary"),
        name="mix_out",
    )(xs, meta, h, attn, four, w_gates, w_gates, b_gates, b_gates, w_a, w_f, w_o)


def _ffn_kernel(x_ref, xp_ref, xn_ref, g_ref, wg_ref, wv_ref, wc_ref, bc_ref, wd_ref,
                o_ref, h_ref, *, tm, tiles_per_batch):
    j = pl.program_id(1)
    H = CONV_HALO

    @pl.when(j == 0)
    def _():
        x = x_ref[...]
        g = g_ref[...]
        h_ref[H:H + tm, :] = (_rms(x) * g).astype(BF16)
        h_ref[0:H, :] = (_rms(xp_ref[...]) * g).astype(BF16)
        h_ref[H + tm:, :] = (_rms(xn_ref[...]) * g).astype(BF16)
        o_ref[...] = x

    h = h_ref[...]
    up_g = jnp.dot(h, wg_ref[...], preferred_element_type=F32)
    n = tm + 2 * H
    wc = wc_ref[...]
    prev = pltpu.roll(up_g, 1, 0)[H:H + tm]
    nxt = pltpu.roll(up_g, n - 1, 0)[H:H + tm]
    row = lax.broadcasted_iota(jnp.int32, (tm, 1), 0)
    seam = jnp.where(pl.program_id(0) % tiles_per_batch == tiles_per_batch - 1, tm - N_META, -1)
    prev = jnp.where(row == seam, 0.0, prev)
    nxt = jnp.where(row == seam - 1, 0.0, nxt)
    u = prev * wc[0:1] + up_g[H:H + tm] * wc[1:2] + nxt * wc[2:3] + bc_ref[...]
    val = jnp.dot(h[H:H + tm], wv_ref[...], preferred_element_type=F32)
    act = (u * jax.nn.sigmoid(u) * val).astype(BF16)
    o_ref[...] += jnp.dot(act, wd_ref[...], preferred_element_type=F32)


def _ffn(hs, gain, w_up, w_conv, b_conv, w_down, *, tm, tf, tiles_per_batch, n_tok_out=None):
    R = hs.shape[0]
    nj = D_FF // tf
    hb = tm // CONV_HALO
    nb = tiles_per_batch * hb
    row = lambda i, j: (i, 0)

    def prev_block(i, j):
        first = i % tiles_per_batch == 0
        return (jnp.where(first, (i // tiles_per_batch) * nb + nb - 1, i * hb - 1), 0)

    def next_block(i, j):
        last = i % tiles_per_batch == tiles_per_batch - 1
        return (jnp.where(last, (i // tiles_per_batch) * nb, (i + 1) * hb), 0)

    if n_tok_out is None:
        out_spec = pl.BlockSpec((tm, D_MODEL), row)
        out_shape = jax.ShapeDtypeStruct((R, D_MODEL), F32)
    else:
        out_spec = pl.BlockSpec((pl.Squeezed(), tm, D_MODEL),
                                lambda i, j: (i // tiles_per_batch, i % tiles_per_batch, 0))
        out_shape = jax.ShapeDtypeStruct((R // (tiles_per_batch * tm), n_tok_out, D_MODEL), F32)
    return pl.pallas_call(
        functools.partial(_ffn_kernel, tm=tm, tiles_per_batch=tiles_per_batch),
        grid=(R // tm, nj),
        in_specs=[
            pl.BlockSpec((tm, D_MODEL), row),
            pl.BlockSpec((CONV_HALO, D_MODEL), prev_block),
            pl.BlockSpec((CONV_HALO, D_MODEL), next_block),
            pl.BlockSpec((1, D_MODEL), lambda i, j: (0, 0)),
            pl.BlockSpec((D_MODEL, tf), lambda i, j: (0, j)),
            pl.BlockSpec((D_MODEL, tf), lambda i, j: (0, j + nj)),
            pl.BlockSpec((3, tf), lambda i, j: (0, j)),
            pl.BlockSpec((1, tf), lambda i, j: (0, j)),
            pl.BlockSpec((tf, D_MODEL), lambda i, j: (j, 0)),
        ],
        out_specs=out_spec,
        out_shape=out_shape,
        scratch_shapes=[pltpu.VMEM((tm + 2 * CONV_HALO, D_MODEL), BF16)],
        compiler_params=_cparams("parallel", "arbitrary"),
        name="ffn",
    )(hs, hs, hs, gain, w_up, w_up, w_conv, b_conv, w_down)


def _rope_tables(n_tok):
    rows = n_tok // GRID_W
    pos_r = np.concatenate([np.repeat(np.arange(rows), GRID_W), np.zeros(N_META)])
    pos_c = np.concatenate([np.tile(np.arange(GRID_W), rows), np.zeros(N_META)])
    inv_freq = 1.0 / (ROPE_THETA ** (np.arange(ROPE_PAIRS) / ROPE_PAIRS))
    ang_r = pos_r[:, None] * inv_freq[None, :]
    ang_c = pos_c[:, None] * inv_freq[None, :]
    cos_t = np.concatenate([np.cos(ang_r), np.cos(ang_r), np.cos(ang_c), np.cos(ang_c)], axis=-1)
    sin_t = np.concatenate([-np.sin(ang_r), np.sin(ang_r), -np.sin(ang_c), np.sin(ang_c)], axis=-1)
    return jnp.asarray(cos_t, F32), jnp.asarray(sin_t, F32)


def _tiles(L):
    tm = L // ROW_TILES_PER_BATCH
    return dict(tm=tm, tq=256, tk=1280, tn_mix=512, tf=512)


def kernel(x, meta_tokens, norm_mix, norm_ffn, w_in, b_gate, q_norm, k_norm, w_attn_br, w_four, w_out,
           w_up, w_conv, b_conv, w_down):
    B, n_tok, D = x.shape
    depth = w_in.shape[0]
    L = n_tok + N_META
    t = _tiles(L)
    tm = t["tm"]
    assert D == D_MODEL and n_tok % GRID_W == 0 and L % DFT_N2 == 0
    assert L % tm == 0 and tm % PACK_ROWS == 0
    R = B * L
    tiles_per_batch = L // tm

    cos_t, sin_t = _rope_tables(n_tok)
    dft = _dft_constants(L)

    meta = meta_tokens.astype(x.dtype)
    xs = x
    for i in range(depth):
        raw = i == 0
        w_qkvf = w_in[i, :, :QKVF_WIDTH].astype(BF16)
        w_gates = w_in[i, :, QKVF_WIDTH:].astype(BF16)
        q, k, v, f, h = _inproj(xs, meta, norm_mix[i][None], w_qkvf, q_norm[i][None], k_norm[i][None],
                                cos_t, sin_t, tm=tm, tiles_per_batch=tiles_per_batch, raw_input=raw)
        attn = _attention(q.reshape(B, L, ATTN_WIDTH), k.reshape(B, L, KV_WIDTH), v.reshape(B, L, KV_WIDTH),
                          tq=t["tq"], tk=t["tk"]).reshape(R, ATTN_WIDTH)
        four = _fourier(f.reshape(B, L, FOURIER_WIDTH), dft, chunk=tm).reshape(R, FOURIER_WIDTH)
        hs = _mix(xs, meta, h, attn, four, w_gates, b_gate[i][None],
                  w_attn_br[i].astype(BF16), w_four[i].astype(BF16), w_out[i].astype(BF16),
                  tm=tm, tn=t["tn_mix"], tiles_per_batch=tiles_per_batch, raw_input=raw)
        hs = _ffn(hs, norm_ffn[i][None], w_up[i].astype(BF16), w_conv[i], b_conv[i][None],
                  w_down[i].astype(BF16), tm=tm, tf=t["tf"], tiles_per_batch=tiles_per_batch,
                  n_tok_out=n_tok if i == depth - 1 else None)
        xs = hs.reshape(B, L, D) if i < depth - 1 else hs
    return xs
```

```python
import functools
import math

import jax
import jax.numpy as jnp
import numpy as np
from jax import lax
from jax.experimental import pallas as pl
from jax.experimental.pallas import tpu as pltpu

F32 = jnp.float32
BF16 = jnp.bfloat16

D_MODEL = 2048
N_HEADS = 8
N_KV_HEADS = 2
HEAD_DIM = 128
GQA_GROUP = N_HEADS // N_KV_HEADS
ATTN_WIDTH = N_HEADS * HEAD_DIM
KV_WIDTH = N_KV_HEADS * HEAD_DIM
N_FOURIER_GROUPS = 8
FOURIER_GROUP_CH = 128
FOURIER_WIDTH = N_FOURIER_GROUPS * FOURIER_GROUP_CH
QKVF_WIDTH = ATTN_WIDTH + 2 * KV_WIDTH + FOURIER_WIDTH
OFF_K = ATTN_WIDTH
OFF_V = OFF_K + KV_WIDTH
OFF_F = OFF_V + KV_WIDTH
D_FF = 5632
N_META = 16
GRID_W = 64
NORM_EPS = 1e-6
ROPE_THETA = 10000.0
ROPE_PAIRS = HEAD_DIM // 4

DFT_N2 = 80
ROW_TILES_PER_BATCH = 25
PACK_ROWS = 16
LANES = 128
CONV_HALO = PACK_ROWS
MASK_VALUE = -1e30
VMEM_LIMIT = 56 * 1024 * 1024


def _cparams(*sem):
    return pltpu.CompilerParams(dimension_semantics=sem, vmem_limit_bytes=VMEM_LIMIT)


def _rms(x):
    return x * lax.rsqrt(jnp.mean(x * x, axis=-1, keepdims=True) + NORM_EPS)


def _round_up(n, m):
    return -(-n // m) * m


def _meta_rows(x_ref, meta_ref, tiles_per_batch):
    split = x_ref.shape[0] - N_META
    last = pl.program_id(0) % tiles_per_batch == tiles_per_batch - 1
    return jnp.where(last, meta_ref[...], x_ref[split:, :])


def _inproj_kernel(x_ref, meta_ref, g_ref, w_ref, qn_ref, kn_ref, cos_ref, sin_ref,
                   q_ref, k_ref, v_ref, f_ref, h_ref, *, raw_input, tiles_per_batch):
    g = g_ref[...]
    if raw_input:
        split = x_ref.shape[0] - N_META
        h_ref[:split, :] = (_rms(x_ref[:split, :]) * g).astype(BF16)
        h_ref[split:, :] = (_rms(_meta_rows(x_ref, meta_ref, tiles_per_batch)) * g).astype(BF16)
    else:
        h_ref[...] = (_rms(x_ref[...]) * g).astype(BF16)
    proj = jnp.dot(h_ref[...], w_ref[...], preferred_element_type=F32)
    cos = cos_ref[...]
    sin = sin_ref[...]
    lane = lax.broadcasted_iota(jnp.int32, cos.shape, 1)
    first_half = (lane % (HEAD_DIM // 2)) < ROPE_PAIRS

    def norm_rope(t, gain):
        t = _rms(t) * gain
        partner = jnp.where(first_half,
                            pltpu.roll(t, HEAD_DIM - ROPE_PAIRS, 1),
                            pltpu.roll(t, ROPE_PAIRS, 1))
        return t * cos + partner * sin

    scale = math.log2(math.e) / math.sqrt(HEAD_DIM)
    for hh in range(N_HEADS):
        c = hh * HEAD_DIM
        q_ref[:, c:c + HEAD_DIM] = (norm_rope(proj[:, c:c + HEAD_DIM], qn_ref[...]) * scale).astype(BF16)
    for hh in range(N_KV_HEADS):
        c = hh * HEAD_DIM
        k_ref[:, c:c + HEAD_DIM] = norm_rope(proj[:, OFF_K + c:OFF_K + c + HEAD_DIM], kn_ref[...]).astype(BF16)
    v_ref[...] = proj[:, OFF_V:OFF_F].astype(BF16)
    f_ref[...] = proj[:, OFF_F:].astype(BF16)


def _inproj(xs, meta, gain, w_qkvf, qn, kn, cos_t, sin_t, *, tm, tiles_per_batch, raw_input):
    R = xs.shape[0] * tiles_per_batch * tm
    row = lambda i: (i, 0)
    fixed = lambda i: (0, 0)
    tab = lambda i: (i % tiles_per_batch, 0)
    widths = (ATTN_WIDTH, KV_WIDTH, KV_WIDTH, FOURIER_WIDTH, D_MODEL)
    return pl.pallas_call(
        functools.partial(_inproj_kernel, raw_input=raw_input, tiles_per_batch=tiles_per_batch),
        grid=(R // tm,),
        in_specs=[
            pl.BlockSpec((pl.Squeezed(), tm, D_MODEL), lambda i: (i // tiles_per_batch, i % tiles_per_batch, 0)),
            pl.BlockSpec((N_META, D_MODEL), fixed),
            pl.BlockSpec((1, D_MODEL), fixed),
            pl.BlockSpec((D_MODEL, QKVF_WIDTH), fixed),
            pl.BlockSpec((1, HEAD_DIM), fixed),
            pl.BlockSpec((1, HEAD_DIM), fixed),
            pl.BlockSpec((tm, HEAD_DIM), tab),
            pl.BlockSpec((tm, HEAD_DIM), tab),
        ],
        out_specs=[pl.BlockSpec((tm, w), row) for w in widths],
        out_shape=[jax.ShapeDtypeStruct((R, w), BF16) for w in widths],
        compiler_params=_cparams("parallel"),
        name="inproj",
    )(xs, meta, gain, w_qkvf, qn, kn, cos_t, sin_t)


def _attn_kernel(q_ref, k_ref, v_ref, o_ref, vt_ref, qt_ref, s_ref, mt_ref, m_ref, l_ref, acc_ref,
                 *, L, tk, n_full, tail, tail_pad):
    has_tail = tail > 0
    n_stage = n_full + (1 if has_tail else 0)
    off = 1 if has_tail else 0
    tail_start = n_full * tk

    def padded_tail(ref):
        rows = ref[0, tail_start:L, :]
        if tail_pad == tail:
            return rows
        return jnp.concatenate([rows, jnp.zeros((tail_pad - tail, HEAD_DIM), rows.dtype)], axis=0)

    @pl.when(pl.program_id(2) == 0)
    def _():
        for j in range(n_full):
            vt_ref[:, j * tk:(j + 1) * tk] = v_ref[0, j * tk:(j + 1) * tk, :].astype(F32).T.astype(BF16)
        if has_tail:
            vt_ref[:, tail_start:tail_start + tail_pad] = padded_tail(v_ref).astype(F32).T.astype(BF16)

    m_ref[...] = jnp.full(m_ref.shape, MASK_VALUE, F32)
    l_ref[...] = jnp.zeros(l_ref.shape, F32)
    acc_ref[...] = jnp.zeros(acc_ref.shape, F32)
    for g in range(GQA_GROUP):
        qt_ref[g] = q_ref[0, :, g * HEAD_DIM:(g + 1) * HEAD_DIM].astype(F32).T.astype(BF16)

    def is_tail(t):
        return has_tail and isinstance(t, int) and t == 0

    def tile_start(t):
        if isinstance(t, int):
            return tail_start if is_tail(t) else (t - off) * tk
        return pl.multiple_of((t - off) * tk, tk)

    def scores(t, slot):
        rows = tail_pad if is_tail(t) else tk
        kj = padded_tail(k_ref) if is_tail(t) else k_ref[0, pl.ds(tile_start(t), tk), :]
        for g in range(GQA_GROUP):
            st = jnp.dot(kj, qt_ref[g], preferred_element_type=F32)
            if is_tail(t) and tail_pad != tail:
                key = lax.broadcasted_iota(jnp.int32, st.shape, 0)
                st = jnp.where(key < tail, st, MASK_VALUE)
            s_ref[slot, g, 0:rows, :] = st
            mt_ref[slot, g] = jnp.max(st, axis=0, keepdims=True)

    def softmax_pv(t, slot):
        rows = tail_pad if is_tail(t) else tk
        vtj = vt_ref[:, pl.ds(tile_start(t), rows)]
        for g in range(GQA_GROUP):
            st = s_ref[slot, g, 0:rows, :]
            m_prev = m_ref[g]
            m_new = jnp.maximum(m_prev, mt_ref[slot, g])
            alpha = jnp.exp2(m_prev - m_new)
            p = jnp.exp2(st - m_new)
            l_ref[g] = alpha * l_ref[g] + jnp.sum(p, axis=0, keepdims=True)
            acc_ref[g] = alpha * acc_ref[g] + jnp.dot(vtj, p.astype(BF16), preferred_element_type=F32)
            m_ref[g] = m_new

    def stage(t, slot):
        if not isinstance(t, int) or t + 1 < n_stage:
            scores(t + 1, 1 - slot)
        softmax_pv(t, slot)

    scores(0, 0)
    stage(0, 0)
    n_pairs = max(0, (n_stage - 2) // 2)
    if n_pairs > 0:
        def body(p, carry):
            stage(1 + 2 * p, 1)
            stage(2 + 2 * p, 0)
            return carry
        lax.fori_loop(0, n_pairs, body, 0)
    for t in range(1 + 2 * n_pairs, n_stage):
        stage(t, t % 2)
    for g in range(GQA_GROUP):
        c = g * HEAD_DIM
        o_ref[0, :, c:c + HEAD_DIM] = (acc_ref[g] / l_ref[g]).T.astype(BF16)


def _attention(q, k, v, *, tq, tk):
    B, L, _ = q.shape
    gw = GQA_GROUP * HEAD_DIM
    n_full = L // tk
    tail = L - n_full * tk
    tail_pad = _round_up(tail, LANES)
    kern = functools.partial(_attn_kernel, L=L, tk=tk, n_full=n_full, tail=tail, tail_pad=tail_pad)
    kv_spec = pl.BlockSpec((1, L, HEAD_DIM), lambda b, h, i: (b, 0, h))
    return pl.pallas_call(
        kern,
        grid=(B, N_KV_HEADS, pl.cdiv(L, tq)),
        in_specs=[pl.BlockSpec((1, tq, gw), lambda b, h, i: (b, i, h)), kv_spec, kv_spec],
        out_specs=pl.BlockSpec((1, tq, gw), lambda b, h, i: (b, i, h)),
        out_shape=jax.ShapeDtypeStruct((B, L, ATTN_WIDTH), BF16),
        scratch_shapes=[
            pltpu.VMEM((HEAD_DIM, n_full * tk + tail_pad), BF16),
            pltpu.VMEM((GQA_GROUP, HEAD_DIM, tq), BF16),
            pltpu.VMEM((2, GQA_GROUP, max(tk, tail_pad), tq), F32),
            pltpu.VMEM((2, GQA_GROUP, 1, tq), F32),
            pltpu.VMEM((GQA_GROUP, 1, tq), F32),
            pltpu.VMEM((GQA_GROUP, 1, tq), F32),
            pltpu.VMEM((GQA_GROUP, HEAD_DIM, tq), F32),
        ],
        compiler_params=_cparams("parallel", "parallel", "arbitrary"),
        name="attention",
    )(q, k, v)


def _fourier_kernel(f_ref, cs_ref, f1r_ref, f1i_ref, g_ref, o_ref, wr_ref, wi_ref, os_ref,
                    *, L, n1, n1p, chunk):
    n_chunks = L // chunk
    last = L - chunk
    split = chunk - N_META

    def channel_dft_rows(r0, rows):
        return jnp.dot(f_ref[0, pl.ds(r0, rows), :], cs_ref[...], preferred_element_type=F32)

    def channel_dft(c, carry):
        r0 = pl.multiple_of(c * chunk, PACK_ROWS)
        z = channel_dft_rows(r0, chunk)
        wr_ref[pl.ds(r0 + N_META, chunk), :] = z[:, :FOURIER_GROUP_CH]
        wi_ref[pl.ds(r0 + N_META, chunk), :] = z[:, FOURIER_GROUP_CH:]
        return carry

    lax.fori_loop(0, n_chunks - 1, channel_dft, 0, unroll=4 if (n_chunks - 1) % 4 == 0 else 1)
    z = channel_dft_rows(last, chunk)
    wr_ref[last + N_META:L, :] = z[:split, :FOURIER_GROUP_CH]
    wi_ref[last + N_META:L, :] = z[:split, FOURIER_GROUP_CH:]
    wr_ref[0:N_META, :] = z[split:, :FOURIER_GROUP_CH]
    wi_ref[0:N_META, :] = z[split:, FOURIER_GROUP_CH:]
    if n1p > n1:
        zeros = jnp.zeros(((n1p - n1) * DFT_N2, FOURIER_GROUP_CH), F32)
        wr_ref[L:, :] = zeros
        wi_ref[L:, :] = zeros

    def row_dft(n2, carry):
        rows = pl.ds(n2, n1p, stride=DFT_N2)
        zr = wr_ref[rows, :].astype(BF16)
        zi = wi_ref[rows, :].astype(BF16)
        a = (jnp.dot(f1r_ref[...], zr, preferred_element_type=F32)
             + jnp.dot(f1i_ref[...], zi, preferred_element_type=F32))
        wr_ref[rows, :] = a[:n1p]
        wi_ref[rows, :] = a[n1p:]
        return carry

    lax.fori_loop(0, DFT_N2, row_dft, 0, unroll=16)

    def col_dft(k1, carry):
        r0 = pl.multiple_of(k1 * DFT_N2, DFT_N2)
        a = jnp.concatenate([wr_ref[pl.ds(r0, DFT_N2), :].astype(BF16),
                             wi_ref[pl.ds(r0, DFT_N2), :].astype(BF16)], axis=0)
        os_ref[pl.ds(k1, DFT_N2, stride=n1), :] = jnp.dot(g_ref[k1], a, preferred_element_type=F32)
        return carry

    lax.fori_loop(0, n1, col_dft, 0, unroll=(41 if n1 % 41 == 0 else 5) if n1 % 5 == 0 else 1)

    def emit(c, carry):
        r0 = pl.multiple_of(c * chunk, PACK_ROWS)
        o_ref[0, pl.ds(r0, chunk), :] = os_ref[pl.ds(r0 + N_META, chunk), :].astype(BF16)
        return carry

    lax.fori_loop(0, n_chunks - 1, emit, 0)
    o_ref[0, last:L - N_META, :] = os_ref[last + N_META:L, :].astype(BF16)
    o_ref[0, L - N_META:L, :] = os_ref[0:N_META, :].astype(BF16)


def _fourier(f, consts, *, chunk):
    B, L, _ = f.shape
    n1 = L // DFT_N2
    n1p = _round_up(n1, 8)
    cs, f1r, f1i, g = consts
    slab = pl.BlockSpec((1, L, FOURIER_GROUP_CH), lambda b, c: (b, 0, c))
    fixed2 = lambda b, c: (0, 0)
    return pl.pallas_call(
        functools.partial(_fourier_kernel, L=L, n1=n1, n1p=n1p, chunk=chunk),
        grid=(B, N_FOURIER_GROUPS),
        in_specs=[
            slab,
            pl.BlockSpec((FOURIER_GROUP_CH, 2 * FOURIER_GROUP_CH), fixed2),
            pl.BlockSpec((2 * n1p, n1p), fixed2),
            pl.BlockSpec((2 * n1p, n1p), fixed2),
            pl.BlockSpec((n1, DFT_N2, 2 * DFT_N2), lambda b, c: (0, 0, 0), pipeline_mode=pl.Buffered(1)),
        ],
        out_specs=slab,
        out_shape=jax.ShapeDtypeStruct((B, L, FOURIER_WIDTH), BF16),
        scratch_shapes=[pltpu.VMEM((n1p * DFT_N2, FOURIER_GROUP_CH), F32)] * 2
        + [pltpu.VMEM((L, FOURIER_GROUP_CH), F32)],
        compiler_params=_cparams("parallel", "parallel"),
        name="fourier",
    )(f, cs, f1r, f1i, g)


def _dft_constants(L):
    n1 = L // DFT_N2
    n1p = _round_up(n1, 8)
    k1 = np.arange(n1)
    ang1 = 2.0 * np.pi * np.outer(k1, k1) / n1
    c1, s1 = np.cos(ang1), np.sin(ang1)
    f1r = np.zeros((2 * n1p, n1p))
    f1i = np.zeros((2 * n1p, n1p))
    f1r[:n1, :n1] = c1
    f1r[n1p:n1p + n1, :n1] = -s1
    f1i[:n1, :n1] = s1
    f1i[n1p:n1p + n1, :n1] = c1
    n2 = np.arange(DFT_N2)
    k = k1[:, None, None] + n1 * n2[None, :, None]
    ang = 2.0 * np.pi * ((k * n2[None, None, :]) % L) / L
    g = np.concatenate([np.cos(ang), np.sin(ang)], axis=2) / math.sqrt(L)
    c = np.arange(FOURIER_GROUP_CH)
    angc = 2.0 * np.pi * np.outer(c, c) / FOURIER_GROUP_CH
    cs = np.concatenate([np.cos(angc), -np.sin(angc)], axis=1) / math.sqrt(FOURIER_GROUP_CH)
    return (jnp.asarray(cs, BF16), jnp.asarray(f1r, BF16), jnp.asarray(f1i, BF16), jnp.asarray(g, BF16))


def _mix_kernel(x_ref, meta_ref, h_ref, attn_ref, four_ref, wga_ref, wgf_ref, bga_ref, bgf_ref,
                wa_ref, wf_ref, wo_ref, o_ref, *, raw_input, tiles_per_batch):
    @pl.when(pl.program_id(1) == 0)
    def _():
        if raw_input:
            split = x_ref.shape[0] - N_META
            o_ref[:split, :] = x_ref[:split, :]
            o_ref[split:, :] = _meta_rows(x_ref, meta_ref, tiles_per_batch)
        else:
            o_ref[...] = x_ref[...]

    h = h_ref[...]
    g_attn = jax.nn.sigmoid(jnp.dot(h, wga_ref[...], preferred_element_type=F32) + bga_ref[...])
    g_four = jax.nn.sigmoid(jnp.dot(h, wgf_ref[...], preferred_element_type=F32) + bgf_ref[...])
    a_br = jnp.dot(attn_ref[...], wa_ref[...], preferred_element_type=F32)
    s_br = jnp.dot(four_ref[...], wf_ref[...], preferred_element_type=F32)
    merged = (g_attn * a_br + g_four * s_br).astype(BF16)
    o_ref[...] += jnp.dot(merged, wo_ref[...], preferred_element_type=F32)


def _mix(xs, meta, h, attn, four, w_gates, b_gates, w_a, w_f, w_o, *, tm, tn, tiles_per_batch, raw_input):
    R = h.shape[0]
    nj = D_MODEL // tn
    row = lambda i, j: (i, 0)
    return pl.pallas_call(
        functools.partial(_mix_kernel, raw_input=raw_input, tiles_per_batch=tiles_per_batch),
        grid=(R // tm, nj),
        in_specs=[
            pl.BlockSpec((pl.Squeezed(), tm, D_MODEL),
                         lambda i, j: (i // tiles_per_batch, i % tiles_per_batch, 0)),
            pl.BlockSpec((N_META, D_MODEL), lambda i, j: (0, 0)),
            pl.BlockSpec((tm, D_MODEL), row),
            pl.BlockSpec((tm, ATTN_WIDTH), row),
            pl.BlockSpec((tm, FOURIER_WIDTH), row),
            pl.BlockSpec((D_MODEL, tn), lambda i, j: (0, j)),
            pl.BlockSpec((D_MODEL, tn), lambda i, j: (0, j + nj)),
            pl.BlockSpec((1, tn), lambda i, j: (0, j)),
            pl.BlockSpec((1, tn), lambda i, j: (0, j + nj)),
            pl.BlockSpec((ATTN_WIDTH, tn), lambda i, j: (0, j)),
            pl.BlockSpec((FOURIER_WIDTH, tn), lambda i, j: (0, j)),
            pl.BlockSpec((tn, D_MODEL), lambda i, j: (j, 0)),
        ],
        out_specs=pl.BlockSpec((tm, D_MODEL), row),
        out_shape=jax.ShapeDtypeStruct((R, D_MODEL), F32),
        compiler_params=_cparams("parallel", "arbitrary"),
        name="mix_out",
    )(xs, meta, h, attn, four, w_gates, w_gates, b_gates, b_gates, w_a, w_f, w_o)


def _ffn_kernel(x_ref, xp_ref, xn_ref, g_ref, wg_ref, wv_ref, wc_ref, bc_ref, wd_ref,
                o_ref, h_ref, *, tm, tiles_per_batch):
    j = pl.program_id(1)
    H = CONV_HALO

    @pl.when(j == 0)
    def _():
        x = x_ref[...]
        g = g_ref[...]
        h_ref[H:H + tm, :] = (_rms(x) * g).astype(BF16)
        h_ref[0:H, :] = (_rms(xp_ref[...]) * g).astype(BF16)
        h_ref[H + tm:, :] = (_rms(xn_ref[...]) * g).astype(BF16)
        o_ref[...] = x

    h = h_ref[...]
    up_g = jnp.dot(h, wg_ref[...], preferred_element_type=F32)
    n = tm + 2 * H
    wc = wc_ref[...]
    prev = pltpu.roll(up_g, 1, 0)[H:H + tm]
    nxt = pltpu.roll(up_g, n - 1, 0)[H:H + tm]
    row = lax.broadcasted_iota(jnp.int32, (tm, 1), 0)
    seam = jnp.where(pl.program_id(0) % tiles_per_batch == tiles_per_batch - 1, tm - N_META, -1)
    prev = jnp.where(row == seam, 0.0, prev)
    nxt = jnp.where(row == seam - 1, 0.0, nxt)
    u = prev * wc[0:1] + up_g[H:H + tm] * wc[1:2] + nxt * wc[2:3] + bc_ref[...]
    val = jnp.dot(h[H:H + tm], wv_ref[...], preferred_element_type=F32)
    act = (u * jax.nn.sigmoid(u) * val).astype(BF16)
    o_ref[...] += jnp.dot(act, wd_ref[...], preferred_element_type=F32)


def _ffn(hs, gain, w_up, w_conv, b_conv, w_down, *, tm, tf, tiles_per_batch, n_tok_out=None):
    R = hs.shape[0]
    nj = D_FF // tf
    hb = tm // CONV_HALO
    nb = tiles_per_batch * hb
    row = lambda i, j: (i, 0)

    def prev_block(i, j):
        first = i % tiles_per_batch == 0
        return (jnp.where(first, (i // tiles_per_batch) * nb + nb - 1, i * hb - 1), 0)

    def next_block(i, j):
        last = i % tiles_per_batch == tiles_per_batch - 1
        return (jnp.where(last, (i // tiles_per_batch) * nb, (i + 1) * hb), 0)

    if n_tok_out is None:
        out_spec = pl.BlockSpec((tm, D_MODEL), row)
        out_shape = jax.ShapeDtypeStruct((R, D_MODEL), F32)
    else:
        out_spec = pl.BlockSpec((pl.Squeezed(), tm, D_MODEL),
                                lambda i, j: (i // tiles_per_batch, i % tiles_per_batch, 0))
        out_shape = jax.ShapeDtypeStruct((R // (tiles_per_batch * tm), n_tok_out, D_MODEL), F32)
    return pl.pallas_call(
        functools.partial(_ffn_kernel, tm=tm, tiles_per_batch=tiles_per_batch),
        grid=(R // tm, nj),
        in_specs=[
            pl.BlockSpec((tm, D_MODEL), row),
            pl.BlockSpec((CONV_HALO, D_MODEL), prev_block),
            pl.BlockSpec((CONV_HALO, D_MODEL), next_block),
            pl.BlockSpec((1, D_MODEL), lambda i, j: (0, 0)),
            pl.BlockSpec((D_MODEL, tf), lambda i, j: (0, j)),
            pl.BlockSpec((D_MODEL, tf), lambda i, j: (0, j + nj)),
            pl.BlockSpec((3, tf), lambda i, j: (0, j)),
            pl.BlockSpec((1, tf), lambda i, j: (0, j)),
            pl.BlockSpec((tf, D_MODEL), lambda i, j: (j, 0)),
        ],
        out_specs=out_spec,
        out_shape=out_shape,
        scratch_shapes=[pltpu.VMEM((tm + 2 * CONV_HALO, D_MODEL), BF16)],
        compiler_params=_cparams("parallel", "arbitrary"),
        name="ffn",
    )(hs, hs, hs, gain, w_up, w_up, w_conv, b_conv, w_down)


def _rope_tables(n_tok):
    rows = n_tok // GRID_W
    pos_r = np.concatenate([np.repeat(np.arange(rows), GRID_W), np.zeros(N_META)])
    pos_c = np.concatenate([np.tile(np.arange(GRID_W), rows), np.zeros(N_META)])
    inv_freq = 1.0 / (ROPE_THETA ** (np.arange(ROPE_PAIRS) / ROPE_PAIRS))
    ang_r = pos_r[:, None] * inv_freq[None, :]
    ang_c = pos_c[:, None] * inv_freq[None, :]
    cos_t = np.concatenate([np.cos(ang_r), np.cos(ang_r), np.cos(ang_c), np.cos(ang_c)], axis=-1)
    sin_t = np.concatenate([-np.sin(ang_r), np.sin(ang_r), -np.sin(ang_c), np.sin(ang_c)], axis=-1)
    return jnp.asarray(cos_t, F32), jnp.asarray(sin_t, F32)


def _tiles(L):
    tm = L // ROW_TILES_PER_BATCH
    return dict(tm=tm, tq=256, tk=1280, tn_mix=512, tf=512)


def kernel(x, meta_tokens, norm_mix, norm_ffn, w_in, b_gate, q_norm, k_norm, w_attn_br, w_four, w_out,
           w_up, w_conv, b_conv, w_down):
    B, n_tok, D = x.shape
    depth = w_in.shape[0]
    L = n_tok + N_META
    t = _tiles(L)
    tm = t["tm"]
    assert D == D_MODEL and n_tok % GRID_W == 0 and L % DFT_N2 == 0
    assert L % tm == 0 and tm % PACK_ROWS == 0
    R = B * L
    tiles_per_batch = L // tm

    cos_t, sin_t = _rope_tables(n_tok)
    dft = _dft_constants(L)

    meta = meta_tokens.astype(x.dtype)
    xs = x
    for i in range(depth):
        raw = i == 0
        w_qkvf = w_in[i, :, :QKVF_WIDTH].astype(BF16)
        w_gates = w_in[i, :, QKVF_WIDTH:].astype(BF16)
        q, k, v, f, h = _inproj(xs, meta, norm_mix[i][None], w_qkvf, q_norm[i][None], k_norm[i][None],
                                cos_t, sin_t, tm=tm, tiles_per_batch=tiles_per_batch, raw_input=raw)
        attn = _attention(q.reshape(B, L, ATTN_WIDTH), k.reshape(B, L, KV_WIDTH), v.reshape(B, L, KV_WIDTH),
                          tq=t["tq"], tk=t["tk"] * (1 + i)).reshape(R, ATTN_WIDTH)
        four = _fourier(f.reshape(B, L, FOURIER_WIDTH), dft, chunk=tm).reshape(R, FOURIER_WIDTH)
        hs = _mix(xs, meta, h, attn, four, w_gates, b_gate[i][None],
                  w_attn_br[i].astype(BF16), w_four[i].astype(BF16), w_out[i].astype(BF16),
                  tm=tm, tn=t["tn_mix"], tiles_per_batch=tiles_per_batch, raw_input=raw)
        hs = _ffn(hs, norm_ffn[i][None], w_up[i].astype(BF16), w_conv[i], b_conv[i][None],
                  w_down[i].astype(BF16), tm=tm, tf=t["tf"], tiles_per_batch=tiles_per_batch,
                  n_tok_out=n_tok if i == depth - 1 else None)
        xs = hs.reshape(B, L, D) if i < depth - 1 else hs
    return xs
```

```python
import functools
import math

import jax
import jax.numpy as jnp
import numpy as np
from jax import lax
from jax.experimental import pallas as pl
from jax.experimental.pallas import tpu as pltpu

F32 = jnp.float32
BF16 = jnp.bfloat16

D_MODEL = 2048
N_HEADS = 8
N_KV_HEADS = 2
HEAD_DIM = 128
GQA_GROUP = N_HEADS // N_KV_HEADS
ATTN_WIDTH = N_HEADS * HEAD_DIM
KV_WIDTH = N_KV_HEADS * HEAD_DIM
N_FOURIER_GROUPS = 8
FOURIER_GROUP_CH = 128
FOURIER_WIDTH = N_FOURIER_GROUPS * FOURIER_GROUP_CH
QKVF_WIDTH = ATTN_WIDTH + 2 * KV_WIDTH + FOURIER_WIDTH
OFF_K = ATTN_WIDTH
OFF_V = OFF_K + KV_WIDTH
OFF_F = OFF_V + KV_WIDTH
D_FF = 5632
N_META = 16
GRID_W = 64
NORM_EPS = 1e-6
ROPE_THETA = 10000.0
ROPE_PAIRS = HEAD_DIM // 4

DFT_N2 = 80
ROW_TILES_PER_BATCH = 25
PACK_ROWS = 16
LANES = 128
CONV_HALO = PACK_ROWS
MASK_VALUE = -1e30
VMEM_LIMIT = 56 * 1024 * 1024


def _cparams(*sem):
    return pltpu.CompilerParams(dimension_semantics=sem, vmem_limit_bytes=VMEM_LIMIT)


def _rms(x):
    return x * lax.rsqrt(jnp.mean(x * x, axis=-1, keepdims=True) + NORM_EPS)


def _round_up(n, m):
    return -(-n // m) * m


def _meta_rows(x_ref, meta_ref, tiles_per_batch):
    split = x_ref.shape[0] - N_META
    last = pl.program_id(0) % tiles_per_batch == tiles_per_batch - 1
    return jnp.where(last, meta_ref[...], x_ref[split:, :])


def _inproj_kernel(x_ref, meta_ref, g_ref, w_ref, qn_ref, kn_ref, cos_ref, sin_ref,
                   q_ref, k_ref, v_ref, f_ref, h_ref, *, raw_input, tiles_per_batch):
    g = g_ref[...]
    if raw_input:
        split = x_ref.shape[0] - N_META
        h_ref[:split, :] = (_rms(x_ref[:split, :]) * g).astype(BF16)
        h_ref[split:, :] = (_rms(_meta_rows(x_ref, meta_ref, tiles_per_batch)) * g).astype(BF16)
    else:
        h_ref[...] = (_rms(x_ref[...]) * g).astype(BF16)
    proj = jnp.dot(h_ref[...], w_ref[...], preferred_element_type=F32)
    cos = cos_ref[...]
    sin = sin_ref[...]
    lane = lax.broadcasted_iota(jnp.int32, cos.shape, 1)
    first_half = (lane % (HEAD_DIM // 2)) < ROPE_PAIRS

    def norm_rope(t, gain):
        t = _rms(t) * gain
        partner = jnp.where(first_half,
                            pltpu.roll(t, HEAD_DIM - ROPE_PAIRS, 1),
                            pltpu.roll(t, ROPE_PAIRS, 1))
        return t * cos + partner * sin

    scale = math.log2(math.e) / math.sqrt(HEAD_DIM)
    for hh in range(N_HEADS):
        c = hh * HEAD_DIM
        q_ref[:, c:c + HEAD_DIM] = (norm_rope(proj[:, c:c + HEAD_DIM], qn_ref[...]) * scale).astype(BF16)
    for hh in range(N_KV_HEADS):
        c = hh * HEAD_DIM
        k_ref[:, c:c + HEAD_DIM] = norm_rope(proj[:, OFF_K + c:OFF_K + c + HEAD_DIM], kn_ref[...]).astype(BF16)
    v_ref[...] = proj[:, OFF_V:OFF_F].astype(BF16)
    f_ref[...] = proj[:, OFF_F:].astype(BF16)


def _inproj(xs, meta, gain, w_qkvf, qn, kn, cos_t, sin_t, *, tm, tiles_per_batch, raw_input):
    R = xs.shape[0] * tiles_per_batch * tm
    row = lambda i: (i, 0)
    fixed = lambda i: (0, 0)
    tab = lambda i: (i % tiles_per_batch, 0)
    widths = (ATTN_WIDTH, KV_WIDTH, KV_WIDTH, FOURIER_WIDTH, D_MODEL)
    return pl.pallas_call(
        functools.partial(_inproj_kernel, raw_input=raw_input, tiles_per_batch=tiles_per_batch),
        grid=(R // tm,),
        in_specs=[
            pl.BlockSpec((pl.Squeezed(), tm, D_MODEL), lambda i: (i // tiles_per_batch, i % tiles_per_batch, 0)),
            pl.BlockSpec((N_META, D_MODEL), fixed),
            pl.BlockSpec((1, D_MODEL), fixed),
            pl.BlockSpec((D_MODEL, QKVF_WIDTH), fixed),
            pl.BlockSpec((1, HEAD_DIM), fixed),
            pl.BlockSpec((1, HEAD_DIM), fixed),
            pl.BlockSpec((tm, HEAD_DIM), tab),
            pl.BlockSpec((tm, HEAD_DIM), tab),
        ],
        out_specs=[pl.BlockSpec((tm, w), row) for w in widths],
        out_shape=[jax.ShapeDtypeStruct((R, w), BF16) for w in widths],
        compiler_params=_cparams("parallel"),
        name="inproj",
    )(xs, meta, gain, w_qkvf, qn, kn, cos_t, sin_t)


def _attn_kernel(q_ref, k_ref, v_ref, o_ref, vt_ref, qt_ref, s_ref, mt_ref, m_ref, l_ref, acc_ref,
                 *, L, tk, n_full, tail, tail_pad):
    has_tail = tail > 0
    n_stage = n_full + (1 if has_tail else 0)
    off = 1 if has_tail else 0
    tail_start = n_full * tk

    def padded_tail(ref):
        rows = ref[0, tail_start:L, :]
        if tail_pad == tail:
            return rows
        return jnp.concatenate([rows, jnp.zeros((tail_pad - tail, HEAD_DIM), rows.dtype)], axis=0)

    @pl.when(pl.program_id(2) == 0)
    def _():
        for j in range(n_full):
            vt_ref[:, j * tk:(j + 1) * tk] = v_ref[0, j * tk:(j + 1) * tk, :].astype(F32).T.astype(BF16)
        if has_tail:
            vt_ref[:, tail_start:tail_start + tail_pad] = padded_tail(v_ref).astype(F32).T.astype(BF16)

    m_ref[...] = jnp.full(m_ref.shape, MASK_VALUE, F32)
    l_ref[...] = jnp.zeros(l_ref.shape, F32)
    acc_ref[...] = jnp.zeros(acc_ref.shape, F32)
    for g in range(GQA_GROUP):
        qt_ref[g] = q_ref[0, :, g * HEAD_DIM:(g + 1) * HEAD_DIM].astype(F32).T.astype(BF16)

    def is_tail(t):
        return has_tail and isinstance(t, int) and t == 0

    def tile_start(t):
        if isinstance(t, int):
            return tail_start if is_tail(t) else (t - off) * tk
        return pl.multiple_of((t - off) * tk, tk)

    def scores(t, slot):
        rows = tail_pad if is_tail(t) else tk
        kj = padded_tail(k_ref) if is_tail(t) else k_ref[0, pl.ds(tile_start(t), tk), :]
        for g in range(GQA_GROUP):
            st = jnp.dot(kj, qt_ref[g], preferred_element_type=F32)
            if is_tail(t) and tail_pad != tail:
                key = lax.broadcasted_iota(jnp.int32, st.shape, 0)
                st = jnp.where(key < tail, st, MASK_VALUE)
            s_ref[slot, g, 0:rows, :] = st
            mt_ref[slot, g] = jnp.max(st, axis=0, keepdims=True)

    def softmax_pv(t, slot):
        rows = tail_pad if is_tail(t) else tk
        vtj = vt_ref[:, pl.ds(tile_start(t), rows)]
        for g in range(GQA_GROUP):
            st = s_ref[slot, g, 0:rows, :]
            m_prev = m_ref[g]
            m_new = jnp.maximum(m_prev, mt_ref[slot, g])
            alpha = jnp.exp2(m_prev - m_new)
            p = jnp.exp2(st - m_new)
            l_ref[g] = alpha * l_ref[g] + jnp.sum(p, axis=0, keepdims=True)
            acc_ref[g] = alpha * acc_ref[g] + jnp.dot(vtj, p.astype(BF16), preferred_element_type=F32)
            m_ref[g] = m_new

    def stage(t, slot):
        if not isinstance(t, int) or t + 1 < n_stage:
            scores(t + 1, 1 - slot)
        softmax_pv(t, slot)

    scores(0, 0)
    stage(0, 0)
    n_pairs = max(0, (n_stage - 2) // 2)
    if n_pairs > 0:
        def body(p, carry):
            stage(1 + 2 * p, 1)
            stage(2 + 2 * p, 0)
            return carry
        lax.fori_loop(0, n_pairs, body, 0)
    for t in range(1 + 2 * n_pairs, n_stage):
        stage(t, t % 2)
    for g in range(GQA_GROUP):
        c = g * HEAD_DIM
        o_ref[0, :, c:c + HEAD_DIM] = (acc_ref[g] / l_ref[g]).T.astype(BF16)


def _attention(q, k, v, *, tq, tk):
    B, L, _ = q.shape
    gw = GQA_GROUP * HEAD_DIM
    n_full = L // tk
    tail = L - n_full * tk
    tail_pad = _round_up(tail, LANES)
    kern = functools.partial(_attn_kernel, L=L, tk=tk, n_full=n_full, tail=tail, tail_pad=tail_pad)
    kv_spec = pl.BlockSpec((1, L, HEAD_DIM), lambda b, h, i: (b, 0, h))
    return pl.pallas_call(
        kern,
        grid=(B, N_KV_HEADS, pl.cdiv(L, tq)),
        in_specs=[pl.BlockSpec((1, tq, gw), lambda b, h, i: (b, i, h)), kv_spec, kv_spec],
        out_specs=pl.BlockSpec((1, tq, gw), lambda b, h, i: (b, i, h)),
        out_shape=jax.ShapeDtypeStruct((B, L, ATTN_WIDTH), BF16),
        scratch_shapes=[
            pltpu.VMEM((HEAD_DIM, n_full * tk + tail_pad), BF16),
            pltpu.VMEM((GQA_GROUP, HEAD_DIM, tq), BF16),
            pltpu.VMEM((2, GQA_GROUP, max(tk, tail_pad), tq), F32),
            pltpu.VMEM((2, GQA_GROUP, 1, tq), F32),
            pltpu.VMEM((GQA_GROUP, 1, tq), F32),
            pltpu.VMEM((GQA_GROUP, 1, tq), F32),
            pltpu.VMEM((GQA_GROUP, HEAD_DIM, tq), F32),
        ],
        compiler_params=_cparams("parallel", "parallel", "arbitrary"),
        name="attention",
    )(q, k, v)


def _fourier_kernel(f_ref, cs_ref, f1r_ref, f1i_ref, g_ref, o_ref, wr_ref, wi_ref, os_ref,
                    *, L, n1, n1p, chunk):
    n_chunks = L // chunk
    last = L - chunk
    split = chunk - N_META

    def channel_dft_rows(r0, rows):
        return jnp.dot(f_ref[0, pl.ds(r0, rows), :], cs_ref[...], preferred_element_type=F32)

    def channel_dft(c, carry):
        r0 = pl.multiple_of(c * chunk, PACK_ROWS)
        z = channel_dft_rows(r0, chunk)
        wr_ref[pl.ds(r0 + N_META, chunk), :] = z[:, :FOURIER_GROUP_CH]
        wi_ref[pl.ds(r0 + N_META, chunk), :] = z[:, FOURIER_GROUP_CH:]
        return carry

    lax.fori_loop(0, n_chunks - 1, channel_dft, 0, unroll=4 if (n_chunks - 1) % 4 == 0 else 1)
    z = channel_dft_rows(last, chunk)
    wr_ref[last + N_META:L, :] = z[:split, :FOURIER_GROUP_CH]
    wi_ref[last + N_META:L, :] = z[:split, FOURIER_GROUP_CH:]
    wr_ref[0:N_META, :] = z[split:, :FOURIER_GROUP_CH]
    wi_ref[0:N_META, :] = z[split:, FOURIER_GROUP_CH:]
    if n1p > n1:
        zeros = jnp.zeros(((n1p - n1) * DFT_N2, FOURIER_GROUP_CH), F32)
        wr_ref[L:, :] = zeros
        wi_ref[L:, :] = zeros

    def row_dft(n2, carry):
        rows = pl.ds(n2, n1p, stride=DFT_N2)
        zr = wr_ref[rows, :].astype(BF16)
        zi = wi_ref[rows, :].astype(BF16)
        a = (jnp.dot(f1r_ref[...], zr, preferred_element_type=F32)
             + jnp.dot(f1i_ref[...], zi, preferred_element_type=F32))
        wr_ref[rows, :] = a[:n1p]
        wi_ref[rows, :] = a[n1p:]
        return carry

    lax.fori_loop(0, DFT_N2, row_dft, 0, unroll=16)

    def col_dft(k1, carry):
        r0 = pl.multiple_of(k1 * DFT_N2, DFT_N2)
        a = jnp.concatenate([wr_ref[pl.ds(r0, DFT_N2), :].astype(BF16),
                             wi_ref[pl.ds(r0, DFT_N2), :].astype(BF16)], axis=0)
        os_ref[pl.ds(k1, DFT_N2, stride=n1), :] = jnp.dot(g_ref[k1], a, preferred_element_type=F32)
        return carry

    lax.fori_loop(0, n1, col_dft, 0, unroll=(41 if n1 % 41 == 0 else 5) if n1 % 5 == 0 else 1)

    def emit(c, carry):
        r0 = pl.multiple_of(c * chunk, PACK_ROWS)
        o_ref[0, pl.ds(r0, chunk), :] = os_ref[pl.ds(r0 + N_META, chunk), :].astype(BF16)
        return carry

    lax.fori_loop(0, n_chunks - 1, emit, 0)
    o_ref[0, last:L - N_META, :] = os_ref[last + N_META:L, :].astype(BF16)
    o_ref[0, L - N_META:L, :] = os_ref[0:N_META, :].astype(BF16)


def _fourier(f, consts, *, chunk):
    B, L, _ = f.shape
    n1 = L // DFT_N2
    n1p = _round_up(n1, 8)
    cs, f1r, f1i, g = consts
    slab = pl.BlockSpec((1, L, FOURIER_GROUP_CH), lambda b, c: (b, 0, c))
    fixed2 = lambda b, c: (0, 0)
    return pl.pallas_call(
        functools.partial(_fourier_kernel, L=L, n1=n1, n1p=n1p, chunk=chunk),
        grid=(B, N_FOURIER_GROUPS),
        in_specs=[
            slab,
            pl.BlockSpec((FOURIER_GROUP_CH, 2 * FOURIER_GROUP_CH), fixed2),
            pl.BlockSpec((2 * n1p, n1p), fixed2),
            pl.BlockSpec((2 * n1p, n1p), fixed2),
            pl.BlockSpec((n1, DFT_N2, 2 * DFT_N2), lambda b, c: (0, 0, 0), pipeline_mode=pl.Buffered(1)),
        ],
        out_specs=slab,
        out_shape=jax.ShapeDtypeStruct((B, L, FOURIER_WIDTH), BF16),
        scratch_shapes=[pltpu.VMEM((n1p * DFT_N2, FOURIER_GROUP_CH), F32)] * 2
        + [pltpu.VMEM((L, FOURIER_GROUP_CH), F32)],
        compiler_params=_cparams("parallel", "parallel"),
        name="fourier",
    )(f, cs, f1r, f1i, g)


def _dft_constants(L):
    n1 = L // DFT_N2
    n1p = _round_up(n1, 8)
    k1 = np.arange(n1)
    ang1 = 2.0 * np.pi * np.outer(k1, k1) / n1
    c1, s1 = np.cos(ang1), np.sin(ang1)
    f1r = np.zeros((2 * n1p, n1p))
    f1i = np.zeros((2 * n1p, n1p))
    f1r[:n1, :n1] = c1
    f1r[n1p:n1p + n1, :n1] = -s1
    f1i[:n1, :n1] = s1
    f1i[n1p:n1p + n1, :n1] = c1
    n2 = np.arange(DFT_N2)
    k = k1[:, None, None] + n1 * n2[None, :, None]
    ang = 2.0 * np.pi * ((k * n2[None, None, :]) % L) / L
    g = np.concatenate([np.cos(ang), np.sin(ang)], axis=2) / math.sqrt(L)
    c = np.arange(FOURIER_GROUP_CH)
    angc = 2.0 * np.pi * np.outer(c, c) / FOURIER_GROUP_CH
    cs = np.concatenate([np.cos(angc), -np.sin(angc)], axis=1) / math.sqrt(FOURIER_GROUP_CH)
    return (jnp.asarray(cs, BF16), jnp.asarray(f1r, BF16), jnp.asarray(f1i, BF16), jnp.asarray(g, BF16))


def _mix_kernel(x_ref, meta_ref, h_ref, attn_ref, four_ref, wga_ref, wgf_ref, bga_ref, bgf_ref,
                wa_ref, wf_ref, wo_ref, o_ref, *, raw_input, tiles_per_batch):
    @pl.when(pl.program_id(1) == 0)
    def _():
        if raw_input:
            split = x_ref.shape[0] - N_META
            o_ref[:split, :] = x_ref[:split, :]
            o_ref[split:, :] = _meta_rows(x_ref, meta_ref, tiles_per_batch)
        else:
            o_ref[...] = x_ref[...]

    h = h_ref[...]
    g_attn = jax.nn.sigmoid(jnp.dot(h, wga_ref[...], preferred_element_type=F32) + bga_ref[...])
    g_four = jax.nn.sigmoid(jnp.dot(h, wgf_ref[...], preferred_element_type=F32) + bgf_ref[...])
    a_br = jnp.dot(attn_ref[...], wa_ref[...], preferred_element_type=F32)
    s_br = jnp.dot(four_ref[...], wf_ref[...], preferred_element_type=F32)
    merged = (g_attn * a_br + g_four * s_br).astype(BF16)
    o_ref[...] += jnp.dot(merged, wo_ref[...], preferred_element_type=F32)


def _mix(xs, meta, h, attn, four, w_gates, b_gates, w_a, w_f, w_o, *, tm, tn, tiles_per_batch, raw_input):
    R = h.shape[0]
    nj = D_MODEL // tn
    row = lambda i, j: (i, 0)
    return pl.pallas_call(
        functools.partial(_mix_kernel, raw_input=raw_input, tiles_per_batch=tiles_per_batch),
        grid=(R // tm, nj),
        in_specs=[
            pl.BlockSpec((pl.Squeezed(), tm, D_MODEL),
                         lambda i, j: (i // tiles_per_batch, i % tiles_per_batch, 0)),
            pl.BlockSpec((N_META, D_MODEL), lambda i, j: (0, 0)),
            pl.BlockSpec((tm, D_MODEL), row),
            pl.BlockSpec((tm, ATTN_WIDTH), row),
            pl.BlockSpec((tm, FOURIER_WIDTH), row),
            pl.BlockSpec((D_MODEL, tn), lambda i, j: (0, j)),
            pl.BlockSpec((D_MODEL, tn), lambda i, j: (0, j + nj)),
            pl.BlockSpec((1, tn), lambda i, j: (0, j)),
            pl.BlockSpec((1, tn), lambda i, j: (0, j + nj)),
            pl.BlockSpec((ATTN_WIDTH, tn), lambda i, j: (0, j)),
            pl.BlockSpec((FOURIER_WIDTH, tn), lambda i, j: (0, j)),
            pl.BlockSpec((tn, D_MODEL), lambda i, j: (j, 0)),
        ],
        out_specs=pl.BlockSpec((tm, D_MODEL), row),
        out_shape=jax.ShapeDtypeStruct((R, D_MODEL), F32),
        compiler_params=_cparams("parallel", "arbitrary"),
        name="mix_out",
    )(xs, meta, h, attn, four, w_gates, w_gates, b_gates, b_gates, w_a, w_f, w_o)


def _ffn_kernel(x_ref, xp_ref, xn_ref, g_ref, wg_ref, wv_ref, wc_ref, bc_ref, wd_ref,
                o_ref, h_ref, *, tm, tiles_per_batch):
    j = pl.program_id(1)
    H = CONV_HALO

    @pl.when(j == 0)
    def _():
        x = x_ref[...]
        g = g_ref[...]
        h_ref[H:H + tm, :] = (_rms(x) * g).astype(BF16)
        h_ref[0:H, :] = (_rms(xp_ref[...]) * g).astype(BF16)
        h_ref[H + tm:, :] = (_rms(xn_ref[...]) * g).astype(BF16)
        o_ref[...] = x

    h = h_ref[...]
    up_g = jnp.dot(h, wg_ref[...], preferred_element_type=F32)
    n = tm + 2 * H
    wc = wc_ref[...]
    prev = pltpu.roll(up_g, 1, 0)[H:H + tm]
    nxt = pltpu.roll(up_g, n - 1, 0)[H:H + tm]
    row = lax.broadcasted_iota(jnp.int32, (tm, 1), 0)
    seam = jnp.where(pl.program_id(0) % tiles_per_batch == tiles_per_batch - 1, tm - N_META, -1)
    prev = jnp.where(row == seam, 0.0, prev)
    nxt = jnp.where(row == seam - 1, 0.0, nxt)
    u = prev * wc[0:1] + up_g[H:H + tm] * wc[1:2] + nxt * wc[2:3] + bc_ref[...]
    val = jnp.dot(h[H:H + tm], wv_ref[...], preferred_element_type=F32)
    act = (u * jax.nn.sigmoid(u) * val).astype(BF16)
    o_ref[...] += jnp.dot(act, wd_ref[...], preferred_element_type=F32)


def _ffn(hs, gain, w_up, w_conv, b_conv, w_down, *, tm, tf, tiles_per_batch, n_tok_out=None):
    R = hs.shape[0]
    nj = D_FF // tf
    hb = tm // CONV_HALO
    nb = tiles_per_batch * hb
    row = lambda i, j: (i, 0)

    def prev_block(i, j):
        first = i % tiles_per_batch == 0
        return (jnp.where(first, (i // tiles_per_batch) * nb + nb - 1, i * hb - 1), 0)

    def next_block(i, j):
        last = i % tiles_per_batch == tiles_per_batch - 1
        return (jnp.where(last, (i // tiles_per_batch) * nb, (i + 1) * hb), 0)

    if n_tok_out is None:
        out_spec = pl.BlockSpec((tm, D_MODEL), row)
        out_shape = jax.ShapeDtypeStruct((R, D_MODEL), F32)
    else:
        out_spec = pl.BlockSpec((pl.Squeezed(), tm, D_MODEL),
                                lambda i, j: (i // tiles_per_batch, i % tiles_per_batch, 0))
        out_shape = jax.ShapeDtypeStruct((R // (tiles_per_batch * tm), n_tok_out, D_MODEL), F32)
    return pl.pallas_call(
        functools.partial(_ffn_kernel, tm=tm, tiles_per_batch=tiles_per_batch),
        grid=(R // tm, nj),
        in_specs=[
            pl.BlockSpec((tm, D_MODEL), row),
            pl.BlockSpec((CONV_HALO, D_MODEL), prev_block),
            pl.BlockSpec((CONV_HALO, D_MODEL), next_block),
            pl.BlockSpec((1, D_MODEL), lambda i, j: (0, 0)),
            pl.BlockSpec((D_MODEL, tf), lambda i, j: (0, j)),
            pl.BlockSpec((D_MODEL, tf), lambda i, j: (0, j + nj)),
            pl.BlockSpec((3, tf), lambda i, j: (0, j)),
            pl.BlockSpec((1, tf), lambda i, j: (0, j)),
            pl.BlockSpec((tf, D_MODEL), lambda i, j: (j, 0)),
        ],
        out_specs=out_spec,
        out_shape=out_shape,
        scratch_shapes=[pltpu.VMEM((tm + 2 * CONV_HALO, D_MODEL), BF16)],
        compiler_params=_cparams("parallel", "arbitrary"),
        name="ffn",
    )(hs, hs, hs, gain, w_up, w_up, w_conv, b_conv, w_down)


def _rope_tables(n_tok):
    rows = n_tok // GRID_W
    pos_r = np.concatenate([np.repeat(np.arange(rows), GRID_W), np.zeros(N_META)])
    pos_c = np.concatenate([np.tile(np.arange(GRID_W), rows), np.zeros(N_META)])
    inv_freq = 1.0 / (ROPE_THETA ** (np.arange(ROPE_PAIRS) / ROPE_PAIRS))
    ang_r = pos_r[:, None] * inv_freq[None, :]
    ang_c = pos_c[:, None] * inv_freq[None, :]
    cos_t = np.concatenate([np.cos(ang_r), np.cos(ang_r), np.cos(ang_c), np.cos(ang_c)], axis=-1)
    sin_t = np.concatenate([-np.sin(ang_r), np.sin(ang_r), -np.sin(ang_c), np.sin(ang_c)], axis=-1)
    return jnp.asarray(cos_t, F32), jnp.asarray(sin_t, F32)


def _tiles(L):
    tm = L // ROW_TILES_PER_BATCH
    return dict(tm=tm, tq=256, tk=2560, tn_mix=512, tf=512)


def kernel(x, meta_tokens, norm_mix, norm_ffn, w_in, b_gate, q_norm, k_norm, w_attn_br, w_four, w_out,
           w_up, w_conv, b_conv, w_down):
    B, n_tok, D = x.shape
    depth = w_in.shape[0]
    L = n_tok + N_META
    t = _tiles(L)
    tm = t["tm"]
    assert D == D_MODEL and n_tok % GRID_W == 0 and L % DFT_N2 == 0
    assert L % tm == 0 and tm % PACK_ROWS == 0
    R = B * L
    tiles_per_batch = L // tm

    cos_t, sin_t = _rope_tables(n_tok)
    dft = _dft_constants(L)

    meta = meta_tokens.astype(x.dtype)
    xs = x
    for i in range(depth):
        raw = i == 0
        w_qkvf = w_in[i, :, :QKVF_WIDTH].astype(BF16)
        w_gates = w_in[i, :, QKVF_WIDTH:].astype(BF16)
        q, k, v, f, h = _inproj(xs, meta, norm_mix[i][None], w_qkvf, q_norm[i][None], k_norm[i][None],
                                cos_t, sin_t, tm=tm, tiles_per_batch=tiles_per_batch, raw_input=raw)
        attn = _attention(q.reshape(B, L, ATTN_WIDTH), k.reshape(B, L, KV_WIDTH), v.reshape(B, L, KV_WIDTH),
                          tq=t["tq"], tk=t["tk"]).reshape(R, ATTN_WIDTH)
        four = _fourier(f.reshape(B, L, FOURIER_WIDTH), dft, chunk=tm).reshape(R, FOURIER_WIDTH)
        hs = _mix(xs, meta, h, attn, four, w_gates, b_gate[i][None],
                  w_attn_br[i].astype(BF16), w_four[i].astype(BF16), w_out[i].astype(BF16),
                  tm=tm, tn=t["tn_mix"], tiles_per_batch=tiles_per_batch, raw_input=raw)
        hs = _ffn(hs, norm_ffn[i][None], w_up[i].astype(BF16), w_conv[i], b_conv[i][None],
                  w_down[i].astype(BF16), tm=tm, tf=t["tf"], tiles_per_batch=tiles_per_batch,
                  n_tok_out=n_tok if i == depth - 1 else None)
        xs = hs.reshape(B, L, D) if i < depth - 1 else hs
    return xs
```

```python
import functools
import math

import jax
import jax.numpy as jnp
import numpy as np
from jax import lax
from jax.experimental import pallas as pl
from jax.experimental.pallas import tpu as pltpu

F32 = jnp.float32
BF16 = jnp.bfloat16

D_MODEL = 2048
N_HEADS = 8
N_KV_HEADS = 2
HEAD_DIM = 128
GQA_GROUP = N_HEADS // N_KV_HEADS
ATTN_WIDTH = N_HEADS * HEAD_DIM
KV_WIDTH = N_KV_HEADS * HEAD_DIM
N_FOURIER_GROUPS = 8
FOURIER_GROUP_CH = 128
FOURIER_WIDTH = N_FOURIER_GROUPS * FOURIER_GROUP_CH
QKVF_WIDTH = ATTN_WIDTH + 2 * KV_WIDTH + FOURIER_WIDTH
OFF_K = ATTN_WIDTH
OFF_V = OFF_K + KV_WIDTH
OFF_F = OFF_V + KV_WIDTH
D_FF = 5632
N_META = 16
GRID_W = 64
NORM_EPS = 1e-6
ROPE_THETA = 10000.0
ROPE_PAIRS = HEAD_DIM // 4

DFT_N2 = 80
ROW_TILES_PER_BATCH = 25
PACK_ROWS = 16
LANES = 128
CONV_HALO = PACK_ROWS
MASK_VALUE = -1e30
VMEM_LIMIT = 56 * 1024 * 1024


def _cparams(*sem):
    return pltpu.CompilerParams(dimension_semantics=sem, vmem_limit_bytes=VMEM_LIMIT)


def _rms(x):
    return x * lax.rsqrt(jnp.mean(x * x, axis=-1, keepdims=True) + NORM_EPS)


def _round_up(n, m):
    return -(-n // m) * m


def _meta_rows(x_ref, meta_ref, tiles_per_batch):
    split = x_ref.shape[0] - N_META
    last = pl.program_id(0) % tiles_per_batch == tiles_per_batch - 1
    return jnp.where(last, meta_ref[...], x_ref[split:, :])


def _inproj_kernel(x_ref, meta_ref, g_ref, w_ref, qn_ref, kn_ref, cos_ref, sin_ref,
                   q_ref, k_ref, v_ref, f_ref, h_ref, *, raw_input, tiles_per_batch):
    g = g_ref[...]
    if raw_input:
        split = x_ref.shape[0] - N_META
        h_ref[:split, :] = (_rms(x_ref[:split, :]) * g).astype(BF16)
        h_ref[split:, :] = (_rms(_meta_rows(x_ref, meta_ref, tiles_per_batch)) * g).astype(BF16)
    else:
        h_ref[...] = (_rms(x_ref[...]) * g).astype(BF16)
    proj = jnp.dot(h_ref[...], w_ref[...], preferred_element_type=F32)
    cos = cos_ref[...]
    sin = sin_ref[...]
    lane = lax.broadcasted_iota(jnp.int32, cos.shape, 1)
    first_half = (lane % (HEAD_DIM // 2)) < ROPE_PAIRS

    def norm_rope(t, gain):
        t = _rms(t) * gain
        partner = jnp.where(first_half,
                            pltpu.roll(t, HEAD_DIM - ROPE_PAIRS, 1),
                            pltpu.roll(t, ROPE_PAIRS, 1))
        return t * cos + partner * sin

    scale = math.log2(math.e) / math.sqrt(HEAD_DIM)
    for hh in range(N_HEADS):
        c = hh * HEAD_DIM
        q_ref[:, c:c + HEAD_DIM] = (norm_rope(proj[:, c:c + HEAD_DIM], qn_ref[...]) * scale).astype(BF16)
    for hh in range(N_KV_HEADS):
        c = hh * HEAD_DIM
        k_ref[:, c:c + HEAD_DIM] = norm_rope(proj[:, OFF_K + c:OFF_K + c + HEAD_DIM], kn_ref[...]).astype(BF16)
    v_ref[...] = proj[:, OFF_V:OFF_F].astype(BF16)
    f_ref[...] = proj[:, OFF_F:].astype(BF16)


def _inproj(xs, meta, gain, w_qkvf, qn, kn, cos_t, sin_t, *, tm, tiles_per_batch, raw_input):
    R = xs.shape[0] * tiles_per_batch * tm
    row = lambda i: (i, 0)
    fixed = lambda i: (0, 0)
    tab = lambda i: (i % tiles_per_batch, 0)
    widths = (ATTN_WIDTH, KV_WIDTH, KV_WIDTH, FOURIER_WIDTH, D_MODEL)
    return pl.pallas_call(
        functools.partial(_inproj_kernel, raw_input=raw_input, tiles_per_batch=tiles_per_batch),
        grid=(R // tm,),
        in_specs=[
            pl.BlockSpec((pl.Squeezed(), tm, D_MODEL), lambda i: (i // tiles_per_batch, i % tiles_per_batch, 0)),
            pl.BlockSpec((N_META, D_MODEL), fixed),
            pl.BlockSpec((1, D_MODEL), fixed),
            pl.BlockSpec((D_MODEL, QKVF_WIDTH), fixed),
            pl.BlockSpec((1, HEAD_DIM), fixed),
            pl.BlockSpec((1, HEAD_DIM), fixed),
            pl.BlockSpec((tm, HEAD_DIM), tab),
            pl.BlockSpec((tm, HEAD_DIM), tab),
        ],
        out_specs=[pl.BlockSpec((tm, w), row) for w in widths],
        out_shape=[jax.ShapeDtypeStruct((R, w), BF16) for w in widths],
        compiler_params=_cparams("parallel"),
        name="inproj",
    )(xs, meta, gain, w_qkvf, qn, kn, cos_t, sin_t)


def _attn_kernel(q_ref, k_ref, v_ref, o_ref, vt_ref, qt_ref, s_ref, mt_ref, m_ref, l_ref, acc_ref,
                 *, L, tk, n_full, tail, tail_pad, exp16):
    has_tail = tail > 0
    n_stage = n_full + (1 if has_tail else 0)
    off = 1 if has_tail else 0
    tail_start = n_full * tk

    def padded_tail(ref):
        rows = ref[0, tail_start:L, :]
        if tail_pad == tail:
            return rows
        return jnp.concatenate([rows, jnp.zeros((tail_pad - tail, HEAD_DIM), rows.dtype)], axis=0)

    @pl.when(pl.program_id(2) == 0)
    def _():
        for j in range(n_full):
            vt_ref[:, j * tk:(j + 1) * tk] = v_ref[0, j * tk:(j + 1) * tk, :].astype(F32).T.astype(BF16)
        if has_tail:
            vt_ref[:, tail_start:tail_start + tail_pad] = padded_tail(v_ref).astype(F32).T.astype(BF16)

    m_ref[...] = jnp.full(m_ref.shape, MASK_VALUE, F32)
    l_ref[...] = jnp.zeros(l_ref.shape, F32)
    acc_ref[...] = jnp.zeros(acc_ref.shape, F32)
    for g in range(GQA_GROUP):
        qt_ref[g] = q_ref[0, :, g * HEAD_DIM:(g + 1) * HEAD_DIM].astype(F32).T.astype(BF16)

    def is_tail(t):
        return has_tail and isinstance(t, int) and t == 0

    def tile_start(t):
        if isinstance(t, int):
            return tail_start if is_tail(t) else (t - off) * tk
        return pl.multiple_of((t - off) * tk, tk)

    def scores(t, slot):
        rows = tail_pad if is_tail(t) else tk
        kj = padded_tail(k_ref) if is_tail(t) else k_ref[0, pl.ds(tile_start(t), tk), :]
        for g in range(GQA_GROUP):
            st = jnp.dot(kj, qt_ref[g], preferred_element_type=F32)
            if is_tail(t) and tail_pad != tail:
                key = lax.broadcasted_iota(jnp.int32, st.shape, 0)
                st = jnp.where(key < tail, st, MASK_VALUE)
            s_ref[slot, g, 0:rows, :] = st
            mt_ref[slot, g] = jnp.max(st, axis=0, keepdims=True)

    def softmax_pv(t, slot):
        rows = tail_pad if is_tail(t) else tk
        vtj = vt_ref[:, pl.ds(tile_start(t), rows)]
        for g in range(GQA_GROUP):
            st = s_ref[slot, g, 0:rows, :]
            m_prev = m_ref[g]
            m_new = jnp.maximum(m_prev, mt_ref[slot, g])
            alpha = jnp.exp2(m_prev - m_new)
            if exp16:
                p16 = jnp.exp2((st - m_new).astype(BF16))
                l_ref[g] = alpha * l_ref[g] + jnp.sum(p16.astype(F32), axis=0, keepdims=True)
                acc_ref[g] = alpha * acc_ref[g] + jnp.dot(vtj, p16, preferred_element_type=F32)
            else:
                p = jnp.exp2(st - m_new)
                l_ref[g] = alpha * l_ref[g] + jnp.sum(p, axis=0, keepdims=True)
                acc_ref[g] = alpha * acc_ref[g] + jnp.dot(vtj, p.astype(BF16), preferred_element_type=F32)
            m_ref[g] = m_new

    def stage(t, slot):
        if not isinstance(t, int) or t + 1 < n_stage:
            scores(t + 1, 1 - slot)
        softmax_pv(t, slot)

    scores(0, 0)
    stage(0, 0)
    n_pairs = max(0, (n_stage - 2) // 2)
    if n_pairs > 0:
        def body(p, carry):
            stage(1 + 2 * p, 1)
            stage(2 + 2 * p, 0)
            return carry
        lax.fori_loop(0, n_pairs, body, 0)
    for t in range(1 + 2 * n_pairs, n_stage):
        stage(t, t % 2)
    for g in range(GQA_GROUP):
        c = g * HEAD_DIM
        o_ref[0, :, c:c + HEAD_DIM] = (acc_ref[g] / l_ref[g]).T.astype(BF16)


def _attention(q, k, v, *, tq, tk, exp16):
    B, L, _ = q.shape
    gw = GQA_GROUP * HEAD_DIM
    n_full = L // tk
    tail = L - n_full * tk
    tail_pad = _round_up(tail, LANES)
    kern = functools.partial(_attn_kernel, L=L, tk=tk, n_full=n_full, tail=tail, tail_pad=tail_pad, exp16=exp16)
    kv_spec = pl.BlockSpec((1, L, HEAD_DIM), lambda b, h, i: (b, 0, h))
    return pl.pallas_call(
        kern,
        grid=(B, N_KV_HEADS, pl.cdiv(L, tq)),
        in_specs=[pl.BlockSpec((1, tq, gw), lambda b, h, i: (b, i, h)), kv_spec, kv_spec],
        out_specs=pl.BlockSpec((1, tq, gw), lambda b, h, i: (b, i, h)),
        out_shape=jax.ShapeDtypeStruct((B, L, ATTN_WIDTH), BF16),
        scratch_shapes=[
            pltpu.VMEM((HEAD_DIM, n_full * tk + tail_pad), BF16),
            pltpu.VMEM((GQA_GROUP, HEAD_DIM, tq), BF16),
            pltpu.VMEM((2, GQA_GROUP, max(tk, tail_pad), tq), F32),
            pltpu.VMEM((2, GQA_GROUP, 1, tq), F32),
            pltpu.VMEM((GQA_GROUP, 1, tq), F32),
            pltpu.VMEM((GQA_GROUP, 1, tq), F32),
            pltpu.VMEM((GQA_GROUP, HEAD_DIM, tq), F32),
        ],
        compiler_params=_cparams("parallel", "parallel", "arbitrary"),
        name="attention",
    )(q, k, v)


def _fourier_kernel(f_ref, cs_ref, f1r_ref, f1i_ref, g_ref, o_ref, wr_ref, wi_ref, os_ref,
                    *, L, n1, n1p, chunk):
    n_chunks = L // chunk
    last = L - chunk
    split = chunk - N_META

    def channel_dft_rows(r0, rows):
        return jnp.dot(f_ref[0, pl.ds(r0, rows), :], cs_ref[...], preferred_element_type=F32)

    def channel_dft(c, carry):
        r0 = pl.multiple_of(c * chunk, PACK_ROWS)
        z = channel_dft_rows(r0, chunk)
        wr_ref[pl.ds(r0 + N_META, chunk), :] = z[:, :FOURIER_GROUP_CH]
        wi_ref[pl.ds(r0 + N_META, chunk), :] = z[:, FOURIER_GROUP_CH:]
        return carry

    lax.fori_loop(0, n_chunks - 1, channel_dft, 0, unroll=4 if (n_chunks - 1) % 4 == 0 else 1)
    z = channel_dft_rows(last, chunk)
    wr_ref[last + N_META:L, :] = z[:split, :FOURIER_GROUP_CH]
    wi_ref[last + N_META:L, :] = z[:split, FOURIER_GROUP_CH:]
    wr_ref[0:N_META, :] = z[split:, :FOURIER_GROUP_CH]
    wi_ref[0:N_META, :] = z[split:, FOURIER_GROUP_CH:]
    if n1p > n1:
        zeros = jnp.zeros(((n1p - n1) * DFT_N2, FOURIER_GROUP_CH), F32)
        wr_ref[L:, :] = zeros
        wi_ref[L:, :] = zeros

    def row_dft(n2, carry):
        rows = pl.ds(n2, n1p, stride=DFT_N2)
        zr = wr_ref[rows, :].astype(BF16)
        zi = wi_ref[rows, :].astype(BF16)
        a = (jnp.dot(f1r_ref[...], zr, preferred_element_type=F32)
             + jnp.dot(f1i_ref[...], zi, preferred_element_type=F32))
        wr_ref[rows, :] = a[:n1p]
        wi_ref[rows, :] = a[n1p:]
        return carry

    lax.fori_loop(0, DFT_N2, row_dft, 0, unroll=16)

    def col_dft(k1, carry):
        r0 = pl.multiple_of(k1 * DFT_N2, DFT_N2)
        a = jnp.concatenate([wr_ref[pl.ds(r0, DFT_N2), :].astype(BF16),
                             wi_ref[pl.ds(r0, DFT_N2), :].astype(BF16)], axis=0)
        os_ref[pl.ds(k1, DFT_N2, stride=n1), :] = jnp.dot(g_ref[k1], a, preferred_element_type=F32)
        return carry

    lax.fori_loop(0, n1, col_dft, 0, unroll=(41 if n1 % 41 == 0 else 5) if n1 % 5 == 0 else 1)

    def emit(c, carry):
        r0 = pl.multiple_of(c * chunk, PACK_ROWS)
        o_ref[0, pl.ds(r0, chunk), :] = os_ref[pl.ds(r0 + N_META, chunk), :].astype(BF16)
        return carry

    lax.fori_loop(0, n_chunks - 1, emit, 0)
    o_ref[0, last:L - N_META, :] = os_ref[last + N_META:L, :].astype(BF16)
    o_ref[0, L - N_META:L, :] = os_ref[0:N_META, :].astype(BF16)


def _fourier(f, consts, *, chunk):
    B, L, _ = f.shape
    n1 = L // DFT_N2
    n1p = _round_up(n1, 8)
    cs, f1r, f1i, g = consts
    slab = pl.BlockSpec((1, L, FOURIER_GROUP_CH), lambda b, c: (b, 0, c))
    fixed2 = lambda b, c: (0, 0)
    return pl.pallas_call(
        functools.partial(_fourier_kernel, L=L, n1=n1, n1p=n1p, chunk=chunk),
        grid=(B, N_FOURIER_GROUPS),
        in_specs=[
            slab,
            pl.BlockSpec((FOURIER_GROUP_CH, 2 * FOURIER_GROUP_CH), fixed2),
            pl.BlockSpec((2 * n1p, n1p), fixed2),
            pl.BlockSpec((2 * n1p, n1p), fixed2),
            pl.BlockSpec((n1, DFT_N2, 2 * DFT_N2), lambda b, c: (0, 0, 0), pipeline_mode=pl.Buffered(1)),
        ],
        out_specs=slab,
        out_shape=jax.ShapeDtypeStruct((B, L, FOURIER_WIDTH), BF16),
        scratch_shapes=[pltpu.VMEM((n1p * DFT_N2, FOURIER_GROUP_CH), F32)] * 2
        + [pltpu.VMEM((L, FOURIER_GROUP_CH), F32)],
        compiler_params=_cparams("parallel", "parallel"),
        name="fourier",
    )(f, cs, f1r, f1i, g)


def _dft_constants(L):
    n1 = L // DFT_N2
    n1p = _round_up(n1, 8)
    k1 = np.arange(n1)
    ang1 = 2.0 * np.pi * np.outer(k1, k1) / n1
    c1, s1 = np.cos(ang1), np.sin(ang1)
    f1r = np.zeros((2 * n1p, n1p))
    f1i = np.zeros((2 * n1p, n1p))
    f1r[:n1, :n1] = c1
    f1r[n1p:n1p + n1, :n1] = -s1
    f1i[:n1, :n1] = s1
    f1i[n1p:n1p + n1, :n1] = c1
    n2 = np.arange(DFT_N2)
    k = k1[:, None, None] + n1 * n2[None, :, None]
    ang = 2.0 * np.pi * ((k * n2[None, None, :]) % L) / L
    g = np.concatenate([np.cos(ang), np.sin(ang)], axis=2) / math.sqrt(L)
    c = np.arange(FOURIER_GROUP_CH)
    angc = 2.0 * np.pi * np.outer(c, c) / FOURIER_GROUP_CH
    cs = np.concatenate([np.cos(angc), -np.sin(angc)], axis=1) / math.sqrt(FOURIER_GROUP_CH)
    return (jnp.asarray(cs, BF16), jnp.asarray(f1r, BF16), jnp.asarray(f1i, BF16), jnp.asarray(g, BF16))


def _mix_kernel(x_ref, meta_ref, h_ref, attn_ref, four_ref, wga_ref, wgf_ref, bga_ref, bgf_ref,
                wa_ref, wf_ref, wo_ref, o_ref, *, raw_input, tiles_per_batch):
    @pl.when(pl.program_id(1) == 0)
    def _():
        if raw_input:
            split = x_ref.shape[0] - N_META
            o_ref[:split, :] = x_ref[:split, :]
            o_ref[split:, :] = _meta_rows(x_ref, meta_ref, tiles_per_batch)
        else:
            o_ref[...] = x_ref[...]

    h = h_ref[...]
    g_attn = jax.nn.sigmoid(jnp.dot(h, wga_ref[...], preferred_element_type=F32) + bga_ref[...])
    g_four = jax.nn.sigmoid(jnp.dot(h, wgf_ref[...], preferred_element_type=F32) + bgf_ref[...])
    a_br = jnp.dot(attn_ref[...], wa_ref[...], preferred_element_type=F32)
    s_br = jnp.dot(four_ref[...], wf_ref[...], preferred_element_type=F32)
    merged = (g_attn * a_br + g_four * s_br).astype(BF16)
    o_ref[...] += jnp.dot(merged, wo_ref[...], preferred_element_type=F32)


def _mix(xs, meta, h, attn, four, w_gates, b_gates, w_a, w_f, w_o, *, tm, tn, tiles_per_batch, raw_input):
    R = h.shape[0]
    nj = D_MODEL // tn
    row = lambda i, j: (i, 0)
    return pl.pallas_call(
        functools.partial(_mix_kernel, raw_input=raw_input, tiles_per_batch=tiles_per_batch),
        grid=(R // tm, nj),
        in_specs=[
            pl.BlockSpec((pl.Squeezed(), tm, D_MODEL),
                         lambda i, j: (i // tiles_per_batch, i % tiles_per_batch, 0)),
            pl.BlockSpec((N_META, D_MODEL), lambda i, j: (0, 0)),
            pl.BlockSpec((tm, D_MODEL), row),
            pl.BlockSpec((tm, ATTN_WIDTH), row),
            pl.BlockSpec((tm, FOURIER_WIDTH), row),
            pl.BlockSpec((D_MODEL, tn), lambda i, j: (0, j)),
            pl.BlockSpec((D_MODEL, tn), lambda i, j: (0, j + nj)),
            pl.BlockSpec((1, tn), lambda i, j: (0, j)),
            pl.BlockSpec((1, tn), lambda i, j: (0, j + nj)),
            pl.BlockSpec((ATTN_WIDTH, tn), lambda i, j: (0, j)),
            pl.BlockSpec((FOURIER_WIDTH, tn), lambda i, j: (0, j)),
            pl.BlockSpec((tn, D_MODEL), lambda i, j: (j, 0)),
        ],
        out_specs=pl.BlockSpec((tm, D_MODEL), row),
        out_shape=jax.ShapeDtypeStruct((R, D_MODEL), F32),
        compiler_params=_cparams("parallel", "arbitrary"),
        name="mix_out",
    )(xs, meta, h, attn, four, w_gates, w_gates, b_gates, b_gates, w_a, w_f, w_o)


def _ffn_kernel(x_ref, xp_ref, xn_ref, g_ref, wg_ref, wv_ref, wc_ref, bc_ref, wd_ref,
                o_ref, h_ref, *, tm, tiles_per_batch):
    j = pl.program_id(1)
    H = CONV_HALO

    @pl.when(j == 0)
    def _():
        x = x_ref[...]
        g = g_ref[...]
        h_ref[H:H + tm, :] = (_rms(x) * g).astype(BF16)
        h_ref[0:H, :] = (_rms(xp_ref[...]) * g).astype(BF16)
        h_ref[H + tm:, :] = (_rms(xn_ref[...]) * g).astype(BF16)
        o_ref[...] = x

    h = h_ref[...]
    up_g = jnp.dot(h, wg_ref[...], preferred_element_type=F32)
    n = tm + 2 * H
    wc = wc_ref[...]
    prev = pltpu.roll(up_g, 1, 0)[H:H + tm]
    nxt = pltpu.roll(up_g, n - 1, 0)[H:H + tm]
    row = lax.broadcasted_iota(jnp.int32, (tm, 1), 0)
    seam = jnp.where(pl.program_id(0) % tiles_per_batch == tiles_per_batch - 1, tm - N_META, -1)
    prev = jnp.where(row == seam, 0.0, prev)
    nxt = jnp.where(row == seam - 1, 0.0, nxt)
    u = prev * wc[0:1] + up_g[H:H + tm] * wc[1:2] + nxt * wc[2:3] + bc_ref[...]
    val = jnp.dot(h[H:H + tm], wv_ref[...], preferred_element_type=F32)
    act = (u * jax.nn.sigmoid(u) * val).astype(BF16)
    o_ref[...] += jnp.dot(act, wd_ref[...], preferred_element_type=F32)


def _ffn(hs, gain, w_up, w_conv, b_conv, w_down, *, tm, tf, tiles_per_batch, n_tok_out=None):
    R = hs.shape[0]
    nj = D_FF // tf
    hb = tm // CONV_HALO
    nb = tiles_per_batch * hb
    row = lambda i, j: (i, 0)

    def prev_block(i, j):
        first = i % tiles_per_batch == 0
        return (jnp.where(first, (i // tiles_per_batch) * nb + nb - 1, i * hb - 1), 0)

    def next_block(i, j):
        last = i % tiles_per_batch == tiles_per_batch - 1
        return (jnp.where(last, (i // tiles_per_batch) * nb, (i + 1) * hb), 0)

    if n_tok_out is None:
        out_spec = pl.BlockSpec((tm, D_MODEL), row)
        out_shape = jax.ShapeDtypeStruct((R, D_MODEL), F32)
    else:
        out_spec = pl.BlockSpec((pl.Squeezed(), tm, D_MODEL),
                                lambda i, j: (i // tiles_per_batch, i % tiles_per_batch, 0))
        out_shape = jax.ShapeDtypeStruct((R // (tiles_per_batch * tm), n_tok_out, D_MODEL), F32)
    return pl.pallas_call(
        functools.partial(_ffn_kernel, tm=tm, tiles_per_batch=tiles_per_batch),
        grid=(R // tm, nj),
        in_specs=[
            pl.BlockSpec((tm, D_MODEL), row),
            pl.BlockSpec((CONV_HALO, D_MODEL), prev_block),
            pl.BlockSpec((CONV_HALO, D_MODEL), next_block),
            pl.BlockSpec((1, D_MODEL), lambda i, j: (0, 0)),
            pl.BlockSpec((D_MODEL, tf), lambda i, j: (0, j)),
            pl.BlockSpec((D_MODEL, tf), lambda i, j: (0, j + nj)),
            pl.BlockSpec((3, tf), lambda i, j: (0, j)),
            pl.BlockSpec((1, tf), lambda i, j: (0, j)),
            pl.BlockSpec((tf, D_MODEL), lambda i, j: (j, 0)),
        ],
        out_specs=out_spec,
        out_shape=out_shape,
        scratch_shapes=[pltpu.VMEM((tm + 2 * CONV_HALO, D_MODEL), BF16)],
        compiler_params=_cparams("parallel", "arbitrary"),
        name="ffn",
    )(hs, hs, hs, gain, w_up, w_up, w_conv, b_conv, w_down)


def _rope_tables(n_tok):
    rows = n_tok // GRID_W
    pos_r = np.concatenate([np.repeat(np.arange(rows), GRID_W), np.zeros(N_META)])
    pos_c = np.concatenate([np.tile(np.arange(GRID_W), rows), np.zeros(N_META)])
    inv_freq = 1.0 / (ROPE_THETA ** (np.arange(ROPE_PAIRS) / ROPE_PAIRS))
    ang_r = pos_r[:, None] * inv_freq[None, :]
    ang_c = pos_c[:, None] * inv_freq[None, :]
    cos_t = np.concatenate([np.cos(ang_r), np.cos(ang_r), np.cos(ang_c), np.cos(ang_c)], axis=-1)
    sin_t = np.concatenate([-np.sin(ang_r), np.sin(ang_r), -np.sin(ang_c), np.sin(ang_c)], axis=-1)
    return jnp.asarray(cos_t, F32), jnp.asarray(sin_t, F32)


def _tiles(L):
    tm = L // ROW_TILES_PER_BATCH
    return dict(tm=tm, tq=256, tk=2560, tn_mix=512, tf=512)


def kernel(x, meta_tokens, norm_mix, norm_ffn, w_in, b_gate, q_norm, k_norm, w_attn_br, w_four, w_out,
           w_up, w_conv, b_conv, w_down):
    B, n_tok, D = x.shape
    depth = w_in.shape[0]
    L = n_tok + N_META
    t = _tiles(L)
    tm = t["tm"]
    assert D == D_MODEL and n_tok % GRID_W == 0 and L % DFT_N2 == 0
    assert L % tm == 0 and tm % PACK_ROWS == 0
    R = B * L
    tiles_per_batch = L // tm

    cos_t, sin_t = _rope_tables(n_tok)
    dft = _dft_constants(L)

    meta = meta_tokens.astype(x.dtype)
    xs = x
    for i in range(depth):
        raw = i == 0
        w_qkvf = w_in[i, :, :QKVF_WIDTH].astype(BF16)
        w_gates = w_in[i, :, QKVF_WIDTH:].astype(BF16)
        q, k, v, f, h = _inproj(xs, meta, norm_mix[i][None], w_qkvf, q_norm[i][None], k_norm[i][None],
                                cos_t, sin_t, tm=tm, tiles_per_batch=tiles_per_batch, raw_input=raw)
        attn = _attention(q.reshape(B, L, ATTN_WIDTH), k.reshape(B, L, KV_WIDTH), v.reshape(B, L, KV_WIDTH),
                          tq=t["tq"], tk=t["tk"], exp16=(i == 1)).reshape(R, ATTN_WIDTH)
        four = _fourier(f.reshape(B, L, FOURIER_WIDTH), dft, chunk=tm).reshape(R, FOURIER_WIDTH)
        hs = _mix(xs, meta, h, attn, four, w_gates, b_gate[i][None],
                  w_attn_br[i].astype(BF16), w_four[i].astype(BF16), w_out[i].astype(BF16),
                  tm=tm, tn=t["tn_mix"], tiles_per_batch=tiles_per_batch, raw_input=raw)
        hs = _ffn(hs, norm_ffn[i][None], w_up[i].astype(BF16), w_conv[i], b_conv[i][None],
                  w_down[i].astype(BF16), tm=tm, tf=t["tf"], tiles_per_batch=tiles_per_batch,
                  n_tok_out=n_tok if i == depth - 1 else None)
        xs = hs.reshape(B, L, D) if i < depth - 1 else hs
    return xs
```

```python
import functools
import math

import jax
import jax.numpy as jnp
import numpy as np
from jax import lax
from jax.experimental import pallas as pl
from jax.experimental.pallas import tpu as pltpu

F32 = jnp.float32
BF16 = jnp.bfloat16

D_MODEL = 2048
N_HEADS = 8
N_KV_HEADS = 2
HEAD_DIM = 128
GQA_GROUP = N_HEADS // N_KV_HEADS
ATTN_WIDTH = N_HEADS * HEAD_DIM
KV_WIDTH = N_KV_HEADS * HEAD_DIM
N_FOURIER_GROUPS = 8
FOURIER_GROUP_CH = 128
FOURIER_WIDTH = N_FOURIER_GROUPS * FOURIER_GROUP_CH
QKVF_WIDTH = ATTN_WIDTH + 2 * KV_WIDTH + FOURIER_WIDTH
OFF_K = ATTN_WIDTH
OFF_V = OFF_K + KV_WIDTH
OFF_F = OFF_V + KV_WIDTH
D_FF = 5632
N_META = 16
GRID_W = 64
NORM_EPS = 1e-6
ROPE_THETA = 10000.0
ROPE_PAIRS = HEAD_DIM // 4

DFT_N2 = 80
ROW_TILES_PER_BATCH = 25
PACK_ROWS = 16
LANES = 128
CONV_HALO = PACK_ROWS
MASK_VALUE = -1e30
VMEM_LIMIT = 56 * 1024 * 1024


def _cparams(*sem):
    return pltpu.CompilerParams(dimension_semantics=sem, vmem_limit_bytes=VMEM_LIMIT)


def _rms(x):
    return x * lax.rsqrt(jnp.mean(x * x, axis=-1, keepdims=True) + NORM_EPS)


def _round_up(n, m):
    return -(-n // m) * m


def _meta_rows(x_ref, meta_ref, tiles_per_batch):
    split = x_ref.shape[0] - N_META
    last = pl.program_id(0) % tiles_per_batch == tiles_per_batch - 1
    return jnp.where(last, meta_ref[...], x_ref[split:, :])


def _inproj_kernel(x_ref, meta_ref, g_ref, w_ref, qn_ref, kn_ref, cos_ref, sin_ref,
                   q_ref, k_ref, v_ref, f_ref, h_ref, *, raw_input, tiles_per_batch):
    g = g_ref[...]
    if raw_input:
        split = x_ref.shape[0] - N_META
        h_ref[:split, :] = (_rms(x_ref[:split, :]) * g).astype(BF16)
        h_ref[split:, :] = (_rms(_meta_rows(x_ref, meta_ref, tiles_per_batch)) * g).astype(BF16)
    else:
        h_ref[...] = (_rms(x_ref[...]) * g).astype(BF16)
    proj = jnp.dot(h_ref[...], w_ref[...], preferred_element_type=F32)
    cos = cos_ref[...]
    sin = sin_ref[...]
    lane = lax.broadcasted_iota(jnp.int32, cos.shape, 1)
    first_half = (lane % (HEAD_DIM // 2)) < ROPE_PAIRS

    def norm_rope(t, gain):
        t = _rms(t) * gain
        partner = jnp.where(first_half,
                            pltpu.roll(t, HEAD_DIM - ROPE_PAIRS, 1),
                            pltpu.roll(t, ROPE_PAIRS, 1))
        return t * cos + partner * sin

    scale = math.log2(math.e) / math.sqrt(HEAD_DIM)
    for hh in range(N_HEADS):
        c = hh * HEAD_DIM
        q_ref[:, c:c + HEAD_DIM] = (norm_rope(proj[:, c:c + HEAD_DIM], qn_ref[...]) * scale).astype(BF16)
    for hh in range(N_KV_HEADS):
        c = hh * HEAD_DIM
        k_ref[:, c:c + HEAD_DIM] = norm_rope(proj[:, OFF_K + c:OFF_K + c + HEAD_DIM], kn_ref[...]).astype(BF16)
    v_ref[...] = proj[:, OFF_V:OFF_F].astype(BF16)
    f_ref[...] = proj[:, OFF_F:].astype(BF16)


def _inproj(xs, meta, gain, w_in, qn, kn, cos_t, sin_t, *, layer, tm, tiles_per_batch, raw_input):
    R = xs.shape[0] * tiles_per_batch * tm
    row = lambda i: (i, 0)
    fixed = lambda i: (0, 0)
    tab = lambda i: (i % tiles_per_batch, 0)
    widths = (ATTN_WIDTH, KV_WIDTH, KV_WIDTH, FOURIER_WIDTH, D_MODEL)
    return pl.pallas_call(
        functools.partial(_inproj_kernel, raw_input=raw_input, tiles_per_batch=tiles_per_batch),
        grid=(R // tm,),
        in_specs=[
            pl.BlockSpec((pl.Squeezed(), tm, D_MODEL), lambda i: (i // tiles_per_batch, i % tiles_per_batch, 0)),
            pl.BlockSpec((N_META, D_MODEL), fixed),
            pl.BlockSpec((1, D_MODEL), fixed),
            pl.BlockSpec((pl.Squeezed(), D_MODEL, QKVF_WIDTH), lambda i: (layer, 0, 0)),
            pl.BlockSpec((1, HEAD_DIM), fixed),
            pl.BlockSpec((1, HEAD_DIM), fixed),
            pl.BlockSpec((tm, HEAD_DIM), tab),
            pl.BlockSpec((tm, HEAD_DIM), tab),
        ],
        out_specs=[pl.BlockSpec((tm, w), row) for w in widths],
        out_shape=[jax.ShapeDtypeStruct((R, w), BF16) for w in widths],
        compiler_params=_cparams("parallel"),
        name="inproj",
    )(xs, meta, gain, w_in, qn, kn, cos_t, sin_t)


def _attn_kernel(q_ref, k_ref, v_ref, o_ref, vt_ref, qt_ref, s_ref, mt_ref, m_ref, l_ref, acc_ref,
                 *, L, tk, n_full, tail, tail_pad):
    has_tail = tail > 0
    n_stage = n_full + (1 if has_tail else 0)
    off = 1 if has_tail else 0
    tail_start = n_full * tk

    def padded_tail(ref):
        rows = ref[0, tail_start:L, :]
        if tail_pad == tail:
            return rows
        return jnp.concatenate([rows, jnp.zeros((tail_pad - tail, HEAD_DIM), rows.dtype)], axis=0)

    @pl.when(pl.program_id(2) == 0)
    def _():
        for j in range(n_full):
            vt_ref[:, j * tk:(j + 1) * tk] = v_ref[0, j * tk:(j + 1) * tk, :].astype(F32).T.astype(BF16)
        if has_tail:
            vt_ref[:, tail_start:tail_start + tail_pad] = padded_tail(v_ref).astype(F32).T.astype(BF16)

    m_ref[...] = jnp.full(m_ref.shape, MASK_VALUE, F32)
    l_ref[...] = jnp.zeros(l_ref.shape, F32)
    acc_ref[...] = jnp.zeros(acc_ref.shape, F32)
    for g in range(GQA_GROUP):
        qt_ref[g] = q_ref[0, :, g * HEAD_DIM:(g + 1) * HEAD_DIM].astype(F32).T.astype(BF16)

    def is_tail(t):
        return has_tail and isinstance(t, int) and t == 0

    def tile_start(t):
        if isinstance(t, int):
            return tail_start if is_tail(t) else (t - off) * tk
        return pl.multiple_of((t - off) * tk, tk)

    def scores(t, slot):
        rows = tail_pad if is_tail(t) else tk
        kj = padded_tail(k_ref) if is_tail(t) else k_ref[0, pl.ds(tile_start(t), tk), :]
        for g in range(GQA_GROUP):
            st = jnp.dot(kj, qt_ref[g], preferred_element_type=F32)
            if is_tail(t) and tail_pad != tail:
                key = lax.broadcasted_iota(jnp.int32, st.shape, 0)
                st = jnp.where(key < tail, st, MASK_VALUE)
            s_ref[slot, g, 0:rows, :] = st
            mt_ref[slot, g] = jnp.max(st, axis=0, keepdims=True)

    def softmax_pv(t, slot):
        rows = tail_pad if is_tail(t) else tk
        vtj = vt_ref[:, pl.ds(tile_start(t), rows)]
        for g in range(GQA_GROUP):
            st = s_ref[slot, g, 0:rows, :]
            m_prev = m_ref[g]
            m_new = jnp.maximum(m_prev, mt_ref[slot, g])
            alpha = jnp.exp2(m_prev - m_new)
            p = jnp.exp2(st - m_new)
            l_ref[g] = alpha * l_ref[g] + jnp.sum(p, axis=0, keepdims=True)
            acc_ref[g] = alpha * acc_ref[g] + jnp.dot(vtj, p.astype(BF16), preferred_element_type=F32)
            m_ref[g] = m_new

    def stage(t, slot):
        if not isinstance(t, int) or t + 1 < n_stage:
            scores(t + 1, 1 - slot)
        softmax_pv(t, slot)

    scores(0, 0)
    stage(0, 0)
    n_pairs = max(0, (n_stage - 2) // 2)
    if n_pairs > 0:
        def body(p, carry):
            stage(1 + 2 * p, 1)
            stage(2 + 2 * p, 0)
            return carry
        lax.fori_loop(0, n_pairs, body, 0)
    for t in range(1 + 2 * n_pairs, n_stage):
        stage(t, t % 2)
    for g in range(GQA_GROUP):
        c = g * HEAD_DIM
        o_ref[0, :, c:c + HEAD_DIM] = (acc_ref[g] / l_ref[g]).T.astype(BF16)


def _attention(q, k, v, *, tq, tk):
    B, L, _ = q.shape
    gw = GQA_GROUP * HEAD_DIM
    n_full = L // tk
    tail = L - n_full * tk
    tail_pad = _round_up(tail, LANES)
    kern = functools.partial(_attn_kernel, L=L, tk=tk, n_full=n_full, tail=tail, tail_pad=tail_pad)
    kv_spec = pl.BlockSpec((1, L, HEAD_DIM), lambda b, h, i: (b, 0, h))
    return pl.pallas_call(
        kern,
        grid=(B, N_KV_HEADS, pl.cdiv(L, tq)),
        in_specs=[pl.BlockSpec((1, tq, gw), lambda b, h, i: (b, i, h)), kv_spec, kv_spec],
        out_specs=pl.BlockSpec((1, tq, gw), lambda b, h, i: (b, i, h)),
        out_shape=jax.ShapeDtypeStruct((B, L, ATTN_WIDTH), BF16),
        scratch_shapes=[
            pltpu.VMEM((HEAD_DIM, n_full * tk + tail_pad), BF16),
            pltpu.VMEM((GQA_GROUP, HEAD_DIM, tq), BF16),
            pltpu.VMEM((2, GQA_GROUP, max(tk, tail_pad), tq), F32),
            pltpu.VMEM((2, GQA_GROUP, 1, tq), F32),
            pltpu.VMEM((GQA_GROUP, 1, tq), F32),
            pltpu.VMEM((GQA_GROUP, 1, tq), F32),
            pltpu.VMEM((GQA_GROUP, HEAD_DIM, tq), F32),
        ],
        compiler_params=_cparams("parallel", "parallel", "arbitrary"),
        name="attention",
    )(q, k, v)


def _fourier_kernel(f_ref, cs_ref, f1r_ref, f1i_ref, g_ref, o_ref, wr_ref, wi_ref, os_ref,
                    *, L, n1, n1p, chunk):
    n_chunks = L // chunk
    last = L - chunk
    split = chunk - N_META

    def channel_dft_rows(r0, rows):
        return jnp.dot(f_ref[0, pl.ds(r0, rows), :], cs_ref[...], preferred_element_type=F32)

    def channel_dft(c, carry):
        r0 = pl.multiple_of(c * chunk, PACK_ROWS)
        z = channel_dft_rows(r0, chunk)
        wr_ref[pl.ds(r0 + N_META, chunk), :] = z[:, :FOURIER_GROUP_CH]
        wi_ref[pl.ds(r0 + N_META, chunk), :] = z[:, FOURIER_GROUP_CH:]
        return carry

    lax.fori_loop(0, n_chunks - 1, channel_dft, 0, unroll=4 if (n_chunks - 1) % 4 == 0 else 1)
    z = channel_dft_rows(last, chunk)
    wr_ref[last + N_META:L, :] = z[:split, :FOURIER_GROUP_CH]
    wi_ref[last + N_META:L, :] = z[:split, FOURIER_GROUP_CH:]
    wr_ref[0:N_META, :] = z[split:, :FOURIER_GROUP_CH]
    wi_ref[0:N_META, :] = z[split:, FOURIER_GROUP_CH:]
    if n1p > n1:
        zeros = jnp.zeros(((n1p - n1) * DFT_N2, FOURIER_GROUP_CH), F32)
        wr_ref[L:, :] = zeros
        wi_ref[L:, :] = zeros

    def row_dft(n2, carry):
        rows = pl.ds(n2, n1p, stride=DFT_N2)
        zr = wr_ref[rows, :].astype(BF16)
        zi = wi_ref[rows, :].astype(BF16)
        a = (jnp.dot(f1r_ref[...], zr, preferred_element_type=F32)
             + jnp.dot(f1i_ref[...], zi, preferred_element_type=F32))
        wr_ref[rows, :] = a[:n1p]
        wi_ref[rows, :] = a[n1p:]
        return carry

    lax.fori_loop(0, DFT_N2, row_dft, 0, unroll=16)

    def col_dft(k1, carry):
        r0 = pl.multiple_of(k1 * DFT_N2, DFT_N2)
        a = jnp.concatenate([wr_ref[pl.ds(r0, DFT_N2), :].astype(BF16),
                             wi_ref[pl.ds(r0, DFT_N2), :].astype(BF16)], axis=0)
        os_ref[pl.ds(k1, DFT_N2, stride=n1), :] = jnp.dot(g_ref[k1], a, preferred_element_type=F32)
        return carry

    lax.fori_loop(0, n1, col_dft, 0, unroll=(41 if n1 % 41 == 0 else 5) if n1 % 5 == 0 else 1)

    def emit(c, carry):
        r0 = pl.multiple_of(c * chunk, PACK_ROWS)
        o_ref[0, pl.ds(r0, chunk), :] = os_ref[pl.ds(r0 + N_META, chunk), :].astype(BF16)
        return carry

    lax.fori_loop(0, n_chunks - 1, emit, 0)
    o_ref[0, last:L - N_META, :] = os_ref[last + N_META:L, :].astype(BF16)
    o_ref[0, L - N_META:L, :] = os_ref[0:N_META, :].astype(BF16)


def _fourier(f, consts, *, chunk):
    B, L, _ = f.shape
    n1 = L // DFT_N2
    n1p = _round_up(n1, 8)
    cs, f1r, f1i, g = consts
    slab = pl.BlockSpec((1, L, FOURIER_GROUP_CH), lambda b, c: (b, 0, c))
    fixed2 = lambda b, c: (0, 0)
    return pl.pallas_call(
        functools.partial(_fourier_kernel, L=L, n1=n1, n1p=n1p, chunk=chunk),
        grid=(B, N_FOURIER_GROUPS),
        in_specs=[
            slab,
            pl.BlockSpec((FOURIER_GROUP_CH, 2 * FOURIER_GROUP_CH), fixed2),
            pl.BlockSpec((2 * n1p, n1p), fixed2),
            pl.BlockSpec((2 * n1p, n1p), fixed2),
            pl.BlockSpec((n1, DFT_N2, 2 * DFT_N2), lambda b, c: (0, 0, 0), pipeline_mode=pl.Buffered(1)),
        ],
        out_specs=slab,
        out_shape=jax.ShapeDtypeStruct((B, L, FOURIER_WIDTH), BF16),
        scratch_shapes=[pltpu.VMEM((n1p * DFT_N2, FOURIER_GROUP_CH), F32)] * 2
        + [pltpu.VMEM((L, FOURIER_GROUP_CH), F32)],
        compiler_params=_cparams("parallel", "parallel"),
        name="fourier",
    )(f, cs, f1r, f1i, g)


def _dft_constants(L):
    n1 = L // DFT_N2
    n1p = _round_up(n1, 8)
    k1 = np.arange(n1)
    ang1 = 2.0 * np.pi * np.outer(k1, k1) / n1
    c1, s1 = np.cos(ang1), np.sin(ang1)
    f1r = np.zeros((2 * n1p, n1p))
    f1i = np.zeros((2 * n1p, n1p))
    f1r[:n1, :n1] = c1
    f1r[n1p:n1p + n1, :n1] = -s1
    f1i[:n1, :n1] = s1
    f1i[n1p:n1p + n1, :n1] = c1
    n2 = np.arange(DFT_N2)
    k = k1[:, None, None] + n1 * n2[None, :, None]
    ang = 2.0 * np.pi * ((k * n2[None, None, :]) % L) / L
    g = np.concatenate([np.cos(ang), np.sin(ang)], axis=2) / math.sqrt(L)
    c = np.arange(FOURIER_GROUP_CH)
    angc = 2.0 * np.pi * np.outer(c, c) / FOURIER_GROUP_CH
    cs = np.concatenate([np.cos(angc), -np.sin(angc)], axis=1) / math.sqrt(FOURIER_GROUP_CH)
    return (jnp.asarray(cs, BF16), jnp.asarray(f1r, BF16), jnp.asarray(f1i, BF16), jnp.asarray(g, BF16))


def _mix_kernel(x_ref, meta_ref, h_ref, attn_ref, four_ref, wga_ref, wgf_ref, bga_ref, bgf_ref,
                wa_ref, wf_ref, wo_ref, o_ref, *, raw_input, tiles_per_batch):
    @pl.when(pl.program_id(1) == 0)
    def _():
        if raw_input:
            split = x_ref.shape[0] - N_META
            o_ref[:split, :] = x_ref[:split, :]
            o_ref[split:, :] = _meta_rows(x_ref, meta_ref, tiles_per_batch)
        else:
            o_ref[...] = x_ref[...]

    h = h_ref[...]
    g_attn = jax.nn.sigmoid(jnp.dot(h, wga_ref[...], preferred_element_type=F32) + bga_ref[...])
    g_four = jax.nn.sigmoid(jnp.dot(h, wgf_ref[...], preferred_element_type=F32) + bgf_ref[...])
    a_br = jnp.dot(attn_ref[...], wa_ref[...], preferred_element_type=F32)
    s_br = jnp.dot(four_ref[...], wf_ref[...], preferred_element_type=F32)
    merged = (g_attn * a_br + g_four * s_br).astype(BF16)
    o_ref[...] += jnp.dot(merged, wo_ref[...], preferred_element_type=F32)


def _mix(xs, meta, h, attn, four, w_in, b_gates, w_a, w_f, w_o, *, layer, tm, tn, tiles_per_batch, raw_input):
    R = h.shape[0]
    nj = D_MODEL // tn
    g0 = QKVF_WIDTH // tn
    assert QKVF_WIDTH % tn == 0
    row = lambda i, j: (i, 0)
    return pl.pallas_call(
        functools.partial(_mix_kernel, raw_input=raw_input, tiles_per_batch=tiles_per_batch),
        grid=(R // tm, nj),
        in_specs=[
            pl.BlockSpec((pl.Squeezed(), tm, D_MODEL),
                         lambda i, j: (i // tiles_per_batch, i % tiles_per_batch, 0)),
            pl.BlockSpec((N_META, D_MODEL), lambda i, j: (0, 0)),
            pl.BlockSpec((tm, D_MODEL), row),
            pl.BlockSpec((tm, ATTN_WIDTH), row),
            pl.BlockSpec((tm, FOURIER_WIDTH), row),
            pl.BlockSpec((pl.Squeezed(), D_MODEL, tn), lambda i, j: (layer, 0, g0 + j)),
            pl.BlockSpec((pl.Squeezed(), D_MODEL, tn), lambda i, j: (layer, 0, g0 + nj + j)),
            pl.BlockSpec((1, tn), lambda i, j: (0, j)),
            pl.BlockSpec((1, tn), lambda i, j: (0, j + nj)),
            pl.BlockSpec((ATTN_WIDTH, tn), lambda i, j: (0, j)),
            pl.BlockSpec((FOURIER_WIDTH, tn), lambda i, j: (0, j)),
            pl.BlockSpec((tn, D_MODEL), lambda i, j: (j, 0)),
        ],
        out_specs=pl.BlockSpec((tm, D_MODEL), row),
        out_shape=jax.ShapeDtypeStruct((R, D_MODEL), F32),
        compiler_params=_cparams("parallel", "arbitrary"),
        name="mix_out",
    )(xs, meta, h, attn, four, w_in, w_in, b_gates, b_gates, w_a, w_f, w_o)


def _ffn_kernel(x_ref, xp_ref, xn_ref, g_ref, wg_ref, wv_ref, wc_ref, bc_ref, wd_ref,
                o_ref, h_ref, *, tm, tiles_per_batch):
    j = pl.program_id(1)
    H = CONV_HALO

    @pl.when(j == 0)
    def _():
        x = x_ref[...]
        g = g_ref[...]
        h_ref[H:H + tm, :] = (_rms(x) * g).astype(BF16)
        h_ref[0:H, :] = (_rms(xp_ref[...]) * g).astype(BF16)
        h_ref[H + tm:, :] = (_rms(xn_ref[...]) * g).astype(BF16)
        o_ref[...] = x

    h = h_ref[...]
    up_g = jnp.dot(h, wg_ref[...], preferred_element_type=F32)
    n = tm + 2 * H
    wc = wc_ref[...]
    prev = pltpu.roll(up_g, 1, 0)[H:H + tm]
    nxt = pltpu.roll(up_g, n - 1, 0)[H:H + tm]
    row = lax.broadcasted_iota(jnp.int32, (tm, 1), 0)
    seam = jnp.where(pl.program_id(0) % tiles_per_batch == tiles_per_batch - 1, tm - N_META, -1)
    prev = jnp.where(row == seam, 0.0, prev)
    nxt = jnp.where(row == seam - 1, 0.0, nxt)
    u = prev * wc[0:1] + up_g[H:H + tm] * wc[1:2] + nxt * wc[2:3] + bc_ref[...]
    val = jnp.dot(h[H:H + tm], wv_ref[...], preferred_element_type=F32)
    act = (u * jax.nn.sigmoid(u) * val).astype(BF16)
    o_ref[...] += jnp.dot(act, wd_ref[...], preferred_element_type=F32)


def _ffn(hs, gain, w_up, w_conv, b_conv, w_down, *, tm, tf, tiles_per_batch, n_tok_out=None):
    R = hs.shape[0]
    nj = D_FF // tf
    hb = tm // CONV_HALO
    nb = tiles_per_batch * hb
    row = lambda i, j: (i, 0)

    def prev_block(i, j):
        first = i % tiles_per_batch == 0
        return (jnp.where(first, (i // tiles_per_batch) * nb + nb - 1, i * hb - 1), 0)

    def next_block(i, j):
        last = i % tiles_per_batch == tiles_per_batch - 1
        return (jnp.where(last, (i // tiles_per_batch) * nb, (i + 1) * hb), 0)

    if n_tok_out is None:
        out_spec = pl.BlockSpec((tm, D_MODEL), row)
        out_shape = jax.ShapeDtypeStruct((R, D_MODEL), F32)
    else:
        out_spec = pl.BlockSpec((pl.Squeezed(), tm, D_MODEL),
                                lambda i, j: (i // tiles_per_batch, i % tiles_per_batch, 0))
        out_shape = jax.ShapeDtypeStruct((R // (tiles_per_batch * tm), n_tok_out, D_MODEL), F32)
    return pl.pallas_call(
        functools.partial(_ffn_kernel, tm=tm, tiles_per_batch=tiles_per_batch),
        grid=(R // tm, nj),
        in_specs=[
            pl.BlockSpec((tm, D_MODEL), row),
            pl.BlockSpec((CONV_HALO, D_MODEL), prev_block),
            pl.BlockSpec((CONV_HALO, D_MODEL), next_block),
            pl.BlockSpec((1, D_MODEL), lambda i, j: (0, 0)),
            pl.BlockSpec((D_MODEL, tf), lambda i, j: (0, j)),
            pl.BlockSpec((D_MODEL, tf), lambda i, j: (0, j + nj)),
            pl.BlockSpec((3, tf), lambda i, j: (0, j)),
            pl.BlockSpec((1, tf), lambda i, j: (0, j)),
            pl.BlockSpec((tf, D_MODEL), lambda i, j: (j, 0)),
        ],
        out_specs=out_spec,
        out_shape=out_shape,
        scratch_shapes=[pltpu.VMEM((tm + 2 * CONV_HALO, D_MODEL), BF16)],
        compiler_params=_cparams("parallel", "arbitrary"),
        name="ffn",
    )(hs, hs, hs, gain, w_up, w_up, w_conv, b_conv, w_down)


def _rope_tables(n_tok):
    rows = n_tok // GRID_W
    pos_r = np.concatenate([np.repeat(np.arange(rows), GRID_W), np.zeros(N_META)])
    pos_c = np.concatenate([np.tile(np.arange(GRID_W), rows), np.zeros(N_META)])
    inv_freq = 1.0 / (ROPE_THETA ** (np.arange(ROPE_PAIRS) / ROPE_PAIRS))
    ang_r = pos_r[:, None] * inv_freq[None, :]
    ang_c = pos_c[:, None] * inv_freq[None, :]
    cos_t = np.concatenate([np.cos(ang_r), np.cos(ang_r), np.cos(ang_c), np.cos(ang_c)], axis=-1)
    sin_t = np.concatenate([-np.sin(ang_r), np.sin(ang_r), -np.sin(ang_c), np.sin(ang_c)], axis=-1)
    return jnp.asarray(cos_t, F32), jnp.asarray(sin_t, F32)


def _tiles(L):
    tm = L // ROW_TILES_PER_BATCH
    return dict(tm=tm, tq=256, tk=2560, tn_mix=512, tf=512)


def kernel(x, meta_tokens, norm_mix, norm_ffn, w_in, b_gate, q_norm, k_norm, w_attn_br, w_four, w_out,
           w_up, w_conv, b_conv, w_down):
    B, n_tok, D = x.shape
    depth = w_in.shape[0]
    L = n_tok + N_META
    t = _tiles(L)
    tm = t["tm"]
    assert D == D_MODEL and n_tok % GRID_W == 0 and L % DFT_N2 == 0
    assert L % tm == 0 and tm % PACK_ROWS == 0
    R = B * L
    tiles_per_batch = L // tm

    cos_t, sin_t = _rope_tables(n_tok)
    dft = _dft_constants(L)

    meta = meta_tokens.astype(x.dtype)
    w_in16 = w_in.astype(BF16)
    xs = x
    for i in range(depth):
        raw = i == 0
        q, k, v, f, h = _inproj(xs, meta, norm_mix[i][None], w_in16, q_norm[i][None], k_norm[i][None],
                                cos_t, sin_t, layer=i, tm=tm, tiles_per_batch=tiles_per_batch, raw_input=raw)
        attn = _attention(q.reshape(B, L, ATTN_WIDTH), k.reshape(B, L, KV_WIDTH), v.reshape(B, L, KV_WIDTH),
                          tq=t["tq"], tk=t["tk"]).reshape(R, ATTN_WIDTH)
        four = _fourier(f.reshape(B, L, FOURIER_WIDTH), dft, chunk=tm).reshape(R, FOURIER_WIDTH)
        hs = _mix(xs, meta, h, attn, four, w_in16, b_gate[i][None],
                  w_attn_br[i].astype(BF16), w_four[i].astype(BF16), w_out[i].astype(BF16),
                  layer=i, tm=tm, tn=t["tn_mix"], tiles_per_batch=tiles_per_batch, raw_input=raw)
        hs = _ffn(hs, norm_ffn[i][None], w_up[i].astype(BF16), w_conv[i], b_conv[i][None],
                  w_down[i].astype(BF16), tm=tm, tf=t["tf"], tiles_per_batch=tiles_per_batch,
                  n_tok_out=n_tok if i == depth - 1 else None)
        xs = hs.reshape(B, L, D) if i < depth - 1 else hs
    return xs
```

```python
import functools
import math

import jax
import jax.numpy as jnp
import numpy as np
from jax import lax
from jax.experimental import pallas as pl
from jax.experimental.pallas import tpu as pltpu

F32 = jnp.float32
BF16 = jnp.bfloat16

D_MODEL = 2048
N_HEADS = 8
N_KV_HEADS = 2
HEAD_DIM = 128
GQA_GROUP = N_HEADS // N_KV_HEADS
ATTN_WIDTH = N_HEADS * HEAD_DIM
KV_WIDTH = N_KV_HEADS * HEAD_DIM
N_FOURIER_GROUPS = 8
FOURIER_GROUP_CH = 128
FOURIER_WIDTH = N_FOURIER_GROUPS * FOURIER_GROUP_CH
QKVF_WIDTH = ATTN_WIDTH + 2 * KV_WIDTH + FOURIER_WIDTH
OFF_K = ATTN_WIDTH
OFF_V = OFF_K + KV_WIDTH
OFF_F = OFF_V + KV_WIDTH
D_FF = 5632
N_META = 16
GRID_W = 64
NORM_EPS = 1e-6
ROPE_THETA = 10000.0
ROPE_PAIRS = HEAD_DIM // 4

DFT_N2 = 80
ROW_TILES_PER_BATCH = 25
PACK_ROWS = 16
LANES = 128
CONV_HALO = PACK_ROWS
MASK_VALUE = -1e30
VMEM_LIMIT = 56 * 1024 * 1024


def _cparams(*sem):
    return pltpu.CompilerParams(dimension_semantics=sem, vmem_limit_bytes=VMEM_LIMIT)


def _rms(x):
    return x * lax.rsqrt(jnp.mean(x * x, axis=-1, keepdims=True) + NORM_EPS)


def _round_up(n, m):
    return -(-n // m) * m


def _meta_rows(x_ref, meta_ref, tiles_per_batch):
    split = x_ref.shape[0] - N_META
    last = pl.program_id(0) % tiles_per_batch == tiles_per_batch - 1
    return jnp.where(last, meta_ref[...], x_ref[split:, :])


def _inproj_kernel(x_ref, meta_ref, g_ref, w_ref, qn_ref, kn_ref, cos_ref, sin_ref,
                   q_ref, k_ref, v_ref, f_ref, h_ref, *, raw_input, tiles_per_batch):
    g = g_ref[...]
    if raw_input:
        split = x_ref.shape[0] - N_META
        h_ref[:split, :] = (_rms(x_ref[:split, :]) * g).astype(BF16)
        h_ref[split:, :] = (_rms(_meta_rows(x_ref, meta_ref, tiles_per_batch)) * g).astype(BF16)
    else:
        h_ref[...] = (_rms(x_ref[...]) * g).astype(BF16)
    proj = jnp.dot(h_ref[...], w_ref[...], preferred_element_type=F32)
    cos = cos_ref[...]
    sin = sin_ref[...]
    lane = lax.broadcasted_iota(jnp.int32, cos.shape, 1)
    first_half = (lane % (HEAD_DIM // 2)) < ROPE_PAIRS

    def norm_rope(t, gain):
        t = _rms(t) * gain
        partner = jnp.where(first_half,
                            pltpu.roll(t, HEAD_DIM - ROPE_PAIRS, 1),
                            pltpu.roll(t, ROPE_PAIRS, 1))
        return t * cos + partner * sin

    scale = math.log2(math.e) / math.sqrt(HEAD_DIM)
    for hh in range(N_HEADS):
        c = hh * HEAD_DIM
        q_ref[:, c:c + HEAD_DIM] = (norm_rope(proj[:, c:c + HEAD_DIM], qn_ref[...]) * scale).astype(BF16)
    for hh in range(N_KV_HEADS):
        c = hh * HEAD_DIM
        k_ref[:, c:c + HEAD_DIM] = norm_rope(proj[:, OFF_K + c:OFF_K + c + HEAD_DIM], kn_ref[...]).astype(BF16)
    v_ref[...] = proj[:, OFF_V:OFF_F].astype(BF16)
    f_ref[...] = proj[:, OFF_F:].astype(BF16)


def _inproj(xs, meta, gain, w_in, qn, kn, cos_t, sin_t, *, layer, tm, tiles_per_batch, raw_input):
    R = xs.shape[0] * tiles_per_batch * tm
    row = lambda i: (i, 0)
    fixed = lambda i: (0, 0)
    tab = lambda i: (i % tiles_per_batch, 0)
    widths = (ATTN_WIDTH, KV_WIDTH, KV_WIDTH, FOURIER_WIDTH, D_MODEL)
    return pl.pallas_call(
        functools.partial(_inproj_kernel, raw_input=raw_input, tiles_per_batch=tiles_per_batch),
        grid=(R // tm,),
        in_specs=[
            pl.BlockSpec((pl.Squeezed(), tm, D_MODEL), lambda i: (i // tiles_per_batch, i % tiles_per_batch, 0)),
            pl.BlockSpec((N_META, D_MODEL), fixed),
            pl.BlockSpec((1, D_MODEL), fixed),
            pl.BlockSpec((pl.Squeezed(), D_MODEL, QKVF_WIDTH), lambda i: (layer, 0, 0)),
            pl.BlockSpec((1, HEAD_DIM), fixed),
            pl.BlockSpec((1, HEAD_DIM), fixed),
            pl.BlockSpec((tm, HEAD_DIM), tab),
            pl.BlockSpec((tm, HEAD_DIM), tab),
        ],
        out_specs=[pl.BlockSpec((tm, w), row) for w in widths],
        out_shape=[jax.ShapeDtypeStruct((R, w), BF16) for w in widths],
        compiler_params=_cparams("parallel"),
        name="inproj",
    )(xs, meta, gain, w_in, qn, kn, cos_t, sin_t)


def _attn_kernel(q_ref, k_ref, v_ref, o_ref, vt_ref, qt_ref, s_ref, mt_ref, m_ref, l_ref, acc_ref,
                 *, L, tk, n_full, tail, tail_pad):
    has_tail = tail > 0
    n_stage = n_full + (1 if has_tail else 0)
    off = 1 if has_tail else 0
    tail_start = n_full * tk

    def padded_tail(ref):
        rows = ref[0, tail_start:L, :]
        if tail_pad == tail:
            return rows
        return jnp.concatenate([rows, jnp.zeros((tail_pad - tail, HEAD_DIM), rows.dtype)], axis=0)

    @pl.when(pl.program_id(2) == 0)
    def _():
        for j in range(n_full):
            vt_ref[:, j * tk:(j + 1) * tk] = v_ref[0, j * tk:(j + 1) * tk, :].astype(F32).T.astype(BF16)
        if has_tail:
            vt_ref[:, tail_start:tail_start + tail_pad] = padded_tail(v_ref).astype(F32).T.astype(BF16)

    m_ref[...] = jnp.full(m_ref.shape, MASK_VALUE, F32)
    l_ref[...] = jnp.zeros(l_ref.shape, F32)
    acc_ref[...] = jnp.zeros(acc_ref.shape, F32)
    for g in range(GQA_GROUP):
        qt_ref[g] = q_ref[0, :, g * HEAD_DIM:(g + 1) * HEAD_DIM].astype(F32).T.astype(BF16)

    def is_tail(t):
        return has_tail and isinstance(t, int) and t == 0

    def tile_start(t):
        if isinstance(t, int):
            return tail_start if is_tail(t) else (t - off) * tk
        return pl.multiple_of((t - off) * tk, tk)

    def scores(t, slot):
        rows = tail_pad if is_tail(t) else tk
        kj = padded_tail(k_ref) if is_tail(t) else k_ref[0, pl.ds(tile_start(t), tk), :]
        for g in range(GQA_GROUP):
            st = jnp.dot(kj, qt_ref[g], preferred_element_type=F32)
            if is_tail(t) and tail_pad != tail:
                key = lax.broadcasted_iota(jnp.int32, st.shape, 0)
                st = jnp.where(key < tail, st, MASK_VALUE)
            s_ref[slot, g, 0:rows, :] = st
            mt_ref[slot, g] = jnp.max(st, axis=0, keepdims=True)

    def softmax_pv(t, slot):
        rows = tail_pad if is_tail(t) else tk
        vtj = vt_ref[:, pl.ds(tile_start(t), rows)]
        for g in range(GQA_GROUP):
            st = s_ref[slot, g, 0:rows, :]
            m_prev = m_ref[g]
            m_new = jnp.maximum(m_prev, mt_ref[slot, g])
            alpha = jnp.exp2(m_prev - m_new)
            p = jnp.exp2(st - m_new)
            l_ref[g] = alpha * l_ref[g] + jnp.sum(p, axis=0, keepdims=True)
            acc_ref[g] = alpha * acc_ref[g] + jnp.dot(vtj, p.astype(BF16), preferred_element_type=F32)
            m_ref[g] = m_new

    def stage(t, slot):
        if not isinstance(t, int) or t + 1 < n_stage:
            scores(t + 1, 1 - slot)
        softmax_pv(t, slot)

    scores(0, 0)
    stage(0, 0)
    n_pairs = max(0, (n_stage - 2) // 2)
    if n_pairs > 0:
        def body(p, carry):
            stage(1 + 2 * p, 1)
            stage(2 + 2 * p, 0)
            return carry
        lax.fori_loop(0, n_pairs, body, 0)
    for t in range(1 + 2 * n_pairs, n_stage):
        stage(t, t % 2)
    for g in range(GQA_GROUP):
        c = g * HEAD_DIM
        o_ref[0, :, c:c + HEAD_DIM] = (acc_ref[g] / l_ref[g]).T.astype(BF16)


def _attention(q, k, v, *, tq, tk):
    B, L, _ = q.shape
    gw = GQA_GROUP * HEAD_DIM
    n_full = L // tk
    tail = L - n_full * tk
    tail_pad = _round_up(tail, LANES)
    kern = functools.partial(_attn_kernel, L=L, tk=tk, n_full=n_full, tail=tail, tail_pad=tail_pad)
    kv_spec = pl.BlockSpec((1, L, HEAD_DIM), lambda b, h, i: (b, 0, h))
    return pl.pallas_call(
        kern,
        grid=(B, N_KV_HEADS, pl.cdiv(L, tq)),
        in_specs=[pl.BlockSpec((1, tq, gw), lambda b, h, i: (b, i, h)), kv_spec, kv_spec],
        out_specs=pl.BlockSpec((1, tq, gw), lambda b, h, i: (b, i, h)),
        out_shape=jax.ShapeDtypeStruct((B, L, ATTN_WIDTH), BF16),
        scratch_shapes=[
            pltpu.VMEM((HEAD_DIM, n_full * tk + tail_pad), BF16),
            pltpu.VMEM((GQA_GROUP, HEAD_DIM, tq), BF16),
            pltpu.VMEM((2, GQA_GROUP, max(tk, tail_pad), tq), F32),
            pltpu.VMEM((2, GQA_GROUP, 1, tq), F32),
            pltpu.VMEM((GQA_GROUP, 1, tq), F32),
            pltpu.VMEM((GQA_GROUP, 1, tq), F32),
            pltpu.VMEM((GQA_GROUP, HEAD_DIM, tq), F32),
        ],
        compiler_params=_cparams("parallel", "parallel", "arbitrary"),
        name="attention",
    )(q, k, v)


def _fourier_kernel(f_ref, cs_ref, f1r_ref, f1i_ref, g_ref, o_ref, wr_ref, wi_ref, os_ref,
                    *, L, n1, n1p, chunk):
    n_chunks = L // chunk
    last = L - chunk
    split = chunk - N_META

    def channel_dft_rows(r0, rows):
        return jnp.dot(f_ref[0, pl.ds(r0, rows), :], cs_ref[...], preferred_element_type=F32)

    def channel_dft(c, carry):
        r0 = pl.multiple_of(c * chunk, PACK_ROWS)
        z = channel_dft_rows(r0, chunk)
        wr_ref[pl.ds(r0 + N_META, chunk), :] = z[:, :FOURIER_GROUP_CH]
        wi_ref[pl.ds(r0 + N_META, chunk), :] = z[:, FOURIER_GROUP_CH:]
        return carry

    lax.fori_loop(0, n_chunks - 1, channel_dft, 0, unroll=4 if (n_chunks - 1) % 4 == 0 else 1)
    z = channel_dft_rows(last, chunk)
    wr_ref[last + N_META:L, :] = z[:split, :FOURIER_GROUP_CH]
    wi_ref[last + N_META:L, :] = z[:split, FOURIER_GROUP_CH:]
    wr_ref[0:N_META, :] = z[split:, :FOURIER_GROUP_CH]
    wi_ref[0:N_META, :] = z[split:, FOURIER_GROUP_CH:]
    if n1p > n1:
        zeros = jnp.zeros(((n1p - n1) * DFT_N2, FOURIER_GROUP_CH), F32)
        wr_ref[L:, :] = zeros
        wi_ref[L:, :] = zeros

    def row_dft(n2, carry):
        rows = pl.ds(n2, n1p, stride=DFT_N2)
        zr = wr_ref[rows, :].astype(BF16)
        zi = wi_ref[rows, :].astype(BF16)
        a = (jnp.dot(f1r_ref[...], zr, preferred_element_type=F32)
             + jnp.dot(f1i_ref[...], zi, preferred_element_type=F32))
        wr_ref[rows, :] = a[:n1p]
        wi_ref[rows, :] = a[n1p:]
        return carry

    lax.fori_loop(0, DFT_N2, row_dft, 0, unroll=16)

    def col_dft(k1, carry):
        r0 = pl.multiple_of(k1 * DFT_N2, DFT_N2)
        a = jnp.concatenate([wr_ref[pl.ds(r0, DFT_N2), :].astype(BF16),
                             wi_ref[pl.ds(r0, DFT_N2), :].astype(BF16)], axis=0)
        os_ref[pl.ds(k1, DFT_N2, stride=n1), :] = jnp.dot(g_ref[k1], a, preferred_element_type=F32)
        return carry

    lax.fori_loop(0, n1, col_dft, 0, unroll=(41 if n1 % 41 == 0 else 5) if n1 % 5 == 0 else 1)

    def emit(c, carry):
        r0 = pl.multiple_of(c * chunk, PACK_ROWS)
        o_ref[0, pl.ds(r0, chunk), :] = os_ref[pl.ds(r0 + N_META, chunk), :].astype(BF16)
        return carry

    lax.fori_loop(0, n_chunks - 1, emit, 0)
    o_ref[0, last:L - N_META, :] = os_ref[last + N_META:L, :].astype(BF16)
    o_ref[0, L - N_META:L, :] = os_ref[0:N_META, :].astype(BF16)


def _fourier(f, consts, *, chunk):
    B, L, _ = f.shape
    n1 = L // DFT_N2
    n1p = _round_up(n1, 8)
    cs, f1r, f1i, g = consts
    slab = pl.BlockSpec((1, L, FOURIER_GROUP_CH), lambda b, c: (b, 0, c))
    fixed2 = lambda b, c: (0, 0)
    return pl.pallas_call(
        functools.partial(_fourier_kernel, L=L, n1=n1, n1p=n1p, chunk=chunk),
        grid=(B, N_FOURIER_GROUPS),
        in_specs=[
            slab,
            pl.BlockSpec((FOURIER_GROUP_CH, 2 * FOURIER_GROUP_CH), fixed2),
            pl.BlockSpec((2 * n1p, n1p), fixed2),
            pl.BlockSpec((2 * n1p, n1p), fixed2),
            pl.BlockSpec((n1, DFT_N2, 2 * DFT_N2), lambda b, c: (0, 0, 0), pipeline_mode=pl.Buffered(1)),
        ],
        out_specs=slab,
        out_shape=jax.ShapeDtypeStruct((B, L, FOURIER_WIDTH), BF16),
        scratch_shapes=[pltpu.VMEM((n1p * DFT_N2, FOURIER_GROUP_CH), F32)] * 2
        + [pltpu.VMEM((L, FOURIER_GROUP_CH), F32)],
        compiler_params=_cparams("parallel", "parallel"),
        name="fourier",
    )(f, cs, f1r, f1i, g)


def _dft_constants(L):
    n1 = L // DFT_N2
    n1p = _round_up(n1, 8)
    k1 = np.arange(n1)
    ang1 = 2.0 * np.pi * np.outer(k1, k1) / n1
    c1, s1 = np.cos(ang1), np.sin(ang1)
    f1r = np.zeros((2 * n1p, n1p))
    f1i = np.zeros((2 * n1p, n1p))
    f1r[:n1, :n1] = c1
    f1r[n1p:n1p + n1, :n1] = -s1
    f1i[:n1, :n1] = s1
    f1i[n1p:n1p + n1, :n1] = c1
    n2 = np.arange(DFT_N2)
    k = k1[:, None, None] + n1 * n2[None, :, None]
    ang = 2.0 * np.pi * ((k * n2[None, None, :]) % L) / L
    g = np.concatenate([np.cos(ang), np.sin(ang)], axis=2) / math.sqrt(L)
    c = np.arange(FOURIER_GROUP_CH)
    angc = 2.0 * np.pi * np.outer(c, c) / FOURIER_GROUP_CH
    cs = np.concatenate([np.cos(angc), -np.sin(angc)], axis=1) / math.sqrt(FOURIER_GROUP_CH)
    return (jnp.asarray(cs, BF16), jnp.asarray(f1r, BF16), jnp.asarray(f1i, BF16), jnp.asarray(g, BF16))


def _mix_kernel(x_ref, meta_ref, h_ref, attn_ref, four_ref, wga_ref, wgf_ref, bga_ref, bgf_ref,
                wa_ref, wf_ref, wo_ref, o_ref, *, raw_input, tiles_per_batch):
    @pl.when(pl.program_id(1) == 0)
    def _():
        if raw_input:
            split = x_ref.shape[0] - N_META
            o_ref[:split, :] = x_ref[:split, :]
            o_ref[split:, :] = _meta_rows(x_ref, meta_ref, tiles_per_batch)
        else:
            o_ref[...] = x_ref[...]

    h = h_ref[...]
    g_attn = jax.nn.sigmoid(jnp.dot(h, wga_ref[...], preferred_element_type=F32) + bga_ref[...])
    g_four = jax.nn.sigmoid(jnp.dot(h, wgf_ref[...], preferred_element_type=F32) + bgf_ref[...])
    a_br = jnp.dot(attn_ref[...], wa_ref[...], preferred_element_type=F32)
    s_br = jnp.dot(four_ref[...], wf_ref[...], preferred_element_type=F32)
    merged = (g_attn * a_br + g_four * s_br).astype(BF16)
    o_ref[...] += jnp.dot(merged, wo_ref[...], preferred_element_type=F32)


def _mix(xs, meta, h, attn, four, w_in, b_gates, w_a, w_f, w_o, *, layer, tm, tn, tiles_per_batch, raw_input):
    R = h.shape[0]
    nj = D_MODEL // tn
    g0 = QKVF_WIDTH // tn
    assert QKVF_WIDTH % tn == 0
    row = lambda i, j: (i, 0)
    return pl.pallas_call(
        functools.partial(_mix_kernel, raw_input=raw_input, tiles_per_batch=tiles_per_batch),
        grid=(R // tm, nj),
        in_specs=[
            pl.BlockSpec((pl.Squeezed(), tm, D_MODEL),
                         lambda i, j: (i // tiles_per_batch, i % tiles_per_batch, 0)),
            pl.BlockSpec((N_META, D_MODEL), lambda i, j: (0, 0)),
            pl.BlockSpec((tm, D_MODEL), row),
            pl.BlockSpec((tm, ATTN_WIDTH), row),
            pl.BlockSpec((tm, FOURIER_WIDTH), row),
            pl.BlockSpec((pl.Squeezed(), D_MODEL, tn), lambda i, j: (layer, 0, g0 + j)),
            pl.BlockSpec((pl.Squeezed(), D_MODEL, tn), lambda i, j: (layer, 0, g0 + nj + j)),
            pl.BlockSpec((1, tn), lambda i, j: (0, j)),
            pl.BlockSpec((1, tn), lambda i, j: (0, j + nj)),
            pl.BlockSpec((pl.Squeezed(), ATTN_WIDTH, tn), lambda i, j: (layer, 0, j)),
            pl.BlockSpec((pl.Squeezed(), FOURIER_WIDTH, tn), lambda i, j: (layer, 0, j)),
            pl.BlockSpec((pl.Squeezed(), tn, D_MODEL), lambda i, j: (layer, j, 0)),
        ],
        out_specs=pl.BlockSpec((tm, D_MODEL), row),
        out_shape=jax.ShapeDtypeStruct((R, D_MODEL), F32),
        compiler_params=_cparams("parallel", "arbitrary"),
        name="mix_out",
    )(xs, meta, h, attn, four, w_in, w_in, b_gates, b_gates, w_a, w_f, w_o)


def _ffn_kernel(x_ref, xp_ref, xn_ref, g_ref, wg_ref, wv_ref, wc_ref, bc_ref, wd_ref,
                o_ref, h_ref, *, tm, tiles_per_batch):
    j = pl.program_id(1)
    H = CONV_HALO

    @pl.when(j == 0)
    def _():
        x = x_ref[...]
        g = g_ref[...]
        h_ref[H:H + tm, :] = (_rms(x) * g).astype(BF16)
        h_ref[0:H, :] = (_rms(xp_ref[...]) * g).astype(BF16)
        h_ref[H + tm:, :] = (_rms(xn_ref[...]) * g).astype(BF16)
        o_ref[...] = x

    h = h_ref[...]
    up_g = jnp.dot(h, wg_ref[...], preferred_element_type=F32)
    n = tm + 2 * H
    wc = wc_ref[...]
    prev = pltpu.roll(up_g, 1, 0)[H:H + tm]
    nxt = pltpu.roll(up_g, n - 1, 0)[H:H + tm]
    row = lax.broadcasted_iota(jnp.int32, (tm, 1), 0)
    seam = jnp.where(pl.program_id(0) % tiles_per_batch == tiles_per_batch - 1, tm - N_META, -1)
    prev = jnp.where(row == seam, 0.0, prev)
    nxt = jnp.where(row == seam - 1, 0.0, nxt)
    u = prev * wc[0:1] + up_g[H:H + tm] * wc[1:2] + nxt * wc[2:3] + bc_ref[...]
    val = jnp.dot(h[H:H + tm], wv_ref[...], preferred_element_type=F32)
    act = (u * jax.nn.sigmoid(u) * val).astype(BF16)
    o_ref[...] += jnp.dot(act, wd_ref[...], preferred_element_type=F32)


def _ffn(hs, gain, w_up, w_conv, b_conv, w_down, *, layer, tm, tf, tiles_per_batch, n_tok_out=None):
    R = hs.shape[0]
    nj = D_FF // tf
    hb = tm // CONV_HALO
    nb = tiles_per_batch * hb
    row = lambda i, j: (i, 0)

    def prev_block(i, j):
        first = i % tiles_per_batch == 0
        return (jnp.where(first, (i // tiles_per_batch) * nb + nb - 1, i * hb - 1), 0)

    def next_block(i, j):
        last = i % tiles_per_batch == tiles_per_batch - 1
        return (jnp.where(last, (i // tiles_per_batch) * nb, (i + 1) * hb), 0)

    if n_tok_out is None:
        out_spec = pl.BlockSpec((tm, D_MODEL), row)
        out_shape = jax.ShapeDtypeStruct((R, D_MODEL), F32)
    else:
        out_spec = pl.BlockSpec((pl.Squeezed(), tm, D_MODEL),
                                lambda i, j: (i // tiles_per_batch, i % tiles_per_batch, 0))
        out_shape = jax.ShapeDtypeStruct((R // (tiles_per_batch * tm), n_tok_out, D_MODEL), F32)
    return pl.pallas_call(
        functools.partial(_ffn_kernel, tm=tm, tiles_per_batch=tiles_per_batch),
        grid=(R // tm, nj),
        in_specs=[
            pl.BlockSpec((tm, D_MODEL), row),
            pl.BlockSpec((CONV_HALO, D_MODEL), prev_block),
            pl.BlockSpec((CONV_HALO, D_MODEL), next_block),
            pl.BlockSpec((1, D_MODEL), lambda i, j: (0, 0)),
            pl.BlockSpec((pl.Squeezed(), D_MODEL, tf), lambda i, j: (layer, 0, j)),
            pl.BlockSpec((pl.Squeezed(), D_MODEL, tf), lambda i, j: (layer, 0, j + nj)),
            pl.BlockSpec((3, tf), lambda i, j: (0, j)),
            pl.BlockSpec((1, tf), lambda i, j: (0, j)),
            pl.BlockSpec((pl.Squeezed(), tf, D_MODEL), lambda i, j: (layer, j, 0)),
        ],
        out_specs=out_spec,
        out_shape=out_shape,
        scratch_shapes=[pltpu.VMEM((tm + 2 * CONV_HALO, D_MODEL), BF16)],
        compiler_params=_cparams("parallel", "arbitrary"),
        name="ffn",
    )(hs, hs, hs, gain, w_up, w_up, w_conv, b_conv, w_down)


def _rope_tables(n_tok):
    rows = n_tok // GRID_W
    pos_r = np.concatenate([np.repeat(np.arange(rows), GRID_W), np.zeros(N_META)])
    pos_c = np.concatenate([np.tile(np.arange(GRID_W), rows), np.zeros(N_META)])
    inv_freq = 1.0 / (ROPE_THETA ** (np.arange(ROPE_PAIRS) / ROPE_PAIRS))
    ang_r = pos_r[:, None] * inv_freq[None, :]
    ang_c = pos_c[:, None] * inv_freq[None, :]
    cos_t = np.concatenate([np.cos(ang_r), np.cos(ang_r), np.cos(ang_c), np.cos(ang_c)], axis=-1)
    sin_t = np.concatenate([-np.sin(ang_r), np.sin(ang_r), -np.sin(ang_c), np.sin(ang_c)], axis=-1)
    return jnp.asarray(cos_t, F32), jnp.asarray(sin_t, F32)


def _tiles(L):
    tm = L // ROW_TILES_PER_BATCH
    return dict(tm=tm, tq=256, tk=2560, tn_mix=512, tf=512)


def kernel(x, meta_tokens, norm_mix, norm_ffn, w_in, b_gate, q_norm, k_norm, w_attn_br, w_four, w_out,
           w_up, w_conv, b_conv, w_down):
    B, n_tok, D = x.shape
    depth = w_in.shape[0]
    L = n_tok + N_META
    t = _tiles(L)
    tm = t["tm"]
    assert D == D_MODEL and n_tok % GRID_W == 0 and L % DFT_N2 == 0
    assert L % tm == 0 and tm % PACK_ROWS == 0
    R = B * L
    tiles_per_batch = L // tm

    cos_t, sin_t = _rope_tables(n_tok)
    dft = _dft_constants(L)

    meta = meta_tokens.astype(x.dtype)
    w_in16, w_a16, w_f16, w_o16, w_up16, w_dn16 = (
        w.astype(BF16) for w in (w_in, w_attn_br, w_four, w_out, w_up, w_down))
    xs = x
    for i in range(depth):
        raw = i == 0
        q, k, v, f, h = _inproj(xs, meta, norm_mix[i][None], w_in16, q_norm[i][None], k_norm[i][None],
                                cos_t, sin_t, layer=i, tm=tm, tiles_per_batch=tiles_per_batch, raw_input=raw)
        attn = _attention(q.reshape(B, L, ATTN_WIDTH), k.reshape(B, L, KV_WIDTH), v.reshape(B, L, KV_WIDTH),
                          tq=t["tq"], tk=t["tk"]).reshape(R, ATTN_WIDTH)
        four = _fourier(f.reshape(B, L, FOURIER_WIDTH), dft, chunk=tm).reshape(R, FOURIER_WIDTH)
        hs = _mix(xs, meta, h, attn, four, w_in16, b_gate[i][None], w_a16, w_f16, w_o16,
                  layer=i, tm=tm, tn=t["tn_mix"], tiles_per_batch=tiles_per_batch, raw_input=raw)
        hs = _ffn(hs, norm_ffn[i][None], w_up16, w_conv[i], b_conv[i][None], w_dn16,
                  layer=i, tm=tm, tf=t["tf"], tiles_per_batch=tiles_per_batch,
                  n_tok_out=n_tok if i == depth - 1 else None)
        xs = hs.reshape(B, L, D) if i < depth - 1 else hs
    return xs
```

```python
import functools
import math

import jax
import jax.numpy as jnp
import numpy as np
from jax import lax
from jax.experimental import pallas as pl
from jax.experimental.pallas import tpu as pltpu

F32 = jnp.float32
BF16 = jnp.bfloat16

D_MODEL = 2048
N_HEADS = 8
N_KV_HEADS = 2
HEAD_DIM = 128
GQA_GROUP = N_HEADS // N_KV_HEADS
ATTN_WIDTH = N_HEADS * HEAD_DIM
KV_WIDTH = N_KV_HEADS * HEAD_DIM
N_FOURIER_GROUPS = 8
FOURIER_GROUP_CH = 128
FOURIER_WIDTH = N_FOURIER_GROUPS * FOURIER_GROUP_CH
QKVF_WIDTH = ATTN_WIDTH + 2 * KV_WIDTH + FOURIER_WIDTH
OFF_K = ATTN_WIDTH
OFF_V = OFF_K + KV_WIDTH
OFF_F = OFF_V + KV_WIDTH
D_FF = 5632
N_META = 16
GRID_W = 64
NORM_EPS = 1e-6
ROPE_THETA = 10000.0
ROPE_PAIRS = HEAD_DIM // 4

DFT_N2 = 80
ROW_TILES_PER_BATCH = 25
PACK_ROWS = 16
LANES = 128
CONV_HALO = PACK_ROWS
MASK_VALUE = -1e30
VMEM_LIMIT = 56 * 1024 * 1024


def _cparams(*sem):
    return pltpu.CompilerParams(dimension_semantics=sem, vmem_limit_bytes=VMEM_LIMIT)


def _rms(x):
    return x * lax.rsqrt(jnp.mean(x * x, axis=-1, keepdims=True) + NORM_EPS)


def _round_up(n, m):
    return -(-n // m) * m


def _meta_rows(x_ref, meta_ref, tiles_per_batch):
    split = x_ref.shape[0] - N_META
    last = pl.program_id(0) % tiles_per_batch == tiles_per_batch - 1
    return jnp.where(last, meta_ref[...], x_ref[split:, :])


def _inproj_kernel(x_ref, meta_ref, g_ref, w_ref, qn_ref, kn_ref, cos_ref, sin_ref,
                   q_ref, k_ref, v_ref, f_ref, h_ref, *, raw_input, tiles_per_batch):
    g = g_ref[...]
    if raw_input:
        split = x_ref.shape[0] - N_META
        h_ref[:split, :] = (_rms(x_ref[:split, :]) * g).astype(BF16)
        h_ref[split:, :] = (_rms(_meta_rows(x_ref, meta_ref, tiles_per_batch)) * g).astype(BF16)
    else:
        h_ref[...] = (_rms(x_ref[...]) * g).astype(BF16)
    proj = jnp.dot(h_ref[...], w_ref[...], preferred_element_type=F32)
    cos = cos_ref[...]
    sin = sin_ref[...]
    lane = lax.broadcasted_iota(jnp.int32, cos.shape, 1)
    first_half = (lane % (HEAD_DIM // 2)) < ROPE_PAIRS

    def norm_rope(t, gain):
        t = _rms(t) * gain
        partner = jnp.where(first_half,
                            pltpu.roll(t, HEAD_DIM - ROPE_PAIRS, 1),
                            pltpu.roll(t, ROPE_PAIRS, 1))
        return t * cos + partner * sin

    scale = math.log2(math.e) / math.sqrt(HEAD_DIM)
    for hh in range(N_HEADS):
        c = hh * HEAD_DIM
        q_ref[:, c:c + HEAD_DIM] = (norm_rope(proj[:, c:c + HEAD_DIM], qn_ref[...]) * scale).astype(BF16)
    for hh in range(N_KV_HEADS):
        c = hh * HEAD_DIM
        k_ref[:, c:c + HEAD_DIM] = norm_rope(proj[:, OFF_K + c:OFF_K + c + HEAD_DIM], kn_ref[...]).astype(BF16)
    v_ref[...] = proj[:, OFF_V:OFF_F].astype(BF16)
    f_ref[...] = proj[:, OFF_F:].astype(BF16)


def _inproj(xs, meta, gain, w_in, qn, kn, cos_t, sin_t, *, layer, tm, tiles_per_batch, raw_input):
    R = xs.shape[0] * tiles_per_batch * tm
    row = lambda i: (i, 0)
    fixed = lambda i: (0, 0)
    tab = lambda i: (i % tiles_per_batch, 0)
    widths = (ATTN_WIDTH, KV_WIDTH, KV_WIDTH, FOURIER_WIDTH, D_MODEL)
    return pl.pallas_call(
        functools.partial(_inproj_kernel, raw_input=raw_input, tiles_per_batch=tiles_per_batch),
        grid=(R // tm,),
        in_specs=[
            pl.BlockSpec((pl.Squeezed(), tm, D_MODEL), lambda i: (i // tiles_per_batch, i % tiles_per_batch, 0)),
            pl.BlockSpec((N_META, D_MODEL), fixed),
            pl.BlockSpec((1, D_MODEL), fixed),
            pl.BlockSpec((pl.Squeezed(), D_MODEL, QKVF_WIDTH), lambda i: (layer, 0, 0)),
            pl.BlockSpec((1, HEAD_DIM), fixed),
            pl.BlockSpec((1, HEAD_DIM), fixed),
            pl.BlockSpec((tm, HEAD_DIM), tab),
            pl.BlockSpec((tm, HEAD_DIM), tab),
        ],
        out_specs=[pl.BlockSpec((tm, w), row) for w in widths],
        out_shape=[jax.ShapeDtypeStruct((R, w), BF16) for w in widths],
        compiler_params=_cparams("parallel"),
        name="inproj",
    )(xs, meta, gain, w_in, qn, kn, cos_t, sin_t)


def _attn_kernel(q_ref, k_ref, v_ref, o_ref, vt_ref, qt_ref, s_ref, mt_ref, m_ref, l_ref, acc_ref,
                 *, L, tk, n_full, tail, tail_pad):
    has_tail = tail > 0
    n_stage = n_full + (1 if has_tail else 0)
    off = 1 if has_tail else 0
    tail_start = n_full * tk

    def padded_tail(ref):
        rows = ref[0, tail_start:L, :]
        if tail_pad == tail:
            return rows
        return jnp.concatenate([rows, jnp.zeros((tail_pad - tail, HEAD_DIM), rows.dtype)], axis=0)

    @pl.when(pl.program_id(2) == 0)
    def _():
        for j in range(n_full):
            vt_ref[:, j * tk:(j + 1) * tk] = v_ref[0, j * tk:(j + 1) * tk, :].astype(F32).T.astype(BF16)
        if has_tail:
            vt_ref[:, tail_start:tail_start + tail_pad] = padded_tail(v_ref).astype(F32).T.astype(BF16)

    m_ref[...] = jnp.full(m_ref.shape, MASK_VALUE, F32)
    l_ref[...] = jnp.zeros(l_ref.shape, F32)
    acc_ref[...] = jnp.zeros(acc_ref.shape, F32)
    for g in range(GQA_GROUP):
        qt_ref[g] = q_ref[0, :, g * HEAD_DIM:(g + 1) * HEAD_DIM].astype(F32).T.astype(BF16)

    def is_tail(t):
        return has_tail and isinstance(t, int) and t == n_stage - 1

    def tile_start(t):
        if isinstance(t, int):
            return tail_start if is_tail(t) else t * tk
        return pl.multiple_of(t * tk, tk)

    def scores(t, slot):
        rows = tail_pad if is_tail(t) else tk
        kj = padded_tail(k_ref) if is_tail(t) else k_ref[0, pl.ds(tile_start(t), tk), :]
        for g in range(GQA_GROUP):
            st = jnp.dot(kj, qt_ref[g], preferred_element_type=F32)
            if is_tail(t) and tail_pad != tail:
                key = lax.broadcasted_iota(jnp.int32, st.shape, 0)
                st = jnp.where(key < tail, st, MASK_VALUE)
            s_ref[slot, g, 0:rows, :] = st
            mt_ref[slot, g] = jnp.max(st, axis=0, keepdims=True)

    def softmax_pv(t, slot):
        rows = tail_pad if is_tail(t) else tk
        vtj = vt_ref[:, pl.ds(tile_start(t), rows)]
        for g in range(GQA_GROUP):
            st = s_ref[slot, g, 0:rows, :]
            m_prev = m_ref[g]
            m_new = jnp.maximum(m_prev, mt_ref[slot, g])
            alpha = jnp.exp2(m_prev - m_new)
            p = jnp.exp2(st - m_new)
            l_ref[g] = alpha * l_ref[g] + jnp.sum(p, axis=0, keepdims=True)
            acc_ref[g] = alpha * acc_ref[g] + jnp.dot(vtj, p.astype(BF16), preferred_element_type=F32)
            m_ref[g] = m_new

    def stage(t, slot):
        if not isinstance(t, int) or t + 1 < n_stage:
            scores(t + 1, 1 - slot)
        softmax_pv(t, slot)

    scores(0, 0)
    stage(0, 0)
    n_pairs = max(0, (n_stage - 2 - off) // 2)
    if n_pairs > 0:
        def body(p, carry):
            stage(1 + 2 * p, 1)
            stage(2 + 2 * p, 0)
            return carry
        lax.fori_loop(0, n_pairs, body, 0)
    for t in range(1 + 2 * n_pairs, n_stage):
        stage(t, t % 2)
    for g in range(GQA_GROUP):
        c = g * HEAD_DIM
        o_ref[0, :, c:c + HEAD_DIM] = (acc_ref[g] / l_ref[g]).T.astype(BF16)


def _attention(q, k, v, *, tq, tk):
    B, L, _ = q.shape
    gw = GQA_GROUP * HEAD_DIM
    n_full = L // tk
    tail = L - n_full * tk
    tail_pad = _round_up(tail, LANES)
    kern = functools.partial(_attn_kernel, L=L, tk=tk, n_full=n_full, tail=tail, tail_pad=tail_pad)
    kv_spec = pl.BlockSpec((1, L, HEAD_DIM), lambda b, h, i: (b, 0, h))
    return pl.pallas_call(
        kern,
        grid=(B, N_KV_HEADS, pl.cdiv(L, tq)),
        in_specs=[pl.BlockSpec((1, tq, gw), lambda b, h, i: (b, i, h)), kv_spec, kv_spec],
        out_specs=pl.BlockSpec((1, tq, gw), lambda b, h, i: (b, i, h)),
        out_shape=jax.ShapeDtypeStruct((B, L, ATTN_WIDTH), BF16),
        scratch_shapes=[
            pltpu.VMEM((HEAD_DIM, n_full * tk + tail_pad), BF16),
            pltpu.VMEM((GQA_GROUP, HEAD_DIM, tq), BF16),
            pltpu.VMEM((2, GQA_GROUP, max(tk, tail_pad), tq), F32),
            pltpu.VMEM((2, GQA_GROUP, 1, tq), F32),
            pltpu.VMEM((GQA_GROUP, 1, tq), F32),
            pltpu.VMEM((GQA_GROUP, 1, tq), F32),
            pltpu.VMEM((GQA_GROUP, HEAD_DIM, tq), F32),
        ],
        compiler_params=_cparams("parallel", "parallel", "arbitrary"),
        name="attention",
    )(q, k, v)


def _fourier_kernel(f_ref, cs_ref, f1r_ref, f1i_ref, g_ref, o_ref, wr_ref, wi_ref, os_ref,
                    *, L, n1, n1p, chunk):
    n_chunks = L // chunk
    last = L - chunk
    split = chunk - N_META

    def channel_dft_rows(r0, rows):
        return jnp.dot(f_ref[0, pl.ds(r0, rows), :], cs_ref[...], preferred_element_type=F32)

    def channel_dft(c, carry):
        r0 = pl.multiple_of(c * chunk, PACK_ROWS)
        z = channel_dft_rows(r0, chunk)
        wr_ref[pl.ds(r0 + N_META, chunk), :] = z[:, :FOURIER_GROUP_CH]
        wi_ref[pl.ds(r0 + N_META, chunk), :] = z[:, FOURIER_GROUP_CH:]
        return carry

    lax.fori_loop(0, n_chunks - 1, channel_dft, 0, unroll=4 if (n_chunks - 1) % 4 == 0 else 1)
    z = channel_dft_rows(last, chunk)
    wr_ref[last + N_META:L, :] = z[:split, :FOURIER_GROUP_CH]
    wi_ref[last + N_META:L, :] = z[:split, FOURIER_GROUP_CH:]
    wr_ref[0:N_META, :] = z[split:, :FOURIER_GROUP_CH]
    wi_ref[0:N_META, :] = z[split:, FOURIER_GROUP_CH:]
    if n1p > n1:
        zeros = jnp.zeros(((n1p - n1) * DFT_N2, FOURIER_GROUP_CH), F32)
        wr_ref[L:, :] = zeros
        wi_ref[L:, :] = zeros

    def row_dft(n2, carry):
        rows = pl.ds(n2, n1p, stride=DFT_N2)
        zr = wr_ref[rows, :].astype(BF16)
        zi = wi_ref[rows, :].astype(BF16)
        a = (jnp.dot(f1r_ref[...], zr, preferred_element_type=F32)
             + jnp.dot(f1i_ref[...], zi, preferred_element_type=F32))
        wr_ref[rows, :] = a[:n1p]
        wi_ref[rows, :] = a[n1p:]
        return carry

    lax.fori_loop(0, DFT_N2, row_dft, 0, unroll=16)

    def col_dft(k1, carry):
        r0 = pl.multiple_of(k1 * DFT_N2, DFT_N2)
        a = jnp.concatenate([wr_ref[pl.ds(r0, DFT_N2), :].astype(BF16),
                             wi_ref[pl.ds(r0, DFT_N2), :].astype(BF16)], axis=0)
        os_ref[pl.ds(k1, DFT_N2, stride=n1), :] = jnp.dot(g_ref[k1], a, preferred_element_type=F32)
        return carry

    lax.fori_loop(0, n1, col_dft, 0, unroll=(41 if n1 % 41 == 0 else 5) if n1 % 5 == 0 else 1)

    def emit(c, carry):
        r0 = pl.multiple_of(c * chunk, PACK_ROWS)
        o_ref[0, pl.ds(r0, chunk), :] = os_ref[pl.ds(r0 + N_META, chunk), :].astype(BF16)
        return carry

    lax.fori_loop(0, n_chunks - 1, emit, 0)
    o_ref[0, last:L - N_META, :] = os_ref[last + N_META:L, :].astype(BF16)
    o_ref[0, L - N_META:L, :] = os_ref[0:N_META, :].astype(BF16)


def _fourier(f, consts, *, chunk):
    B, L, _ = f.shape
    n1 = L // DFT_N2
    n1p = _round_up(n1, 8)
    cs, f1r, f1i, g = consts
    slab = pl.BlockSpec((1, L, FOURIER_GROUP_CH), lambda b, c: (b, 0, c))
    fixed2 = lambda b, c: (0, 0)
    return pl.pallas_call(
        functools.partial(_fourier_kernel, L=L, n1=n1, n1p=n1p, chunk=chunk),
        grid=(B, N_FOURIER_GROUPS),
        in_specs=[
            slab,
            pl.BlockSpec((FOURIER_GROUP_CH, 2 * FOURIER_GROUP_CH), fixed2),
            pl.BlockSpec((2 * n1p, n1p), fixed2),
            pl.BlockSpec((2 * n1p, n1p), fixed2),
            pl.BlockSpec((n1, DFT_N2, 2 * DFT_N2), lambda b, c: (0, 0, 0), pipeline_mode=pl.Buffered(1)),
        ],
        out_specs=slab,
        out_shape=jax.ShapeDtypeStruct((B, L, FOURIER_WIDTH), BF16),
        scratch_shapes=[pltpu.VMEM((n1p * DFT_N2, FOURIER_GROUP_CH), F32)] * 2
        + [pltpu.VMEM((L, FOURIER_GROUP_CH), F32)],
        compiler_params=_cparams("parallel", "parallel"),
        name="fourier",
    )(f, cs, f1r, f1i, g)


def _dft_constants(L):
    n1 = L // DFT_N2
    n1p = _round_up(n1, 8)
    k1 = np.arange(n1)
    ang1 = 2.0 * np.pi * np.outer(k1, k1) / n1
    c1, s1 = np.cos(ang1), np.sin(ang1)
    f1r = np.zeros((2 * n1p, n1p))
    f1i = np.zeros((2 * n1p, n1p))
    f1r[:n1, :n1] = c1
    f1r[n1p:n1p + n1, :n1] = -s1
    f1i[:n1, :n1] = s1
    f1i[n1p:n1p + n1, :n1] = c1
    n2 = np.arange(DFT_N2)
    k = k1[:, None, None] + n1 * n2[None, :, None]
    ang = 2.0 * np.pi * ((k * n2[None, None, :]) % L) / L
    g = np.concatenate([np.cos(ang), np.sin(ang)], axis=2) / math.sqrt(L)
    c = np.arange(FOURIER_GROUP_CH)
    angc = 2.0 * np.pi * np.outer(c, c) / FOURIER_GROUP_CH
    cs = np.concatenate([np.cos(angc), -np.sin(angc)], axis=1) / math.sqrt(FOURIER_GROUP_CH)
    return (jnp.asarray(cs, BF16), jnp.asarray(f1r, BF16), jnp.asarray(f1i, BF16), jnp.asarray(g, BF16))


def _mix_kernel(x_ref, meta_ref, h_ref, attn_ref, four_ref, wga_ref, wgf_ref, bga_ref, bgf_ref,
                wa_ref, wf_ref, wo_ref, o_ref, *, raw_input, tiles_per_batch):
    @pl.when(pl.program_id(1) == 0)
    def _():
        if raw_input:
            split = x_ref.shape[0] - N_META
            o_ref[:split, :] = x_ref[:split, :]
            o_ref[split:, :] = _meta_rows(x_ref, meta_ref, tiles_per_batch)
        else:
            o_ref[...] = x_ref[...]

    h = h_ref[...]
    g_attn = jax.nn.sigmoid(jnp.dot(h, wga_ref[...], preferred_element_type=F32) + bga_ref[...])
    g_four = jax.nn.sigmoid(jnp.dot(h, wgf_ref[...], preferred_element_type=F32) + bgf_ref[...])
    a_br = jnp.dot(attn_ref[...], wa_ref[...], preferred_element_type=F32)
    s_br = jnp.dot(four_ref[...], wf_ref[...], preferred_element_type=F32)
    merged = (g_attn * a_br + g_four * s_br).astype(BF16)
    o_ref[...] += jnp.dot(merged, wo_ref[...], preferred_element_type=F32)


def _mix(xs, meta, h, attn, four, w_in, b_gates, w_a, w_f, w_o, *, layer, tm, tn, tiles_per_batch, raw_input):
    R = h.shape[0]
    nj = D_MODEL // tn
    g0 = QKVF_WIDTH // tn
    assert QKVF_WIDTH % tn == 0
    row = lambda i, j: (i, 0)
    return pl.pallas_call(
        functools.partial(_mix_kernel, raw_input=raw_input, tiles_per_batch=tiles_per_batch),
        grid=(R // tm, nj),
        in_specs=[
            pl.BlockSpec((pl.Squeezed(), tm, D_MODEL),
                         lambda i, j: (i // tiles_per_batch, i % tiles_per_batch, 0)),
            pl.BlockSpec((N_META, D_MODEL), lambda i, j: (0, 0)),
            pl.BlockSpec((tm, D_MODEL), row),
            pl.BlockSpec((tm, ATTN_WIDTH), row),
            pl.BlockSpec((tm, FOURIER_WIDTH), row),
            pl.BlockSpec((pl.Squeezed(), D_MODEL, tn), lambda i, j: (layer, 0, g0 + j)),
            pl.BlockSpec((pl.Squeezed(), D_MODEL, tn), lambda i, j: (layer, 0, g0 + nj + j)),
            pl.BlockSpec((1, tn), lambda i, j: (0, j)),
            pl.BlockSpec((1, tn), lambda i, j: (0, j + nj)),
            pl.BlockSpec((pl.Squeezed(), ATTN_WIDTH, tn), lambda i, j: (layer, 0, j)),
            pl.BlockSpec((pl.Squeezed(), FOURIER_WIDTH, tn), lambda i, j: (layer, 0, j)),
            pl.BlockSpec((pl.Squeezed(), tn, D_MODEL), lambda i, j: (layer, j, 0)),
        ],
        out_specs=pl.BlockSpec((tm, D_MODEL), row),
        out_shape=jax.ShapeDtypeStruct((R, D_MODEL), F32),
        compiler_params=_cparams("parallel", "arbitrary"),
        name="mix_out",
    )(xs, meta, h, attn, four, w_in, w_in, b_gates, b_gates, w_a, w_f, w_o)


def _ffn_kernel(x_ref, xp_ref, xn_ref, g_ref, wg_ref, wv_ref, wc_ref, bc_ref, wd_ref,
                o_ref, h_ref, *, tm, tiles_per_batch):
    j = pl.program_id(1)
    H = CONV_HALO

    @pl.when(j == 0)
    def _():
        x = x_ref[...]
        g = g_ref[...]
        h_ref[H:H + tm, :] = (_rms(x) * g).astype(BF16)
        h_ref[0:H, :] = (_rms(xp_ref[...]) * g).astype(BF16)
        h_ref[H + tm:, :] = (_rms(xn_ref[...]) * g).astype(BF16)
        o_ref[...] = x

    h = h_ref[...]
    up_g = jnp.dot(h, wg_ref[...], preferred_element_type=F32)
    n = tm + 2 * H
    wc = wc_ref[...]
    prev = pltpu.roll(up_g, 1, 0)[H:H + tm]
    nxt = pltpu.roll(up_g, n - 1, 0)[H:H + tm]
    row = lax.broadcasted_iota(jnp.int32, (tm, 1), 0)
    seam = jnp.where(pl.program_id(0) % tiles_per_batch == tiles_per_batch - 1, tm - N_META, -1)
    prev = jnp.where(row == seam, 0.0, prev)
    nxt = jnp.where(row == seam - 1, 0.0, nxt)
    u = prev * wc[0:1] + up_g[H:H + tm] * wc[1:2] + nxt * wc[2:3] + bc_ref[...]
    val = jnp.dot(h[H:H + tm], wv_ref[...], preferred_element_type=F32)
    act = (u * jax.nn.sigmoid(u) * val).astype(BF16)
    o_ref[...] += jnp.dot(act, wd_ref[...], preferred_element_type=F32)


def _ffn(hs, gain, w_up, w_conv, b_conv, w_down, *, layer, tm, tf, tiles_per_batch, n_tok_out=None):
    R = hs.shape[0]
    nj = D_FF // tf
    hb = tm // CONV_HALO
    nb = tiles_per_batch * hb
    row = lambda i, j: (i, 0)

    def prev_block(i, j):
        first = i % tiles_per_batch == 0
        return (jnp.where(first, (i // tiles_per_batch) * nb + nb - 1, i * hb - 1), 0)

    def next_block(i, j):
        last = i % tiles_per_batch == tiles_per_batch - 1
        return (jnp.where(last, (i // tiles_per_batch) * nb, (i + 1) * hb), 0)

    if n_tok_out is None:
        out_spec = pl.BlockSpec((tm, D_MODEL), row)
        out_shape = jax.ShapeDtypeStruct((R, D_MODEL), F32)
    else:
        out_spec = pl.BlockSpec((pl.Squeezed(), tm, D_MODEL),
                                lambda i, j: (i // tiles_per_batch, i % tiles_per_batch, 0))
        out_shape = jax.ShapeDtypeStruct((R // (tiles_per_batch * tm), n_tok_out, D_MODEL), F32)
    return pl.pallas_call(
        functools.partial(_ffn_kernel, tm=tm, tiles_per_batch=tiles_per_batch),
        grid=(R // tm, nj),
        in_specs=[
            pl.BlockSpec((tm, D_MODEL), row),
            pl.BlockSpec((CONV_HALO, D_MODEL), prev_block),
            pl.BlockSpec((CONV_HALO, D_MODEL), next_block),
            pl.BlockSpec((1, D_MODEL), lambda i, j: (0, 0)),
            pl.BlockSpec((pl.Squeezed(), D_MODEL, tf), lambda i, j: (layer, 0, j)),
            pl.BlockSpec((pl.Squeezed(), D_MODEL, tf), lambda i, j: (layer, 0, j + nj)),
            pl.BlockSpec((3, tf), lambda i, j: (0, j)),
            pl.BlockSpec((1, tf), lambda i, j: (0, j)),
            pl.BlockSpec((pl.Squeezed(), tf, D_MODEL), lambda i, j: (layer, j, 0)),
        ],
        out_specs=out_spec,
        out_shape=out_shape,
        scratch_shapes=[pltpu.VMEM((tm + 2 * CONV_HALO, D_MODEL), BF16)],
        compiler_params=_cparams("parallel", "arbitrary"),
        name="ffn",
    )(hs, hs, hs, gain, w_up, w_up, w_conv, b_conv, w_down)


def _rope_tables(n_tok):
    rows = n_tok // GRID_W
    pos_r = np.concatenate([np.repeat(np.arange(rows), GRID_W), np.zeros(N_META)])
    pos_c = np.concatenate([np.tile(np.arange(GRID_W), rows), np.zeros(N_META)])
    inv_freq = 1.0 / (ROPE_THETA ** (np.arange(ROPE_PAIRS) / ROPE_PAIRS))
    ang_r = pos_r[:, None] * inv_freq[None, :]
    ang_c = pos_c[:, None] * inv_freq[None, :]
    cos_t = np.concatenate([np.cos(ang_r), np.cos(ang_r), np.cos(ang_c), np.cos(ang_c)], axis=-1)
    sin_t = np.concatenate([-np.sin(ang_r), np.sin(ang_r), -np.sin(ang_c), np.sin(ang_c)], axis=-1)
    return jnp.asarray(cos_t, F32), jnp.asarray(sin_t, F32)


def _tiles(L):
    tm = L // ROW_TILES_PER_BATCH
    return dict(tm=tm, tq=256, tk=2560, tn_mix=512, tf=512)


def kernel(x, meta_tokens, norm_mix, norm_ffn, w_in, b_gate, q_norm, k_norm, w_attn_br, w_four, w_out,
           w_up, w_conv, b_conv, w_down):
    B, n_tok, D = x.shape
    depth = w_in.shape[0]
    L = n_tok + N_META
    t = _tiles(L)
    tm = t["tm"]
    assert D == D_MODEL and n_tok % GRID_W == 0 and L % DFT_N2 == 0
    assert L % tm == 0 and tm % PACK_ROWS == 0
    R = B * L
    tiles_per_batch = L // tm

    cos_t, sin_t = _rope_tables(n_tok)
    dft = _dft_constants(L)

    meta = meta_tokens.astype(x.dtype)
    w_in16, w_a16, w_f16, w_o16, w_up16, w_dn16 = (
        w.astype(BF16) for w in (w_in, w_attn_br, w_four, w_out, w_up, w_down))
    xs = x
    for i in range(depth):
        raw = i == 0
        q, k, v, f, h = _inproj(xs, meta, norm_mix[i][None], w_in16, q_norm[i][None], k_norm[i][None],
                                cos_t, sin_t, layer=i, tm=tm, tiles_per_batch=tiles_per_batch, raw_input=raw)
        attn = _attention(q.reshape(B, L, ATTN_WIDTH), k.reshape(B, L, KV_WIDTH), v.reshape(B, L, KV_WIDTH),
                          tq=t["tq"], tk=t["tk"]).reshape(R, ATTN_WIDTH)
        four = _fourier(f.reshape(B, L, FOURIER_WIDTH), dft, chunk=tm).reshape(R, FOURIER_WIDTH)
        hs = _mix(xs, meta, h, attn, four, w_in16, b_gate[i][None], w_a16, w_f16, w_o16,
                  layer=i, tm=tm, tn=t["tn_mix"], tiles_per_batch=tiles_per_batch, raw_input=raw)
        hs = _ffn(hs, norm_ffn[i][None], w_up16, w_conv[i], b_conv[i][None], w_dn16,
                  layer=i, tm=tm, tf=t["tf"], tiles_per_batch=tiles_per_batch,
                  n_tok_out=n_tok if i == depth - 1 else None)
        xs = hs.reshape(B, L, D) if i < depth - 1 else hs
    return xs
```

```python
import functools
import math

import jax
import jax.numpy as jnp
import numpy as np
from jax import lax
from jax.experimental import pallas as pl
from jax.experimental.pallas import tpu as pltpu

F32 = jnp.float32
BF16 = jnp.bfloat16

D_MODEL = 2048
N_HEADS = 8
N_KV_HEADS = 2
HEAD_DIM = 128
GQA_GROUP = N_HEADS // N_KV_HEADS
ATTN_WIDTH = N_HEADS * HEAD_DIM
KV_WIDTH = N_KV_HEADS * HEAD_DIM
N_FOURIER_GROUPS = 8
FOURIER_GROUP_CH = 128
FOURIER_WIDTH = N_FOURIER_GROUPS * FOURIER_GROUP_CH
QKVF_WIDTH = ATTN_WIDTH + 2 * KV_WIDTH + FOURIER_WIDTH
OFF_K = ATTN_WIDTH
OFF_V = OFF_K + KV_WIDTH
OFF_F = OFF_V + KV_WIDTH
D_FF = 5632
N_META = 16
GRID_W = 64
NORM_EPS = 1e-6
ROPE_THETA = 10000.0
ROPE_PAIRS = HEAD_DIM // 4

DFT_N2 = 80
ROW_TILES_PER_BATCH = 25
PACK_ROWS = 16
LANES = 128
CONV_HALO = PACK_ROWS
MASK_VALUE = -1e30
VMEM_LIMIT = 56 * 1024 * 1024


def _cparams(*sem):
    return pltpu.CompilerParams(dimension_semantics=sem, vmem_limit_bytes=VMEM_LIMIT)


def _rms(x):
    return x * lax.rsqrt(jnp.mean(x * x, axis=-1, keepdims=True) + NORM_EPS)


def _round_up(n, m):
    return -(-n // m) * m


def _meta_rows(x_ref, meta_ref, tiles_per_batch):
    split = x_ref.shape[0] - N_META
    last = pl.program_id(0) % tiles_per_batch == tiles_per_batch - 1
    return jnp.where(last, meta_ref[...], x_ref[split:, :])


def _inproj_kernel(x_ref, meta_ref, g_ref, w_ref, qn_ref, kn_ref, cos_ref, sin_ref,
                   q_ref, k_ref, v_ref, f_ref, h_ref, *, raw_input, tiles_per_batch):
    g = g_ref[...]
    if raw_input:
        split = x_ref.shape[0] - N_META
        h_ref[:split, :] = (_rms(x_ref[:split, :]) * g).astype(BF16)
        h_ref[split:, :] = (_rms(_meta_rows(x_ref, meta_ref, tiles_per_batch)) * g).astype(BF16)
    else:
        h_ref[...] = (_rms(x_ref[...]) * g).astype(BF16)
    proj = jnp.dot(h_ref[...], w_ref[...], preferred_element_type=F32)
    cos = cos_ref[...]
    sin = sin_ref[...]
    lane = lax.broadcasted_iota(jnp.int32, cos.shape, 1)
    first_half = (lane % (HEAD_DIM // 2)) < ROPE_PAIRS

    def norm_rope(t, gain):
        t = _rms(t) * gain
        partner = jnp.where(first_half,
                            pltpu.roll(t, HEAD_DIM - ROPE_PAIRS, 1),
                            pltpu.roll(t, ROPE_PAIRS, 1))
        return t * cos + partner * sin

    scale = math.log2(math.e) / math.sqrt(HEAD_DIM)
    for hh in range(N_HEADS):
        c = hh * HEAD_DIM
        q_ref[:, c:c + HEAD_DIM] = (norm_rope(proj[:, c:c + HEAD_DIM], qn_ref[...]) * scale).astype(BF16)
    for hh in range(N_KV_HEADS):
        c = hh * HEAD_DIM
        k_ref[:, c:c + HEAD_DIM] = norm_rope(proj[:, OFF_K + c:OFF_K + c + HEAD_DIM], kn_ref[...]).astype(BF16)
    v_ref[...] = proj[:, OFF_V:OFF_F].astype(BF16)
    f_ref[...] = proj[:, OFF_F:].astype(BF16)


def _inproj(xs, meta, gain, w_in, qn, kn, cos_t, sin_t, *, layer, tm, tiles_per_batch, raw_input):
    R = xs.shape[0] * tiles_per_batch * tm
    row = lambda i: (i, 0)
    fixed = lambda i: (0, 0)
    tab = lambda i: (i % tiles_per_batch, 0)
    widths = (ATTN_WIDTH, KV_WIDTH, KV_WIDTH, FOURIER_WIDTH, D_MODEL)
    return pl.pallas_call(
        functools.partial(_inproj_kernel, raw_input=raw_input, tiles_per_batch=tiles_per_batch),
        grid=(R // tm,),
        in_specs=[
            pl.BlockSpec((pl.Squeezed(), tm, D_MODEL), lambda i: (i // tiles_per_batch, i % tiles_per_batch, 0)),
            pl.BlockSpec((N_META, D_MODEL), fixed),
            pl.BlockSpec((1, D_MODEL), fixed),
            pl.BlockSpec((pl.Squeezed(), D_MODEL, QKVF_WIDTH), lambda i: (layer, 0, 0)),
            pl.BlockSpec((1, HEAD_DIM), fixed),
            pl.BlockSpec((1, HEAD_DIM), fixed),
            pl.BlockSpec((tm, HEAD_DIM), tab),
            pl.BlockSpec((tm, HEAD_DIM), tab),
        ],
        out_specs=[pl.BlockSpec((tm, w), row) for w in widths],
        out_shape=[jax.ShapeDtypeStruct((R, w), BF16) for w in widths],
        compiler_params=_cparams("parallel"),
        name="inproj",
    )(xs, meta, gain, w_in, qn, kn, cos_t, sin_t)


def _attn_kernel(q_ref, k_ref, v_ref, o_ref, vt_ref, qt_ref, s_ref, mt_ref, m_ref, l_ref, acc_ref,
                 *, L, tk, n_full, tail, tail_pad):
    has_tail = tail > 0
    n_stage = n_full + (1 if has_tail else 0)
    off = 1 if has_tail else 0
    tail_start = n_full * tk

    def padded_tail(ref):
        rows = ref[0, tail_start:L, :]
        if tail_pad == tail:
            return rows
        return jnp.concatenate([rows, jnp.zeros((tail_pad - tail, HEAD_DIM), rows.dtype)], axis=0)

    @pl.when(pl.program_id(2) == 0)
    def _():
        for j in range(n_full):
            vt_ref[:, j * tk:(j + 1) * tk] = v_ref[0, j * tk:(j + 1) * tk, :].astype(F32).T.astype(BF16)
        if has_tail:
            vt_ref[:, tail_start:tail_start + tail_pad] = padded_tail(v_ref).astype(F32).T.astype(BF16)

    m_ref[...] = jnp.full(m_ref.shape, MASK_VALUE, F32)
    l_ref[...] = jnp.zeros(l_ref.shape, F32)
    acc_ref[...] = jnp.zeros(acc_ref.shape, F32)
    for g in range(GQA_GROUP):
        qt_ref[g] = q_ref[0, :, g * HEAD_DIM:(g + 1) * HEAD_DIM].astype(F32).T.astype(BF16)

    def is_tail(t):
        return has_tail and isinstance(t, int) and t == n_stage - 1

    def tile_start(t):
        if isinstance(t, int):
            return tail_start if is_tail(t) else t * tk
        return pl.multiple_of(t * tk, tk)

    def scores(t, slot):
        rows = tail_pad if is_tail(t) else tk
        kj = padded_tail(k_ref) if is_tail(t) else k_ref[0, pl.ds(tile_start(t), tk), :]
        for g in range(GQA_GROUP):
            st = jnp.dot(kj, qt_ref[g], preferred_element_type=F32)
            if is_tail(t) and tail_pad != tail:
                key = lax.broadcasted_iota(jnp.int32, st.shape, 0)
                st = jnp.where(key < tail, st, MASK_VALUE)
            s_ref[slot, g, 0:rows, :] = st
            mt_ref[slot, g] = jnp.max(st, axis=0, keepdims=True)

    def softmax_pv(t, slot):
        rows = tail_pad if is_tail(t) else tk
        vtj = vt_ref[:, pl.ds(tile_start(t), rows)]
        for g in range(GQA_GROUP):
            st = s_ref[slot, g, 0:rows, :]
            m_prev = m_ref[g]
            m_new = jnp.maximum(m_prev, mt_ref[slot, g])
            alpha = jnp.exp2(m_prev - m_new)
            p = jnp.exp2(st - m_new)
            l_ref[g] = alpha * l_ref[g] + jnp.sum(p, axis=0, keepdims=True)
            acc_ref[g] = alpha * acc_ref[g] + jnp.dot(vtj, p.astype(BF16), preferred_element_type=F32)
            m_ref[g] = m_new

    def stage(t, slot):
        if not isinstance(t, int) or t + 1 < n_stage:
            scores(t + 1, 1 - slot)
        softmax_pv(t, slot)

    scores(0, 0)
    stage(0, 0)
    n_pairs = max(0, (n_stage - 2 - off) // 2)
    if n_pairs > 0:
        def body(p, carry):
            stage(1 + 2 * p, 1)
            stage(2 + 2 * p, 0)
            return carry
        lax.fori_loop(0, n_pairs, body, 0)
    for t in range(1 + 2 * n_pairs, n_stage):
        stage(t, t % 2)
    for g in range(GQA_GROUP):
        c = g * HEAD_DIM
        o_ref[0, :, c:c + HEAD_DIM] = (acc_ref[g] / l_ref[g]).T.astype(BF16)


def _attention(q, k, v, *, tq, tk):
    B, L, _ = q.shape
    gw = GQA_GROUP * HEAD_DIM
    n_full = L // tk
    tail = L - n_full * tk
    tail_pad = _round_up(tail, LANES)
    kern = functools.partial(_attn_kernel, L=L, tk=tk, n_full=n_full, tail=tail, tail_pad=tail_pad)
    kv_spec = pl.BlockSpec((1, L, HEAD_DIM), lambda b, h, i: (b, 0, h))
    return pl.pallas_call(
        kern,
        grid=(B, N_KV_HEADS, pl.cdiv(L, tq)),
        in_specs=[pl.BlockSpec((1, tq, gw), lambda b, h, i: (b, i, h)), kv_spec, kv_spec],
        out_specs=pl.BlockSpec((1, tq, gw), lambda b, h, i: (b, i, h)),
        out_shape=jax.ShapeDtypeStruct((B, L, ATTN_WIDTH), BF16),
        scratch_shapes=[
            pltpu.VMEM((HEAD_DIM, n_full * tk + tail_pad), BF16),
            pltpu.VMEM((GQA_GROUP, HEAD_DIM, tq), BF16),
            pltpu.VMEM((2, GQA_GROUP, max(tk, tail_pad), tq), F32),
            pltpu.VMEM((2, GQA_GROUP, 1, tq), F32),
            pltpu.VMEM((GQA_GROUP, 1, tq), F32),
            pltpu.VMEM((GQA_GROUP, 1, tq), F32),
            pltpu.VMEM((GQA_GROUP, HEAD_DIM, tq), F32),
        ],
        compiler_params=_cparams("parallel", "parallel", "arbitrary"),
        name="attention",
    )(q, k, v)


def _fourier_kernel(f_ref, cs_ref, f1r_ref, f1i_ref, g_ref, o_ref, wr_ref, wi_ref, os_ref,
                    *, L, n1, n1p, chunk):
    n_chunks = L // chunk
    last = L - chunk
    split = chunk - N_META

    def channel_dft_rows(r0, rows):
        return jnp.dot(f_ref[0, pl.ds(r0, rows), :], cs_ref[...], preferred_element_type=F32)

    def channel_dft(c, carry):
        r0 = pl.multiple_of(c * chunk, PACK_ROWS)
        z = channel_dft_rows(r0, chunk)
        wr_ref[pl.ds(r0 + N_META, chunk), :] = z[:, :FOURIER_GROUP_CH]
        wi_ref[pl.ds(r0 + N_META, chunk), :] = z[:, FOURIER_GROUP_CH:]
        return carry

    lax.fori_loop(0, n_chunks - 1, channel_dft, 0, unroll=12 if (n_chunks - 1) % 12 == 0 else 1)
    z = channel_dft_rows(last, chunk)
    wr_ref[last + N_META:L, :] = z[:split, :FOURIER_GROUP_CH]
    wi_ref[last + N_META:L, :] = z[:split, FOURIER_GROUP_CH:]
    wr_ref[0:N_META, :] = z[split:, :FOURIER_GROUP_CH]
    wi_ref[0:N_META, :] = z[split:, FOURIER_GROUP_CH:]
    if n1p > n1:
        zeros = jnp.zeros(((n1p - n1) * DFT_N2, FOURIER_GROUP_CH), F32)
        wr_ref[L:, :] = zeros
        wi_ref[L:, :] = zeros

    def row_dft(n2, carry):
        rows = pl.ds(n2, n1p, stride=DFT_N2)
        zr = wr_ref[rows, :].astype(BF16)
        zi = wi_ref[rows, :].astype(BF16)
        a = (jnp.dot(f1r_ref[...], zr, preferred_element_type=F32)
             + jnp.dot(f1i_ref[...], zi, preferred_element_type=F32))
        wr_ref[rows, :] = a[:n1p]
        wi_ref[rows, :] = a[n1p:]
        return carry

    lax.fori_loop(0, DFT_N2, row_dft, 0, unroll=20)

    def col_dft(k1, carry):
        r0 = pl.multiple_of(k1 * DFT_N2, DFT_N2)
        a = jnp.concatenate([wr_ref[pl.ds(r0, DFT_N2), :].astype(BF16),
                             wi_ref[pl.ds(r0, DFT_N2), :].astype(BF16)], axis=0)
        os_ref[pl.ds(k1, DFT_N2, stride=n1), :] = jnp.dot(g_ref[k1], a, preferred_element_type=F32)
        return carry

    lax.fori_loop(0, n1, col_dft, 0, unroll=(41 if n1 % 41 == 0 else 5) if n1 % 5 == 0 else 1)

    def emit(c, carry):
        r0 = pl.multiple_of(c * chunk, PACK_ROWS)
        o_ref[0, pl.ds(r0, chunk), :] = os_ref[pl.ds(r0 + N_META, chunk), :].astype(BF16)
        return carry

    lax.fori_loop(0, n_chunks - 1, emit, 0)
    o_ref[0, last:L - N_META, :] = os_ref[last + N_META:L, :].astype(BF16)
    o_ref[0, L - N_META:L, :] = os_ref[0:N_META, :].astype(BF16)


def _fourier(f, consts, *, chunk):
    B, L, _ = f.shape
    n1 = L // DFT_N2
    n1p = _round_up(n1, 8)
    cs, f1r, f1i, g = consts
    slab = pl.BlockSpec((1, L, FOURIER_GROUP_CH), lambda b, c: (b, 0, c))
    fixed2 = lambda b, c: (0, 0)
    return pl.pallas_call(
        functools.partial(_fourier_kernel, L=L, n1=n1, n1p=n1p, chunk=chunk),
        grid=(B, N_FOURIER_GROUPS),
        in_specs=[
            slab,
            pl.BlockSpec((FOURIER_GROUP_CH, 2 * FOURIER_GROUP_CH), fixed2),
            pl.BlockSpec((2 * n1p, n1p), fixed2),
            pl.BlockSpec((2 * n1p, n1p), fixed2),
            pl.BlockSpec((n1, DFT_N2, 2 * DFT_N2), lambda b, c: (0, 0, 0), pipeline_mode=pl.Buffered(1)),
        ],
        out_specs=slab,
        out_shape=jax.ShapeDtypeStruct((B, L, FOURIER_WIDTH), BF16),
        scratch_shapes=[pltpu.VMEM((n1p * DFT_N2, FOURIER_GROUP_CH), F32)] * 2
        + [pltpu.VMEM((L, FOURIER_GROUP_CH), F32)],
        compiler_params=_cparams("parallel", "parallel"),
        name="fourier",
    )(f, cs, f1r, f1i, g)


def _dft_constants(L):
    n1 = L // DFT_N2
    n1p = _round_up(n1, 8)
    k1 = np.arange(n1)
    ang1 = 2.0 * np.pi * np.outer(k1, k1) / n1
    c1, s1 = np.cos(ang1), np.sin(ang1)
    f1r = np.zeros((2 * n1p, n1p))
    f1i = np.zeros((2 * n1p, n1p))
    f1r[:n1, :n1] = c1
    f1r[n1p:n1p + n1, :n1] = -s1
    f1i[:n1, :n1] = s1
    f1i[n1p:n1p + n1, :n1] = c1
    n2 = np.arange(DFT_N2)
    k = k1[:, None, None] + n1 * n2[None, :, None]
    ang = 2.0 * np.pi * ((k * n2[None, None, :]) % L) / L
    g = np.concatenate([np.cos(ang), np.sin(ang)], axis=2) / math.sqrt(L)
    c = np.arange(FOURIER_GROUP_CH)
    angc = 2.0 * np.pi * np.outer(c, c) / FOURIER_GROUP_CH
    cs = np.concatenate([np.cos(angc), -np.sin(angc)], axis=1) / math.sqrt(FOURIER_GROUP_CH)
    return (jnp.asarray(cs, BF16), jnp.asarray(f1r, BF16), jnp.asarray(f1i, BF16), jnp.asarray(g, BF16))


def _mix_kernel(x_ref, meta_ref, h_ref, attn_ref, four_ref, wga_ref, wgf_ref, bga_ref, bgf_ref,
                wa_ref, wf_ref, wo_ref, o_ref, *, raw_input, tiles_per_batch):
    @pl.when(pl.program_id(1) == 0)
    def _():
        if raw_input:
            split = x_ref.shape[0] - N_META
            o_ref[:split, :] = x_ref[:split, :]
            o_ref[split:, :] = _meta_rows(x_ref, meta_ref, tiles_per_batch)
        else:
            o_ref[...] = x_ref[...]

    h = h_ref[...]
    g_attn = jax.nn.sigmoid(jnp.dot(h, wga_ref[...], preferred_element_type=F32) + bga_ref[...])
    g_four = jax.nn.sigmoid(jnp.dot(h, wgf_ref[...], preferred_element_type=F32) + bgf_ref[...])
    a_br = jnp.dot(attn_ref[...], wa_ref[...], preferred_element_type=F32)
    s_br = jnp.dot(four_ref[...], wf_ref[...], preferred_element_type=F32)
    merged = (g_attn * a_br + g_four * s_br).astype(BF16)
    o_ref[...] += jnp.dot(merged, wo_ref[...], preferred_element_type=F32)


def _mix(xs, meta, h, attn, four, w_in, b_gates, w_a, w_f, w_o, *, layer, tm, tn, tiles_per_batch, raw_input):
    R = h.shape[0]
    nj = D_MODEL // tn
    g0 = QKVF_WIDTH // tn
    assert QKVF_WIDTH % tn == 0
    row = lambda i, j: (i, 0)
    return pl.pallas_call(
        functools.partial(_mix_kernel, raw_input=raw_input, tiles_per_batch=tiles_per_batch),
        grid=(R // tm, nj),
        in_specs=[
            pl.BlockSpec((pl.Squeezed(), tm, D_MODEL),
                         lambda i, j: (i // tiles_per_batch, i % tiles_per_batch, 0)),
            pl.BlockSpec((N_META, D_MODEL), lambda i, j: (0, 0)),
            pl.BlockSpec((tm, D_MODEL), row),
            pl.BlockSpec((tm, ATTN_WIDTH), row),
            pl.BlockSpec((tm, FOURIER_WIDTH), row),
            pl.BlockSpec((pl.Squeezed(), D_MODEL, tn), lambda i, j: (layer, 0, g0 + j)),
            pl.BlockSpec((pl.Squeezed(), D_MODEL, tn), lambda i, j: (layer, 0, g0 + nj + j)),
            pl.BlockSpec((1, tn), lambda i, j: (0, j)),
            pl.BlockSpec((1, tn), lambda i, j: (0, j + nj)),
            pl.BlockSpec((pl.Squeezed(), ATTN_WIDTH, tn), lambda i, j: (layer, 0, j)),
            pl.BlockSpec((pl.Squeezed(), FOURIER_WIDTH, tn), lambda i, j: (layer, 0, j)),
            pl.BlockSpec((pl.Squeezed(), tn, D_MODEL), lambda i, j: (layer, j, 0)),
        ],
        out_specs=pl.BlockSpec((tm, D_MODEL), row),
        out_shape=jax.ShapeDtypeStruct((R, D_MODEL), F32),
        compiler_params=_cparams("parallel", "arbitrary"),
        name="mix_out",
    )(xs, meta, h, attn, four, w_in, w_in, b_gates, b_gates, w_a, w_f, w_o)


def _ffn_kernel(x_ref, xp_ref, xn_ref, g_ref, wg_ref, wv_ref, wc_ref, bc_ref, wd_ref,
                o_ref, h_ref, *, tm, tiles_per_batch):
    j = pl.program_id(1)
    H = CONV_HALO

    @pl.when(j == 0)
    def _():
        x = x_ref[...]
        g = g_ref[...]
        h_ref[H:H + tm, :] = (_rms(x) * g).astype(BF16)
        h_ref[0:H, :] = (_rms(xp_ref[...]) * g).astype(BF16)
        h_ref[H + tm:, :] = (_rms(xn_ref[...]) * g).astype(BF16)
        o_ref[...] = x

    h = h_ref[...]
    up_g = jnp.dot(h, wg_ref[...], preferred_element_type=F32)
    n = tm + 2 * H
    wc = wc_ref[...]
    prev = pltpu.roll(up_g, 1, 0)[H:H + tm]
    nxt = pltpu.roll(up_g, n - 1, 0)[H:H + tm]
    row = lax.broadcasted_iota(jnp.int32, (tm, 1), 0)
    seam = jnp.where(pl.program_id(0) % tiles_per_batch == tiles_per_batch - 1, tm - N_META, -1)
    prev = jnp.where(row == seam, 0.0, prev)
    nxt = jnp.where(row == seam - 1, 0.0, nxt)
    u = prev * wc[0:1] + up_g[H:H + tm] * wc[1:2] + nxt * wc[2:3] + bc_ref[...]
    val = jnp.dot(h[H:H + tm], wv_ref[...], preferred_element_type=F32)
    act = (u * jax.nn.sigmoid(u) * val).astype(BF16)
    o_ref[...] += jnp.dot(act, wd_ref[...], preferred_element_type=F32)


def _ffn(hs, gain, w_up, w_conv, b_conv, w_down, *, layer, tm, tf, tiles_per_batch, n_tok_out=None):
    R = hs.shape[0]
    nj = D_FF // tf
    hb = tm // CONV_HALO
    nb = tiles_per_batch * hb
    row = lambda i, j: (i, 0)

    def prev_block(i, j):
        first = i % tiles_per_batch == 0
        return (jnp.where(first, (i // tiles_per_batch) * nb + nb - 1, i * hb - 1), 0)

    def next_block(i, j):
        last = i % tiles_per_batch == tiles_per_batch - 1
        return (jnp.where(last, (i // tiles_per_batch) * nb, (i + 1) * hb), 0)

    if n_tok_out is None:
        out_spec = pl.BlockSpec((tm, D_MODEL), row)
        out_shape = jax.ShapeDtypeStruct((R, D_MODEL), F32)
    else:
        out_spec = pl.BlockSpec((pl.Squeezed(), tm, D_MODEL),
                                lambda i, j: (i // tiles_per_batch, i % tiles_per_batch, 0))
        out_shape = jax.ShapeDtypeStruct((R // (tiles_per_batch * tm), n_tok_out, D_MODEL), F32)
    return pl.pallas_call(
        functools.partial(_ffn_kernel, tm=tm, tiles_per_batch=tiles_per_batch),
        grid=(R // tm, nj),
        in_specs=[
            pl.BlockSpec((tm, D_MODEL), row),
            pl.BlockSpec((CONV_HALO, D_MODEL), prev_block),
            pl.BlockSpec((CONV_HALO, D_MODEL), next_block),
            pl.BlockSpec((1, D_MODEL), lambda i, j: (0, 0)),
            pl.BlockSpec((pl.Squeezed(), D_MODEL, tf), lambda i, j: (layer, 0, j)),
            pl.BlockSpec((pl.Squeezed(), D_MODEL, tf), lambda i, j: (layer, 0, j + nj)),
            pl.BlockSpec((3, tf), lambda i, j: (0, j)),
            pl.BlockSpec((1, tf), lambda i, j: (0, j)),
            pl.BlockSpec((pl.Squeezed(), tf, D_MODEL), lambda i, j: (layer, j, 0)),
        ],
        out_specs=out_spec,
        out_shape=out_shape,
        scratch_shapes=[pltpu.VMEM((tm + 2 * CONV_HALO, D_MODEL), BF16)],
        compiler_params=_cparams("parallel", "arbitrary"),
        name="ffn",
    )(hs, hs, hs, gain, w_up, w_up, w_conv, b_conv, w_down)


def _rope_tables(n_tok):
    rows = n_tok // GRID_W
    pos_r = np.concatenate([np.repeat(np.arange(rows), GRID_W), np.zeros(N_META)])
    pos_c = np.concatenate([np.tile(np.arange(GRID_W), rows), np.zeros(N_META)])
    inv_freq = 1.0 / (ROPE_THETA ** (np.arange(ROPE_PAIRS) / ROPE_PAIRS))
    ang_r = pos_r[:, None] * inv_freq[None, :]
    ang_c = pos_c[:, None] * inv_freq[None, :]
    cos_t = np.concatenate([np.cos(ang_r), np.cos(ang_r), np.cos(ang_c), np.cos(ang_c)], axis=-1)
    sin_t = np.concatenate([-np.sin(ang_r), np.sin(ang_r), -np.sin(ang_c), np.sin(ang_c)], axis=-1)
    return jnp.asarray(cos_t, F32), jnp.asarray(sin_t, F32)


def _tiles(L):
    tm = L // ROW_TILES_PER_BATCH
    return dict(tm=tm, tq=256, tk=2560, tn_mix=512, tf=512)


def kernel(x, meta_tokens, norm_mix, norm_ffn, w_in, b_gate, q_norm, k_norm, w_attn_br, w_four, w_out,
           w_up, w_conv, b_conv, w_down):
    B, n_tok, D = x.shape
    depth = w_in.shape[0]
    L = n_tok + N_META
    t = _tiles(L)
    tm = t["tm"]
    assert D == D_MODEL and n_tok % GRID_W == 0 and L % DFT_N2 == 0
    assert L % tm == 0 and tm % PACK_ROWS == 0
    R = B * L
    tiles_per_batch = L // tm

    cos_t, sin_t = _rope_tables(n_tok)
    dft = _dft_constants(L)

    meta = meta_tokens.astype(x.dtype)
    w_in16, w_a16, w_f16, w_o16, w_up16, w_dn16 = (
        w.astype(BF16) for w in (w_in, w_attn_br, w_four, w_out, w_up, w_down))
    xs = x
    for i in range(depth):
        raw = i == 0
        q, k, v, f, h = _inproj(xs, meta, norm_mix[i][None], w_in16, q_norm[i][None], k_norm[i][None],
                                cos_t, sin_t, layer=i, tm=tm, tiles_per_batch=tiles_per_batch, raw_input=raw)
        attn = _attention(q.reshape(B, L, ATTN_WIDTH), k.reshape(B, L, KV_WIDTH), v.reshape(B, L, KV_WIDTH),
                          tq=t["tq"], tk=t["tk"]).reshape(R, ATTN_WIDTH)
        four = _fourier(f.reshape(B, L, FOURIER_WIDTH), dft, chunk=tm).reshape(R, FOURIER_WIDTH)
        hs = _mix(xs, meta, h, attn, four, w_in16, b_gate[i][None], w_a16, w_f16, w_o16,
                  layer=i, tm=tm, tn=t["tn_mix"], tiles_per_batch=tiles_per_batch, raw_input=raw)
        hs = _ffn(hs, norm_ffn[i][None], w_up16, w_conv[i], b_conv[i][None], w_dn16,
                  layer=i, tm=tm, tf=t["tf"], tiles_per_batch=tiles_per_batch,
                  n_tok_out=n_tok if i == depth - 1 else None)
        xs = hs.reshape(B, L, D) if i < depth - 1 else hs
    return xs
```
